```python
import jax
import jax.numpy as jnp
from jax import lax
import numpy as np

D_MODEL = 1024
BATCH = 16
SEQ = 2048
DEPTH = 2

CTX_LEN = 256
GRID_W = 64
D_MIX = D_MODEL
HEAD_DIM = 64
FT_GROUPS = 4
FT_GDIM = 64
FT_CH = FT_GROUPS * FT_GDIM
ATT_HQ = 8
ATT_HKV = 2
ATT_GROUP = ATT_HQ // ATT_HKV
ATT_CH = ATT_HQ * HEAD_DIM
DN_HEADS = 4
DN_DIM = 64
DN_CH = DN_HEADS * DN_DIM
CONV_K = 3
CONV_CH = 3 * DN_CH
CHUNK = 64
QBLK = 128
D_FF = 4 * D_MODEL
ROPE_THETA = 10000.0
AXIS_DIM = HEAD_DIM // 2
ROT_PAIRS = AXIS_DIM // 2
EPS = 1e-6
_S_F = FT_CH
_S_AQ = _S_F + ATT_CH
_S_AK = _S_AQ + ATT_HKV * HEAD_DIM
_S_AV = _S_AK + ATT_HKV * HEAD_DIM
_S_DQKV = _S_AV + CONV_CH
_S_DZ = _S_DQKV + DN_CH
PROJ = _S_DZ + 4 * DN_HEADS
SPLITS = (_S_F, _S_AQ, _S_AK, _S_AV, _S_DQKV, _S_DZ)

kernel_name = 'hybrid_fourier_gqa_gdn_prefix_block'


def _rms(x, g):
    xf = x.astype(jnp.float32)
    y = xf * lax.rsqrt(jnp.mean(xf * xf, axis=-1, keepdims=True) + EPS)
    return (y * g.astype(jnp.float32)).astype(x.dtype)


def _l2(x):
    xf = x.astype(jnp.float32)
    return xf * lax.rsqrt(jnp.sum(xf * xf, axis=-1, keepdims=True) + EPS)


def _rope2d(x, cos, sin):
    xs = x.reshape(x.shape[:-1] + (2, 2, ROT_PAIRS))
    rot = jnp.stack([-xs[..., 1, :], xs[..., 0, :]], axis=-2).reshape(x.shape)
    return (x.astype(jnp.float32) * cos + rot.astype(jnp.float32) * sin).astype(x.dtype)


def _dwconv(u, w):
    return lax.conv_general_dilated(u, w[:, None, :].astype(u.dtype), window_strides=(1,),
                                    padding=[(CONV_K // 2, CONV_K // 2)],
                                    dimension_numbers=('NWC', 'WIO', 'NWC'),
                                    feature_group_count=u.shape[-1])


def _fourier(f_in):
    B, L, _ = f_in.shape
    f = f_in.astype(jnp.float32).reshape(B, L, FT_GROUPS, FT_GDIM)
    y = jnp.fft.fftn(f, axes=(1, 3), norm='ortho').real
    return y.reshape(B, L, FT_CH).astype(f_in.dtype)


def _attend(q, k, v):
    B, L, _, D = q.shape
    nb = L // QBLK
    qb = q.reshape(B, nb, QBLK, ATT_HKV, ATT_GROUP, D).transpose(1, 0, 2, 3, 4, 5)
    scale = D ** -0.5

    def blk(qi):
        s = jnp.einsum('bqhgd,bkhd->bhgqk', qi, k, preferred_element_type=jnp.float32) * scale
        p = jax.nn.softmax(s, axis=-1).astype(v.dtype)
        return jnp.einsum('bhgqk,bkhd->bqhgd', p, v)

    o = lax.map(blk, qb)
    return o.transpose(1, 0, 2, 3, 4, 5).reshape(B, L, ATT_HQ * D)


def _chunk_gdn(q, k, v, g, beta, s0):
    B, L, H, DK = q.shape
    DV = v.shape[-1]
    N = L // CHUNK
    ch = lambda t: t.astype(jnp.float32).reshape((B, N, CHUNK, H) + t.shape[3:]).swapaxes(2, 3).swapaxes(0, 1)
    qc, kc, vc = ch(q), ch(k), ch(v)
    gc = jnp.cumsum(ch(g), axis=-1)
    bc = ch(beta)
    idx = jnp.arange(CHUNK)
    tril = idx[:, None] >= idx[None, :]
    strict = idx[:, None] > idx[None, :]
    decay = jnp.exp(jnp.where(tril, gc[..., :, None] - gc[..., None, :], -jnp.inf))
    kb = kc * bc[..., None]
    vb = vc * bc[..., None]
    lmat = jnp.where(strict, jnp.einsum('nbhid,nbhjd->nbhij', kb, kc) * decay, 0.0)
    eye = jnp.eye(CHUNK, dtype=jnp.float32)
    tmat = lax.linalg.triangular_solve(eye + lmat, jnp.broadcast_to(eye, lmat.shape),
                                       left_side=True, lower=True)
    u = jnp.einsum('nbhij,nbhjd->nbhid', tmat, vb)
    w = jnp.einsum('nbhij,nbhjd->nbhid', tmat, kb * jnp.exp(gc)[..., None])
    qk = jnp.where(tril, jnp.einsum('nbhid,nbhjd->nbhij', qc, kc) * decay, 0.0)

    def step(S, inp):
        q_i, k_i, u_i, w_i, g_i, qk_i = inp
        v_new = u_i - jnp.einsum('bhcd,bhde->bhce', w_i, S)
        o = (jnp.einsum('bhcd,bhde->bhce', q_i * jnp.exp(g_i)[..., None], S)
             + jnp.einsum('bhij,bhje->bhie', qk_i, v_new))
        g_last = g_i[..., -1]
        S = (S * jnp.exp(g_last)[..., None, None]
             + jnp.einsum('bhcd,bhce->bhde', k_i * jnp.exp(g_last[..., None] - g_i)[..., None], v_new))
        return S, o

    s_fin, o = lax.scan(step, s0, (qc, kc, u, w, gc, qk))
    return o.transpose(1, 0, 3, 2, 4).reshape(B, L, H, DV), s_fin


def _bidir_gdn(P, s0_f, s0_b):
    fl = lambda t: t[:, ::-1]
    o_f, s_f = _chunk_gdn(P['dq'], P['dk'], P['dv'], P['g'][:, :, 0], P['beta'][:, :, 0], s0_f)
    o_b, s_b = _chunk_gdn(fl(P['dq']), fl(P['dk']), fl(P['dv']), fl(P['g'][:, :, 1]),
                          fl(P['beta'][:, :, 1]), s0_b)
    return o_f + fl(o_b), s_f, s_b


def _project(h, lp, rope):
    B, L, _ = h.shape
    z = h @ lp['w_in']
    f, aq, ak, av, dqkv, dz, gates = jnp.split(z, SPLITS, axis=-1)
    aq = _rms(aq.reshape(B, L, ATT_HQ, HEAD_DIM), lp['q_norm'])
    ak = _rms(ak.reshape(B, L, ATT_HKV, HEAD_DIM), lp['k_norm'])
    if rope is not None:
        aq = _rope2d(aq, rope[0], rope[1])
        ak = _rope2d(ak, rope[0], rope[1])
    dqkv = jax.nn.silu(_dwconv(dqkv, lp['conv_w'])).reshape(B, L, 3, DN_HEADS, DN_DIM)
    gates = gates.astype(jnp.float32).reshape(B, L, 4, DN_HEADS)
    beta = jax.nn.sigmoid(gates[:, :, 0:2])
    g = -jnp.exp(lp['a_log'].astype(jnp.float32)) * jax.nn.softplus(
        gates[:, :, 2:4] + lp['dt_bias'].astype(jnp.float32))
    return dict(f=f, aq=aq, ak=ak, av=av.reshape(B, L, ATT_HKV, HEAD_DIM),
                dq=_l2(dqkv[:, :, 0]) * (DN_DIM ** -0.5), dk=_l2(dqkv[:, :, 1]),
                dv=dqkv[:, :, 2].astype(jnp.float32), dz=dz, beta=beta, g=g)


def _mix_out(P, o_att, o_dn, w_out, o_norm_g):
    B, L, _ = o_att.shape
    gate = jax.nn.silu(P['dz'].astype(jnp.float32)).reshape(B, L, DN_HEADS, DN_DIM)
    dn = (_rms(o_dn, o_norm_g) * gate).reshape(B, L, DN_CH).astype(o_att.dtype)
    cat = jnp.concatenate([_fourier(P['f']), o_att, dn], axis=-1)
    return cat @ w_out


def _mlp(h, w1, w2):
    u = jnp.maximum(h @ w1, 0.0)
    return (u * u) @ w2


def _layer(x, xc, c, c_ctx, rope, lp, ctx_out):
    mod = jax.nn.silu(c) @ lp['w_mod'] + lp['b_mod']
    modc = jax.nn.silu(c_ctx) @ lp['w_mod'] + lp['b_mod']
    sh1, sc1, g1, sh2, sc2, g2 = jnp.split(mod[:, None, :], 6, axis=-1)
    csh1, csc1, cg1, csh2, csc2, cg2 = jnp.split(modc, 6)
    h = _rms(x, lp['norm1']) * (1 + sc1) + sh1
    hc = _rms(xc, lp['norm1']) * (1 + csc1) + csh1
    P = _project(h, lp, rope)
    Pc = _project(hc, lp, None)
    s0 = jnp.zeros((x.shape[0], DN_HEADS, DN_DIM, DN_DIM), jnp.float32)
    oc_dn, s_f, s_b = _bidir_gdn(Pc, s0, s0)
    k_all = jnp.concatenate([P['ak'], Pc['ak']], axis=1)
    v_all = jnp.concatenate([P['av'], Pc['av']], axis=1)
    o_att = _attend(P['aq'], k_all, v_all)
    o_dn, _, _ = _bidir_gdn(P, s_f, s_b)
    x = x + g1 * _mix_out(P, o_att, o_dn, lp['w_out'], lp['o_norm'])
    x = x + g2 * _mlp(_rms(x, lp['norm2']) * (1 + sc2) + sh2, lp['w_ff1'], lp['w_ff2'])
    if ctx_out:
        oc_att = _attend(Pc['aq'], Pc['ak'], Pc['av'])
        xc = xc + cg1 * _mix_out(Pc, oc_att, oc_dn, lp['w_out'], lp['o_norm'])
        xc = xc + cg2 * _mlp(_rms(xc, lp['norm2']) * (1 + csc2) + csh2, lp['w_ff1'], lp['w_ff2'])
    return x, xc


def setup_inputs(seed: int = 0) -> dict:
    key = jax.random.key(seed)
    ks = jax.random.split(key, 18)
    f32 = jnp.float32
    nrm = lambda k, shape, s: jax.random.normal(k, shape, f32) * s
    x = nrm(ks[0], (BATCH, SEQ, D_MODEL), 1.0)
    c = nrm(ks[1], (BATCH, D_MODEL), 1.0)
    ctx = nrm(ks[2], (BATCH, CTX_LEN, D_MODEL), 1.0)
    c_ctx = nrm(ks[3], (D_MODEL,), 1.0)
    norm1_g = 1.0 + nrm(ks[4], (DEPTH, D_MODEL), 0.02)
    norm2_g = 1.0 + nrm(ks[5], (DEPTH, D_MODEL), 0.02)
    w_mod = nrm(ks[6], (DEPTH, D_MODEL, 6 * D_MODEL), 0.5 * D_MODEL ** -0.5)
    b_mod = nrm(ks[7], (DEPTH, 6 * D_MODEL), 0.02)
    w_in = nrm(ks[8], (DEPTH, D_MODEL, PROJ), D_MODEL ** -0.5)
    conv_w = nrm(ks[9], (DEPTH, CONV_K, CONV_CH), CONV_K ** -0.5)
    q_norm_g = 1.0 + nrm(ks[10], (DEPTH, HEAD_DIM), 0.02)
    k_norm_g = 1.0 + nrm(ks[11], (DEPTH, HEAD_DIM), 0.02)
    a_log = jnp.log(jax.random.uniform(ks[12], (DEPTH, 2, DN_HEADS), f32, 1.0, 16.0))
    dt = jnp.exp(jax.random.uniform(ks[13], (DEPTH, 2, DN_HEADS), f32,
                                    float(np.log(1e-3)), float(np.log(1e-1))))
    dt_bias = dt + jnp.log(-jnp.expm1(-dt))
    o_norm_g = 1.0 + nrm(ks[14], (DEPTH, DN_DIM), 0.02)
    w_out = nrm(ks[15], (DEPTH, D_MIX, D_MODEL), D_MIX ** -0.5)
    w_ff1 = nrm(ks[16], (DEPTH, D_MODEL, D_FF), D_MODEL ** -0.5)
    w_ff2 = nrm(ks[17], (DEPTH, D_FF, D_MODEL), D_FF ** -0.5)
    return {'x': x, 'c': c, 'ctx': ctx, 'c_ctx': c_ctx, 'norm1_g': norm1_g, 'norm2_g': norm2_g,
            'w_mod': w_mod, 'b_mod': b_mod, 'w_in': w_in, 'conv_w': conv_w,
            'q_norm_g': q_norm_g, 'k_norm_g': k_norm_g, 'a_log': a_log, 'dt_bias': dt_bias,
            'o_norm_g': o_norm_g, 'w_out': w_out, 'w_ff1': w_ff1, 'w_ff2': w_ff2}


def reference(x, c, ctx, c_ctx, norm1_g, norm2_g, w_mod, b_mod, w_in, conv_w, q_norm_g, k_norm_g,
              a_log, dt_bias, o_norm_g, w_out, w_ff1, w_ff2):
    S = x.shape[1]
    rows = S // GRID_W
    t_row = jnp.repeat(jnp.arange(rows), GRID_W).astype(jnp.float32)
    t_col = jnp.tile(jnp.arange(GRID_W), rows).astype(jnp.float32)
    inv_freq = ROPE_THETA ** (-jnp.arange(ROT_PAIRS, dtype=jnp.float32) * 2.0 / AXIS_DIM)
    ang_r = t_row[:, None] * inv_freq
    ang_c = t_col[:, None] * inv_freq
    ang = jnp.concatenate([ang_r, ang_r, ang_c, ang_c], axis=-1)
    rope = (jnp.cos(ang)[:, None, :], jnp.sin(ang)[:, None, :])
    x_lat, x_ctx = x, ctx
    for l in range(DEPTH):
        lp = dict(norm1=norm1_g[l], norm2=norm2_g[l], w_mod=w_mod[l], b_mod=b_mod[l],
                  w_in=w_in[l], conv_w=conv_w[l], q_norm=q_norm_g[l], k_norm=k_norm_g[l],
                  a_log=a_log[l], dt_bias=dt_bias[l], o_norm=o_norm_g[l], w_out=w_out[l],
                  w_ff1=w_ff1[l], w_ff2=w_ff2[l])
        x_lat, x_ctx = _layer(x_lat, x_ctx, c, c_ctx, rope, lp, l < DEPTH - 1)
    return x_lat
```

```python
import functools

import jax
import jax.numpy as jnp
import numpy as np
from jax import lax
from jax.experimental import pallas as pl
from jax.experimental.pallas import tpu as pltpu

HEAD_DIM = 64
FT_GROUPS = 4
FT_CH = FT_GROUPS * HEAD_DIM
ATT_HQ = 8
ATT_HKV = 2
ATT_GROUP = ATT_HQ // ATT_HKV
ATT_CH = ATT_HQ * HEAD_DIM
KV_CH = ATT_HKV * HEAD_DIM
DN_HEADS = 4
DN_CH = DN_HEADS * HEAD_DIM
CONV_CH = 3 * DN_CH
N_GATES = 4 * DN_HEADS
GRID_W = 64
ROPE_THETA = 10000.0
AXIS_DIM = HEAD_DIM // 2
ROT_PAIRS = AXIS_DIM // 2
EPS = 1e-6
CHUNK = 64

LANES = 128
V7X_VMEM_BYTES = 64 * 1024 * 1024
VMEM_LIMIT = V7X_VMEM_BYTES - 8 * 1024 * 1024

GDN_TILE = 256
CHUNKS_PER_TILE = GDN_TILE // CHUNK
NEG_BIG = -1e30
GRAM_PASSES = 1
SOLVE_PASSES = 3
SCAN_PASSES = 1

F32 = jnp.float32
BF16 = jnp.bfloat16


def _dot(a, b):
    return jnp.dot(a, b, preferred_element_type=F32)


def _dot_nt(a, b):
    return lax.dot_general(a, b, (((1,), (1,)), ((), ())), preferred_element_type=F32)


def _split3(x):
    hi = x.astype(BF16)
    r1 = x - hi.astype(F32)
    mid = r1.astype(BF16)
    lo = (r1 - mid.astype(F32)).astype(BF16)
    return hi, mid, lo


def _dot_exact_lhs(m_bf16, x):
    hi, mid, lo = _split3(x)
    return _dot(m_bf16, hi) + _dot(m_bf16, mid) + _dot(m_bf16, lo)


def _split2(x):
    hi = x.astype(BF16)
    return hi, (x - hi.astype(F32)).astype(BF16)


def _dot_f32(a, b, passes, nt=False):
    mm = _dot_nt if nt else _dot
    if passes == 1:
        return mm(a.astype(BF16), b.astype(BF16))
    ah, al = _split2(a)
    bh, bl = _split2(b)
    return mm(ah, bh) + (mm(al, bh) + mm(ah, bl))


def _seg64_sum(x):
    lane = lax.broadcasted_iota(jnp.int32, x.shape, 1)
    for sh in (1, 2, 4, 8, 16, 32):
        up = pltpu.roll(x, LANES - sh, axis=1)
        dn = pltpu.roll(x, sh, axis=1)
        x = x + jnp.where((lane & sh) == 0, up, dn)
    return x


def _sigmoid(x):
    return 1.0 / (1.0 + jnp.exp(-x))


def _silu(x):
    return x * _sigmoid(x)


def _softplus(x):
    return jnp.maximum(x, 0.0) + jnp.log1p(jnp.exp(-jnp.abs(x)))


def _mod_kernel(c_ref, w_ref, b_ref, o_ref):
    a = _silu(c_ref[...])
    o_ref[0] = jnp.dot(a, w_ref[0], preferred_element_type=F32,
                       precision=lax.Precision.HIGHEST) + b_ref[0]


def _modulation(c_all, w_mod, b_mod):
    depth, d, n = w_mod.shape
    rows = c_all.shape[0]
    tn = 1024
    return pl.pallas_call(
        _mod_kernel,
        out_shape=jax.ShapeDtypeStruct((depth, rows, n), F32),
        grid=(depth, n // tn),
        in_specs=[pl.BlockSpec((rows, d), lambda l, j: (0, 0)),
                  pl.BlockSpec((1, d, tn), lambda l, j: (l, 0, j)),
                  pl.BlockSpec((1, 1, tn), lambda l, j: (l, 0, j))],
        out_specs=pl.BlockSpec((1, rows, tn), lambda l, j: (l, 0, j)),
        compiler_params=pltpu.CompilerParams(
            dimension_semantics=("arbitrary", "arbitrary"), vmem_limit_bytes=VMEM_LIMIT),
        name="modulation",
    )(c_all, w_mod, b_mod.reshape(depth, 1, n))


def _head_rms_rope(z, gain, cos, sin_signed, scale):
    ms = _seg64_sum(z * z) * (1.0 / HEAD_DIM)
    y = z * lax.rsqrt(ms + EPS) * gain
    if cos is not None:
        lane = lax.broadcasted_iota(jnp.int32, y.shape, 1)
        partner = jnp.where((lane & ROT_PAIRS) == 0,
                            pltpu.roll(y, LANES - ROT_PAIRS, axis=1),
                            pltpu.roll(y, ROT_PAIRS, axis=1))
        y = y * cos + partner * sin_signed
    if scale != 1.0:
        y = y * scale
    return y


def _project_kernel(use_rope, x_ref, sc_ref, sh_ref, g_ref, wf_ref, dft_ref, wq_ref, wkv_ref,
                    wd_ref, wz_ref, wg_ref, qg_ref, kg_ref, cos_ref, sin_ref,
                    f_ref, q_ref, k_ref, v_ref, d_ref, z_ref, gt_ref):
    x = x_ref[0]
    ms = jnp.mean(x * x, axis=-1, keepdims=True)
    gain = g_ref[...] * (1.0 + sc_ref[0])
    h = (x * lax.rsqrt(ms + EPS) * gain + sh_ref[0]).astype(BF16)

    cos = cos_ref[...] if use_rope else None
    sin = sin_ref[...] if use_rope else None

    f = _dot(h, wf_ref[...]).astype(BF16)
    f_ref[0] = _dot(f, dft_ref[...]).astype(BF16)

    zq = _dot(h, wq_ref[...])
    qg = qg_ref[...]
    for j in range(ATT_CH // LANES):
        sl = slice(j * LANES, (j + 1) * LANES)
        q_ref[0, :, sl] = _head_rms_rope(zq[:, sl], qg, cos, sin, HEAD_DIM ** -0.5).astype(BF16)

    zkv = _dot(h, wkv_ref[...])
    k_ref[0] = _head_rms_rope(zkv[:, :KV_CH], kg_ref[...], cos, sin, 1.0).astype(BF16)
    v_ref[0] = zkv[:, KV_CH:].astype(BF16)

    d_ref[0] = _dot(h, wd_ref[...])
    z_ref[0] = _dot(h, wz_ref[...])
    gt_ref[0] = _dot(h, wg_ref[...])


def _project(x, sc, sh, norm_g, wts, qg, kg, cos_t, sin_t, use_rope, tm):
    B, L, D = x.shape
    per_batch = sc.shape[0] > 1
    mod_map = (lambda b, i: (b, 0, 0)) if per_batch else (lambda b, i: (0, 0, 0))
    full = lambda a: pl.BlockSpec(a.shape, lambda b, i: (0,) * a.ndim)
    row = lambda n: pl.BlockSpec((1, tm, n), lambda b, i: (b, i, 0))
    wf, dft, wq, wkv, wd, wz, wg = wts
    out_shape = (jax.ShapeDtypeStruct((B, L, 2 * FT_CH), BF16),
                 jax.ShapeDtypeStruct((B, L, ATT_CH), BF16),
                 jax.ShapeDtypeStruct((B, L, KV_CH), BF16),
                 jax.ShapeDtypeStruct((B, L, KV_CH), BF16),
                 jax.ShapeDtypeStruct((B, L, CONV_CH), F32),
                 jax.ShapeDtypeStruct((B, L, DN_CH), F32),
                 jax.ShapeDtypeStruct((B, L, LANES), F32))
    return pl.pallas_call(
        functools.partial(_project_kernel, use_rope),
        out_shape=out_shape,
        grid=(B, L // tm),
        in_specs=[row(D),
                  pl.BlockSpec((1, 1, D), mod_map), pl.BlockSpec((1, 1, D), mod_map),
                  full(norm_g), full(wf), full(dft), full(wq), full(wkv), full(wd), full(wz),
                  full(wg), full(qg), full(kg),
                  pl.BlockSpec((tm, LANES), lambda b, i: (i, 0)),
                  pl.BlockSpec((tm, LANES), lambda b, i: (i, 0))],
        out_specs=(row(2 * FT_CH), row(ATT_CH), row(KV_CH), row(KV_CH), row(CONV_CH),
                   row(DN_CH), row(LANES)),
        compiler_params=pltpu.CompilerParams(
            dimension_semantics=("arbitrary", "arbitrary"), vmem_limit_bytes=VMEM_LIMIT),
        name="project_rope" if use_rope else "project",
    )(x, sc, sh, norm_g, wf, dft, wq, wkv, wd, wz, wg, qg, kg, cos_t, sin_t)


def _fourier_kernel(scale, dc_ref, ds_ref, f_ref, o_ref):
    fcs = f_ref[0]
    y = _dot(dc_ref[...], fcs[:, :FT_CH]) - _dot(ds_ref[...], fcs[:, FT_CH:])
    o_ref[0] = (y * scale).astype(BF16)


def _fourier(fcs, dft_c, dft_s, tn):
    B, L, _ = fcs.shape
    scale = float(1.0 / np.sqrt(L * HEAD_DIM))
    return pl.pallas_call(
        functools.partial(_fourier_kernel, scale),
        out_shape=jax.ShapeDtypeStruct((B, L, FT_CH), BF16),
        grid=(L // tn, B),
        in_specs=[pl.BlockSpec((tn, L), lambda n, b: (n, 0)),
                  pl.BlockSpec((tn, L), lambda n, b: (n, 0)),
                  pl.BlockSpec((1, L, 2 * FT_CH), lambda n, b: (b, 0, 0))],
        out_specs=pl.BlockSpec((1, tn, FT_CH), lambda n, b: (b, n, 0)),
        compiler_params=pltpu.CompilerParams(
            dimension_semantics=("arbitrary", "arbitrary"), vmem_limit_bytes=VMEM_LIMIT),
        name="fourier",
    )(dft_c, dft_s, fcs)


def _attention_kernel(n_src, q_ref, *refs):
    kv_refs = refs[:2 * n_src]
    o_ref = refs[2 * n_src]
    tq = q_ref.shape[1]
    for h in range(ATT_HKV):
        q = q_ref[0, :, h * ATT_GROUP * HEAD_DIM:(h + 1) * ATT_GROUP * HEAD_DIM]
        q4 = jnp.concatenate([q[:, g * HEAD_DIM:(g + 1) * HEAD_DIM] for g in range(ATT_GROUP)],
                             axis=0)
        scores = []
        m = None
        for s in range(n_src):
            k = kv_refs[2 * s][0, :, h * HEAD_DIM:(h + 1) * HEAD_DIM]
            sc = _dot_nt(q4, k)
            scores.append(sc)
            ms = jnp.max(sc, axis=-1, keepdims=True)
            m = ms if m is None else jnp.maximum(m, ms)
        acc = None
        for s in range(n_src):
            v = kv_refs[2 * s + 1][0]
            lane = lax.broadcasted_iota(jnp.int32, v.shape, 1)
            in_head = (lane >= h * HEAD_DIM) & (lane < (h + 1) * HEAD_DIM)
            v_aug = jnp.where(in_head, v, jnp.ones_like(v))
            p = jnp.exp(scores[s] - m).astype(BF16)
            pv = _dot(p, v_aug)
            acc = pv if acc is None else acc + pv
        o = acc[:, h * HEAD_DIM:(h + 1) * HEAD_DIM]
        den = acc[:, (1 - h) * HEAD_DIM:(2 - h) * HEAD_DIM]
        o = (o / den).astype(BF16)
        for g in range(ATT_GROUP):
            c0 = (h * ATT_GROUP + g) * HEAD_DIM
            o_ref[0, :, c0:c0 + HEAD_DIM] = o[g * tq:(g + 1) * tq]


def _attention(q, kv_sources, tq):
    B, L, _ = q.shape
    n_src = len(kv_sources)
    in_specs = [pl.BlockSpec((1, tq, ATT_CH), lambda b, i: (b, i, 0))]
    args = [q]
    for k, v in kv_sources:
        lk = k.shape[1]
        in_specs += [pl.BlockSpec((1, lk, KV_CH), lambda b, i: (b, 0, 0)),
                     pl.BlockSpec((1, lk, KV_CH), lambda b, i: (b, 0, 0))]
        args += [k, v]
    return pl.pallas_call(
        functools.partial(_attention_kernel, n_src),
        out_shape=jax.ShapeDtypeStruct((B, L, ATT_CH), BF16),
        grid=(B, L // tq),
        in_specs=in_specs,
        out_specs=pl.BlockSpec((1, tq, ATT_CH), lambda b, i: (b, i, 0)),
        compiler_params=pltpu.CompilerParams(
            dimension_semantics=("arbitrary", "arbitrary"), vmem_limit_bytes=VMEM_LIMIT),
        name="attention",
    )(*args)


def _lane_col(x, c):
    lane = lax.broadcasted_iota(jnp.int32, x.shape, 1)
    return jnp.sum(jnp.where(lane == c, x, 0.0), axis=1, keepdims=True)


def _head_bcast(cols, lane_head):
    out = cols[DN_HEADS - 1]
    for h in range(DN_HEADS - 2, -1, -1):
        out = jnp.where(lane_head <= h, cols[h], out)
    return out


def _conv_silu_norm(x, prev_row, next_row, w_ref):
    n = x.shape[0]
    row = lax.broadcasted_iota(jnp.int32, x.shape, 0)
    x_m1 = jnp.where(row == 0, prev_row, pltpu.roll(x, 1, axis=0))
    x_p1 = jnp.where(row == n - 1, next_row, pltpu.roll(x, n - 1, axis=0))
    y = _silu(x_m1 * w_ref[0:1, :] + x * w_ref[1:2, :] + x_p1 * w_ref[2:3, :])
    outs = []
    for j in range(CONV_CH // LANES):
        slab = y[:, j * LANES:(j + 1) * LANES]
        if j < 2 * DN_CH // LANES:
            ss = _seg64_sum(slab * slab)
            slab = slab * lax.rsqrt(ss + EPS)
            if j < DN_CH // LANES:
                slab = slab * (HEAD_DIM ** -0.5)
        outs.append(slab)
    return jnp.concatenate(outs, axis=1)


def _gates_to_beta_g(z, alog_ref, dtb_ref):
    lane = lax.broadcasted_iota(jnp.int32, z.shape, 1)
    g = -jnp.exp(alog_ref[...]) * _softplus(z + dtb_ref[...])
    return jnp.where(lane < 2 * DN_HEADS, _sigmoid(z), g)


def _gdn_tile(direction, qkv, bg, s_ref, masks):
    cum_ref, ones_ref = masks
    cum_m = cum_ref[...]
    ones_m = ones_ref[...]
    ri = lax.broadcasted_iota(jnp.int32, (GDN_TILE, GDN_TILE), 0)
    ci = lax.broadcasted_iota(jnp.int32, (GDN_TILE, GDN_TILE), 1)
    blk = (ri // CHUNK) == (ci // CHUNK)
    lane_head = ci // HEAD_DIM
    if direction == 0:
        tri_strict, tri_incl = blk & (ri > ci), blk & (ri >= ci)
    else:
        tri_strict, tri_incl = blk & (ri < ci), blk & (ri <= ci)
    q = qkv[:, 0:DN_CH]
    k = qkv[:, DN_CH:2 * DN_CH]
    v = qkv[:, 2 * DN_CH:3 * DN_CH]

    gc = _dot_exact_lhs(cum_m, bg)
    gt = _dot_exact_lhs(ones_m, bg)
    gc_t = gc.T
    row_t = lax.broadcasted_iota(jnp.int32, gc_t.shape, 0)

    beta_cols, gc_cols, gt_cols = [], [], []
    for h in range(DN_HEADS):
        beta_cols.append(_lane_col(bg, direction * DN_HEADS + h))
        gc_cols.append(_lane_col(gc, 2 * DN_HEADS + direction * DN_HEADS + h))
        gt_cols.append(_lane_col(gt, 2 * DN_HEADS + direction * DN_HEADS + h))
    beta_b = _head_bcast(beta_cols, lane_head)
    gc_b = _head_bcast(gc_cols, lane_head)
    gt_b = _head_bcast(gt_cols, lane_head)

    e_gc = jnp.exp(gc_b)
    kb = k * beta_b
    vb = v * beta_b
    kbe = kb * e_gc
    qe = q * e_gc
    kd = k * jnp.exp(gt_b - gc_b)
    e_last = jnp.exp(gt_b)
    kd_t = kd.T

    rhs_uw = jnp.concatenate([vb, kbe], axis=1)
    eye = (lax.broadcasted_iota(jnp.int32, (GDN_TILE, GDN_TILE), 0)
           == lax.broadcasted_iota(jnp.int32, (GDN_TILE, GDN_TILE), 1)).astype(F32)

    u_all = jnp.zeros((GDN_TILE, DN_CH), F32)
    w_all = jnp.zeros((GDN_TILE, DN_CH), F32)
    qk_heads = []
    for h in range(DN_HEADS):
        in_h = lane_head == h
        r = jnp.sum(jnp.where(row_t == 2 * DN_HEADS + direction * DN_HEADS + h, gc_t, 0.0),
                    axis=0, keepdims=True)
        diff = gc_cols[h] - r
        dec_s = jnp.exp(jnp.where(tri_strict, diff, NEG_BIG))
        dec_i = jnp.exp(jnp.where(tri_incl, diff, NEG_BIG))
        kk = _dot_f32(jnp.where(in_h, kb, 0.0), k, GRAM_PASSES, nt=True)
        qk = _dot_f32(jnp.where(in_h, q, 0.0), k, GRAM_PASSES, nt=True)
        p = -(kk * dec_s)
        t = eye + p
        xp = p
        for _ in range(5):
            xp = _dot_f32(xp, xp, SOLVE_PASSES)
            t = t + _dot_f32(t, xp, SOLVE_PASSES)
        uw = _dot_f32(t, rhs_uw, SOLVE_PASSES)
        u_all = jnp.where(in_h, uw[:, :DN_CH], u_all)
        w_all = jnp.where(in_h, uw[:, DN_CH:], w_all)
        qk_heads.append(qk * dec_i)

    order = range(CHUNKS_PER_TILE) if direction == 0 else range(CHUNKS_PER_TILE - 1, -1, -1)
    outs = [None] * CHUNKS_PER_TILE
    zeros_c = jnp.zeros((CHUNK, DN_CH), F32)
    lane_head_c = lax.broadcasted_iota(jnp.int32, (CHUNK, DN_CH), 1) // HEAD_DIM
    for c in order:
        rows = slice(c * CHUNK, (c + 1) * CHUNK)
        s = s_ref[...]
        ws = _dot_f32(jnp.concatenate([w_all[rows], qe[rows]], axis=0), s, SCAN_PASSES)
        v_new = u_all[rows] - ws[:CHUNK]
        v_tile = jnp.concatenate([v_new if i == c else zeros_c for i in range(CHUNKS_PER_TILE)],
                                 axis=0)
        lhs = jnp.concatenate([kd_t] + [qk_heads[h][rows] for h in range(DN_HEADS)], axis=0)
        r2 = _dot_f32(lhs, v_tile, SCAN_PASSES)
        s_ref[...] = s * e_last[c * CHUNK:c * CHUNK + 1, :] + jnp.where(blk, r2[:GDN_TILE], 0.0)
        o = ws[CHUNK:]
        for h in range(DN_HEADS):
            o = o + jnp.where(lane_head_c == h,
                              r2[GDN_TILE + h * CHUNK:GDN_TILE + (h + 1) * CHUNK], 0.0)
        outs[c] = o
    return jnp.concatenate(outs, axis=0)


def _gdn_kernel(n_lat, write_ctx, dc_ref, dl_ref, gc_ref, gl_ref, zc_ref, zl_ref, cw_ref,
                alog_ref, dtb_ref, on_ref, cum_f_ref, cum_b_ref, ones_ref,
                ol_ref, oc_ref, qkv_ref, bg_ref, of_ref, ob_ref, sf_ref, sb_ref):
    T = GDN_TILE
    zero_row = jnp.zeros((1, CONV_CH), F32)

    qkv_ref[0:T] = _conv_silu_norm(dc_ref[0], zero_row, zero_row, cw_ref)
    bg_ref[0:T] = _gates_to_beta_g(gc_ref[0], alog_ref, dtb_ref)

    def prep(t, carry):
        r0 = pl.multiple_of(t * T, T)
        x = dl_ref[0, pl.ds(r0, T), :]
        prev_row = dl_ref[0, pl.ds(jnp.maximum(r0 - 1, 0), 1), :]
        next_row = dl_ref[0, pl.ds(jnp.minimum(r0 + T, n_lat * T - 1), 1), :]
        prev_row = jnp.where(t == 0, zero_row, prev_row)
        next_row = jnp.where(t == n_lat - 1, zero_row, next_row)
        o0 = pl.multiple_of(r0 + T, T)
        qkv_ref[pl.ds(o0, T), :] = _conv_silu_norm(x, prev_row, next_row, cw_ref)
        bg_ref[pl.ds(o0, T), :] = _gates_to_beta_g(gl_ref[0, pl.ds(r0, T), :], alog_ref, dtb_ref)
        return carry

    lax.fori_loop(0, n_lat, prep, 0)

    sf_ref[...] = jnp.zeros_like(sf_ref)
    sb_ref[...] = jnp.zeros_like(sb_ref)

    def scan(s, carry):
        tf = pl.multiple_of(s * T, T)
        tb = pl.multiple_of(jnp.where(s == 0, 0, n_lat + 1 - s) * T, T)
        of_ref[pl.ds(tf, T), :] = _gdn_tile(0, qkv_ref[pl.ds(tf, T), :], bg_ref[pl.ds(tf, T), :],
                                            sf_ref, (cum_f_ref, ones_ref))
        ob_ref[pl.ds(tb, T), :] = _gdn_tile(1, qkv_ref[pl.ds(tb, T), :], bg_ref[pl.ds(tb, T), :],
                                            sb_ref, (cum_b_ref, ones_ref))
        return carry

    lax.fori_loop(0, n_lat + 1, scan, 0)

    def finish(o, z):
        slabs = []
        for j in range(DN_CH // LANES):
            sl = slice(j * LANES, (j + 1) * LANES)
            ms = _seg64_sum(o[:, sl] * o[:, sl]) * (1.0 / HEAD_DIM)
            slabs.append(o[:, sl] * lax.rsqrt(ms + EPS) * on_ref[...] * _silu(z[:, sl]))
        return jnp.concatenate(slabs, axis=1).astype(BF16)

    if write_ctx:
        oc_ref[0] = finish(of_ref[0:T] + ob_ref[0:T], zc_ref[0])
    else:
        oc_ref[0] = jnp.zeros(oc_ref.shape[1:], BF16)

    def fin(t, carry):
        r0 = pl.multiple_of(t * T, T)
        o0 = pl.multiple_of(r0 + T, T)
        ol_ref[0, pl.ds(r0, T), :] = finish(of_ref[pl.ds(o0, T), :] + ob_ref[pl.ds(o0, T), :],
                                            zl_ref[0, pl.ds(r0, T), :])
        return carry

    lax.fori_loop(0, n_lat, fin, 0)


def _gdn(d_ctx, d_lat, g_ctx, g_lat, z_ctx, z_lat, conv_w, alog_row, dtb_row, on_row, consts,
         write_ctx):
    B, L, _ = d_lat.shape
    Lc = d_ctx.shape[1]
    assert Lc == GDN_TILE and L % GDN_TILE == 0
    n_lat = L // GDN_TILE
    cum_f, cum_b, ones_m = consts
    per_b = lambda n, c: pl.BlockSpec((1, n, c), lambda b: (b, 0, 0))
    full = lambda a: pl.BlockSpec(a.shape, lambda b: (0,) * a.ndim)
    tot = L + Lc
    return pl.pallas_call(
        functools.partial(_gdn_kernel, n_lat, write_ctx),
        out_shape=(jax.ShapeDtypeStruct((B, L, DN_CH), BF16),
                   jax.ShapeDtypeStruct((B, Lc, DN_CH), BF16)),
        grid=(B,),
        in_specs=[per_b(Lc, CONV_CH), per_b(L, CONV_CH), per_b(Lc, LANES), per_b(L, LANES),
                  per_b(Lc, DN_CH), per_b(L, DN_CH), full(conv_w), full(alog_row), full(dtb_row),
                  full(on_row), full(cum_f), full(cum_b), full(ones_m)],
        out_specs=(per_b(L, DN_CH), per_b(Lc, DN_CH)),
        scratch_shapes=[pltpu.VMEM((tot, CONV_CH), F32), pltpu.VMEM((tot, LANES), F32),
                        pltpu.VMEM((tot, DN_CH), F32), pltpu.VMEM((tot, DN_CH), F32),
                        pltpu.VMEM((GDN_TILE, DN_CH), F32), pltpu.VMEM((GDN_TILE, DN_CH), F32)],
        compiler_params=pltpu.CompilerParams(
            dimension_semantics=("arbitrary",), vmem_limit_bytes=VMEM_LIMIT),
        name="gdn",
    )(d_ctx, d_lat, g_ctx, g_lat, z_ctx, z_lat, conv_w, alog_row, dtb_row, on_row,
      cum_f, cum_b, ones_m)


def _mix_mlp_kernel(ff_tile, x_ref, fy_ref, at_ref, dn_ref, g1_ref, sc_ref, sh_ref, g2_ref, ng_ref,
                    wof_ref, woa_ref, wod_ref, w1_ref, w2_ref, o_ref):
    mix = (_dot(fy_ref[0], wof_ref[...]) + _dot(at_ref[0], woa_ref[...])
           + _dot(dn_ref[0], wod_ref[...]))
    x1 = x_ref[0] + g1_ref[0] * mix
    ms = jnp.mean(x1 * x1, axis=-1, keepdims=True)
    gain = ng_ref[...] * (1.0 + sc_ref[0])
    h = (x1 * lax.rsqrt(ms + EPS) * gain + sh_ref[0]).astype(BF16)
    acc = jnp.zeros_like(x1)
    for c in range(w1_ref.shape[1] // ff_tile):
        u = jnp.maximum(_dot(h, w1_ref[:, c * ff_tile:(c + 1) * ff_tile]), 0.0)
        acc = acc + _dot((u * u).astype(BF16), w2_ref[c * ff_tile:(c + 1) * ff_tile, :])
    o_ref[0] = x1 + g2_ref[0] * acc


def _mix_mlp(x, fy, att, dn, g1, sc, sh, g2, norm_g, wts, tm):
    B, L, D = x.shape
    per_batch = g1.shape[0] > 1
    mod_map = (lambda b, i: (b, 0, 0)) if per_batch else (lambda b, i: (0, 0, 0))
    mod = pl.BlockSpec((1, 1, D), mod_map)
    full = lambda a: pl.BlockSpec(a.shape, lambda b, i: (0,) * a.ndim,
                                  pipeline_mode=pl.Buffered(1))
    row = lambda n: pl.BlockSpec((1, tm, n), lambda b, i: (b, i, 0))
    wof, woa, wod, w1, w2 = wts
    return pl.pallas_call(
        functools.partial(_mix_mlp_kernel, 1024),
        out_shape=jax.ShapeDtypeStruct((B, L, D), F32),
        grid=(B, L // tm),
        in_specs=[row(D), row(FT_CH), row(ATT_CH), row(DN_CH), mod, mod, mod, mod,
                  pl.BlockSpec(norm_g.shape, lambda b, i: (0, 0)),
                  full(wof), full(woa), full(wod), full(w1), full(w2)],
        out_specs=row(D),
        compiler_params=pltpu.CompilerParams(
            dimension_semantics=("arbitrary", "arbitrary"), vmem_limit_bytes=VMEM_LIMIT),
        name="mix_mlp",
    )(x, fy, att, dn, g1, sc, sh, g2, norm_g, wof, woa, wod, w1, w2)


def _rope_tables(S):
    rows = S // GRID_W
    t_row = jnp.repeat(jnp.arange(rows), GRID_W).astype(F32)
    t_col = jnp.tile(jnp.arange(GRID_W), rows).astype(F32)
    inv_freq = ROPE_THETA ** (-jnp.arange(ROT_PAIRS, dtype=F32) * 2.0 / AXIS_DIM)
    ang_r = t_row[:, None] * inv_freq
    ang_c = t_col[:, None] * inv_freq
    ang = jnp.concatenate([ang_r, ang_r, ang_c, ang_c], axis=-1)
    cos, sin = jnp.cos(ang), jnp.sin(ang)
    sign = jnp.where((jnp.arange(HEAD_DIM) & ROT_PAIRS) == 0, -1.0, 1.0).astype(F32)
    tile2 = lambda a: jnp.concatenate([a, a], axis=-1)
    return tile2(cos), tile2(sin * sign)


def _dft_tables(n):
    k = (jnp.arange(n, dtype=jnp.int32)[:, None] * jnp.arange(n, dtype=jnp.int32)[None, :]) % n
    ang = k.astype(F32) * np.float32(2.0 * np.pi / n)
    return jnp.cos(ang), jnp.sin(ang)


def _channel_dft():
    c, s = _dft_tables(HEAD_DIM)
    eye = jnp.eye(FT_GROUPS, dtype=F32)
    return jnp.concatenate([jnp.kron(eye, c), jnp.kron(eye, s)], axis=1).astype(BF16)


def _gdn_consts():
    i = np.arange(GDN_TILE)
    blk = (i[:, None] // CHUNK) == (i[None, :] // CHUNK)
    cum_f = (blk & (i[:, None] >= i[None, :])).astype(np.float32)
    cum_b = (blk & (i[:, None] <= i[None, :])).astype(np.float32)
    return (jnp.asarray(cum_f, BF16), jnp.asarray(cum_b, BF16), jnp.asarray(blk.astype(np.float32), BF16))


def _pad_lanes(a):
    flat = a.reshape(1, -1).astype(F32)
    return jnp.pad(flat, ((0, 0), (0, LANES - flat.shape[1])))


def kernel(x, c, ctx, c_ctx, norm1_g, norm2_g, w_mod, b_mod, w_in, conv_w, q_norm_g, k_norm_g,
           a_log, dt_bias, o_norm_g, w_out, w_ff1, w_ff2):
    B, S, D = x.shape
    Lc = ctx.shape[1]
    depth = w_mod.shape[0]

    rows = ((B + 1 + 7) // 8) * 8
    c_all = jnp.concatenate([c, c_ctx[None, :], jnp.zeros((rows - B - 1, D), F32)], axis=0)
    mod_all = _modulation(c_all, w_mod, b_mod)

    cos_t, sin_t = _rope_tables(S)
    dft_ch = _channel_dft()
    dft_lat = tuple(t.astype(BF16) for t in _dft_tables(S))
    dft_ctx = tuple(t.astype(BF16) for t in _dft_tables(Lc))
    gdn_consts = _gdn_consts()

    o0 = FT_CH
    o1 = o0 + ATT_CH
    o2 = o1 + 2 * KV_CH
    o3 = o2 + CONV_CH
    o4 = o3 + DN_CH

    x_lat, x_ctx = x, ctx
    for l in range(depth):
        last = l == depth - 1
        w = w_in[l]
        wg = jnp.pad(w[:, o4:], ((0, 0), (0, LANES - N_GATES)))
        in_wts = (w[:, :o0].astype(BF16), dft_ch, w[:, o0:o1].astype(BF16), w[:, o1:o2].astype(BF16),
                  w[:, o2:o3].astype(BF16), w[:, o3:o4].astype(BF16), wg.astype(BF16))
        qg = jnp.tile(q_norm_g[l], 2).reshape(1, LANES)
        kg = jnp.tile(k_norm_g[l], 2).reshape(1, LANES)
        n1 = norm1_g[l].reshape(1, D)
        n2 = norm2_g[l].reshape(1, D)
        wo = w_out[l].astype(BF16)
        out_wts = (wo[:FT_CH], wo[FT_CH:FT_CH + ATT_CH], wo[FT_CH + ATT_CH:],
                   w_ff1[l].astype(BF16), w_ff2[l].astype(BF16))
        alog_row = jnp.pad(a_log[l].reshape(1, -1), ((0, 0), (2 * DN_HEADS, LANES - N_GATES)))
        dtb_row = jnp.pad(dt_bias[l].reshape(1, -1), ((0, 0), (2 * DN_HEADS, LANES - N_GATES)))
        on_row = jnp.tile(o_norm_g[l], 2).reshape(1, LANES)

        mod = mod_all[l, :B].reshape(B, 1, 6 * D)
        modc = mod_all[l, B:B + 1].reshape(1, 1, 6 * D)
        sh1, sc1, g1, sh2, sc2, g2 = [mod[:, :, i * D:(i + 1) * D] for i in range(6)]
        csh1, csc1, cg1, csh2, csc2, cg2 = [modc[:, :, i * D:(i + 1) * D] for i in range(6)]

        fl, ql, kl, vl, dl, zl, gl = _project(x_lat, sc1, sh1, n1, in_wts, qg, kg, cos_t, sin_t,
                                              True, 512)
        fc, qc, kc, vc, dc, zc, gc = _project(x_ctx, csc1, csh1, n1, in_wts, qg, kg,
                                              cos_t[:Lc], sin_t[:Lc], False, Lc)

        dn_lat, dn_ctx = _gdn(dc, dl, gc, gl, zc, zl, conv_w[l], alog_row, dtb_row, on_row,
                              gdn_consts, not last)
        att_lat = _attention(ql, [(kl, vl), (kc, vc)], 128)
        fy_lat = _fourier(fl, dft_lat[0], dft_lat[1], 512)
        x_lat = _mix_mlp(x_lat, fy_lat, att_lat, dn_lat, g1, sc2, sh2, g2, n2, out_wts, 512)
        if not last:
            att_ctx = _attention(qc, [(kc, vc)], 128)
            fy_ctx = _fourier(fc, dft_ctx[0], dft_ctx[1], Lc)
            x_ctx = _mix_mlp(x_ctx, fy_ctx, att_ctx, dn_ctx, cg1, csc2, csh2, cg2, n2, out_wts, Lc)
    return x_lat
```

```python
import functools

import jax
import jax.numpy as jnp
import numpy as np
from jax import lax
from jax.experimental import pallas as pl
from jax.experimental.pallas import tpu as pltpu

HEAD_DIM = 64
FT_GROUPS = 4
FT_CH = FT_GROUPS * HEAD_DIM
ATT_HQ = 8
ATT_HKV = 2
ATT_GROUP = ATT_HQ // ATT_HKV
ATT_CH = ATT_HQ * HEAD_DIM
KV_CH = ATT_HKV * HEAD_DIM
DN_HEADS = 4
DN_CH = DN_HEADS * HEAD_DIM
CONV_CH = 3 * DN_CH
N_GATES = 4 * DN_HEADS
GRID_W = 64
ROPE_THETA = 10000.0
AXIS_DIM = HEAD_DIM // 2
ROT_PAIRS = AXIS_DIM // 2
EPS = 1e-6
CHUNK = 64

LANES = 128
V7X_VMEM_BYTES = 64 * 1024 * 1024
VMEM_LIMIT = V7X_VMEM_BYTES - 8 * 1024 * 1024

GDN_TILE = 256
CHUNKS_PER_TILE = GDN_TILE // CHUNK
NEG_BIG = -1e30
GRAM_PASSES = 1
SOLVE_PASSES = 1
UW_PASSES = 1
SCAN_PASSES = 1

F32 = jnp.float32
BF16 = jnp.bfloat16


def _dot(a, b):
    return jnp.dot(a, b, preferred_element_type=F32)


def _dot_nt(a, b):
    return lax.dot_general(a, b, (((1,), (1,)), ((), ())), preferred_element_type=F32)


def _split3(x):
    hi = x.astype(BF16)
    r1 = x - hi.astype(F32)
    mid = r1.astype(BF16)
    lo = (r1 - mid.astype(F32)).astype(BF16)
    return hi, mid, lo


def _dot_exact_lhs(m_bf16, x):
    hi, mid, lo = _split3(x)
    return _dot(m_bf16, hi) + _dot(m_bf16, mid) + _dot(m_bf16, lo)


def _split2(x):
    hi = x.astype(BF16)
    return hi, (x - hi.astype(F32)).astype(BF16)


def _dot_f32(a, b, passes, nt=False):
    mm = _dot_nt if nt else _dot
    if passes == 1:
        return mm(a.astype(BF16), b.astype(BF16))
    ah, al = _split2(a)
    bh, bl = _split2(b)
    return mm(ah, bh) + (mm(al, bh) + mm(ah, bl))


def _seg64_sum(x):
    lane = lax.broadcasted_iota(jnp.int32, x.shape, 1)
    for sh in (1, 2, 4, 8, 16, 32):
        up = pltpu.roll(x, LANES - sh, axis=1)
        dn = pltpu.roll(x, sh, axis=1)
        x = x + jnp.where((lane & sh) == 0, up, dn)
    return x


def _sigmoid(x):
    return 1.0 / (1.0 + jnp.exp(-x))


def _silu(x):
    return x * _sigmoid(x)


def _softplus(x):
    return jnp.maximum(x, 0.0) + jnp.log1p(jnp.exp(-jnp.abs(x)))


def _mod_kernel(c_ref, w_ref, b_ref, o_ref):
    a = _silu(c_ref[...])
    o_ref[0] = jnp.dot(a, w_ref[0], preferred_element_type=F32,
                       precision=lax.Precision.HIGHEST) + b_ref[0]


def _modulation(c_all, w_mod, b_mod):
    depth, d, n = w_mod.shape
    rows = c_all.shape[0]
    tn = 1024
    return pl.pallas_call(
        _mod_kernel,
        out_shape=jax.ShapeDtypeStruct((depth, rows, n), F32),
        grid=(depth, n // tn),
        in_specs=[pl.BlockSpec((rows, d), lambda l, j: (0, 0)),
                  pl.BlockSpec((1, d, tn), lambda l, j: (l, 0, j)),
                  pl.BlockSpec((1, 1, tn), lambda l, j: (l, 0, j))],
        out_specs=pl.BlockSpec((1, rows, tn), lambda l, j: (l, 0, j)),
        compiler_params=pltpu.CompilerParams(
            dimension_semantics=("arbitrary", "arbitrary"), vmem_limit_bytes=VMEM_LIMIT),
        name="modulation",
    )(c_all, w_mod, b_mod.reshape(depth, 1, n))


def _head_rms_rope(z, gain, cos, sin_signed, scale):
    ms = _seg64_sum(z * z) * (1.0 / HEAD_DIM)
    y = z * lax.rsqrt(ms + EPS) * gain
    if cos is not None:
        lane = lax.broadcasted_iota(jnp.int32, y.shape, 1)
        partner = jnp.where((lane & ROT_PAIRS) == 0,
                            pltpu.roll(y, LANES - ROT_PAIRS, axis=1),
                            pltpu.roll(y, ROT_PAIRS, axis=1))
        y = y * cos + partner * sin_signed
    if scale != 1.0:
        y = y * scale
    return y


def _project_kernel(use_rope, x_ref, sc_ref, sh_ref, g_ref, wf_ref, dft_ref, wq_ref, wkv_ref,
                    wd_ref, wz_ref, wg_ref, qg_ref, kg_ref, cos_ref, sin_ref,
                    f_ref, q_ref, k_ref, v_ref, d_ref, z_ref, gt_ref):
    x = x_ref[0]
    ms = jnp.mean(x * x, axis=-1, keepdims=True)
    gain = g_ref[...] * (1.0 + sc_ref[0])
    h = (x * lax.rsqrt(ms + EPS) * gain + sh_ref[0]).astype(BF16)

    cos = cos_ref[...] if use_rope else None
    sin = sin_ref[...] if use_rope else None

    f = _dot(h, wf_ref[...]).astype(BF16)
    f_ref[0] = _dot(f, dft_ref[...]).astype(BF16)

    zq = _dot(h, wq_ref[...])
    qg = qg_ref[...]
    for j in range(ATT_CH // LANES):
        sl = slice(j * LANES, (j + 1) * LANES)
        q_ref[0, :, sl] = _head_rms_rope(zq[:, sl], qg, cos, sin, HEAD_DIM ** -0.5).astype(BF16)

    zkv = _dot(h, wkv_ref[...])
    k_ref[0] = _head_rms_rope(zkv[:, :KV_CH], kg_ref[...], cos, sin, 1.0).astype(BF16)
    v_ref[0] = zkv[:, KV_CH:].astype(BF16)

    d_ref[0] = _dot(h, wd_ref[...])
    z_ref[0] = _dot(h, wz_ref[...])
    gt_ref[0] = _dot(h, wg_ref[...])


def _project(x, sc, sh, norm_g, wts, qg, kg, cos_t, sin_t, use_rope, tm):
    B, L, D = x.shape
    per_batch = sc.shape[0] > 1
    mod_map = (lambda b, i: (b, 0, 0)) if per_batch else (lambda b, i: (0, 0, 0))
    full = lambda a: pl.BlockSpec(a.shape, lambda b, i: (0,) * a.ndim)
    row = lambda n: pl.BlockSpec((1, tm, n), lambda b, i: (b, i, 0))
    wf, dft, wq, wkv, wd, wz, wg = wts
    out_shape = (jax.ShapeDtypeStruct((B, L, 2 * FT_CH), BF16),
                 jax.ShapeDtypeStruct((B, L, ATT_CH), BF16),
                 jax.ShapeDtypeStruct((B, L, KV_CH), BF16),
                 jax.ShapeDtypeStruct((B, L, KV_CH), BF16),
                 jax.ShapeDtypeStruct((B, L, CONV_CH), F32),
                 jax.ShapeDtypeStruct((B, L, DN_CH), F32),
                 jax.ShapeDtypeStruct((B, L, LANES), F32))
    return pl.pallas_call(
        functools.partial(_project_kernel, use_rope),
        out_shape=out_shape,
        grid=(B, L // tm),
        in_specs=[row(D),
                  pl.BlockSpec((1, 1, D), mod_map), pl.BlockSpec((1, 1, D), mod_map),
                  full(norm_g), full(wf), full(dft), full(wq), full(wkv), full(wd), full(wz),
                  full(wg), full(qg), full(kg),
                  pl.BlockSpec((tm, LANES), lambda b, i: (i, 0)),
                  pl.BlockSpec((tm, LANES), lambda b, i: (i, 0))],
        out_specs=(row(2 * FT_CH), row(ATT_CH), row(KV_CH), row(KV_CH), row(CONV_CH),
                   row(DN_CH), row(LANES)),
        compiler_params=pltpu.CompilerParams(
            dimension_semantics=("arbitrary", "arbitrary"), vmem_limit_bytes=VMEM_LIMIT),
        name="project_rope" if use_rope else "project",
    )(x, sc, sh, norm_g, wf, dft, wq, wkv, wd, wz, wg, qg, kg, cos_t, sin_t)


def _fourier_kernel(scale, dc_ref, ds_ref, f_ref, o_ref):
    fcs = f_ref[0]
    y = _dot(dc_ref[...], fcs[:, :FT_CH]) - _dot(ds_ref[...], fcs[:, FT_CH:])
    o_ref[0] = (y * scale).astype(BF16)


def _fourier(fcs, dft_c, dft_s, tn):
    B, L, _ = fcs.shape
    scale = float(1.0 / np.sqrt(L * HEAD_DIM))
    return pl.pallas_call(
        functools.partial(_fourier_kernel, scale),
        out_shape=jax.ShapeDtypeStruct((B, L, FT_CH), BF16),
        grid=(L // tn, B),
        in_specs=[pl.BlockSpec((tn, L), lambda n, b: (n, 0)),
                  pl.BlockSpec((tn, L), lambda n, b: (n, 0)),
                  pl.BlockSpec((1, L, 2 * FT_CH), lambda n, b: (b, 0, 0))],
        out_specs=pl.BlockSpec((1, tn, FT_CH), lambda n, b: (b, n, 0)),
        compiler_params=pltpu.CompilerParams(
            dimension_semantics=("arbitrary", "arbitrary"), vmem_limit_bytes=VMEM_LIMIT),
        name="fourier",
    )(dft_c, dft_s, fcs)


def _attention_kernel(n_src, q_ref, *refs):
    kv_refs = refs[:2 * n_src]
    o_ref = refs[2 * n_src]
    tq = q_ref.shape[1]
    for h in range(ATT_HKV):
        q = q_ref[0, :, h * ATT_GROUP * HEAD_DIM:(h + 1) * ATT_GROUP * HEAD_DIM]
        q4 = jnp.concatenate([q[:, g * HEAD_DIM:(g + 1) * HEAD_DIM] for g in range(ATT_GROUP)],
                             axis=0)
        scores = []
        m = None
        for s in range(n_src):
            k = kv_refs[2 * s][0, :, h * HEAD_DIM:(h + 1) * HEAD_DIM]
            sc = _dot_nt(q4, k)
            scores.append(sc)
            ms = jnp.max(sc, axis=-1, keepdims=True)
            m = ms if m is None else jnp.maximum(m, ms)
        acc = None
        for s in range(n_src):
            v = kv_refs[2 * s + 1][0]
            lane = lax.broadcasted_iota(jnp.int32, v.shape, 1)
            in_head = (lane >= h * HEAD_DIM) & (lane < (h + 1) * HEAD_DIM)
            v_aug = jnp.where(in_head, v, jnp.ones_like(v))
            p = jnp.exp(scores[s] - m).astype(BF16)
            pv = _dot(p, v_aug)
            acc = pv if acc is None else acc + pv
        o = acc[:, h * HEAD_DIM:(h + 1) * HEAD_DIM]
        den = acc[:, (1 - h) * HEAD_DIM:(2 - h) * HEAD_DIM]
        o = (o / den).astype(BF16)
        for g in range(ATT_GROUP):
            c0 = (h * ATT_GROUP + g) * HEAD_DIM
            o_ref[0, :, c0:c0 + HEAD_DIM] = o[g * tq:(g + 1) * tq]


def _attention(q, kv_sources, tq):
    B, L, _ = q.shape
    n_src = len(kv_sources)
    in_specs = [pl.BlockSpec((1, tq, ATT_CH), lambda b, i: (b, i, 0))]
    args = [q]
    for k, v in kv_sources:
        lk = k.shape[1]
        in_specs += [pl.BlockSpec((1, lk, KV_CH), lambda b, i: (b, 0, 0)),
                     pl.BlockSpec((1, lk, KV_CH), lambda b, i: (b, 0, 0))]
        args += [k, v]
    return pl.pallas_call(
        functools.partial(_attention_kernel, n_src),
        out_shape=jax.ShapeDtypeStruct((B, L, ATT_CH), BF16),
        grid=(B, L // tq),
        in_specs=in_specs,
        out_specs=pl.BlockSpec((1, tq, ATT_CH), lambda b, i: (b, i, 0)),
        compiler_params=pltpu.CompilerParams(
            dimension_semantics=("arbitrary", "arbitrary"), vmem_limit_bytes=VMEM_LIMIT),
        name="attention",
    )(*args)


def _lane_col(x, c):
    lane = lax.broadcasted_iota(jnp.int32, x.shape, 1)
    return jnp.sum(jnp.where(lane == c, x, 0.0), axis=1, keepdims=True)


def _head_bcast(cols, lane_head):
    out = cols[DN_HEADS - 1]
    for h in range(DN_HEADS - 2, -1, -1):
        out = jnp.where(lane_head <= h, cols[h], out)
    return out


def _conv_silu_norm(x, prev_row, next_row, w_ref):
    n = x.shape[0]
    row = lax.broadcasted_iota(jnp.int32, x.shape, 0)
    x_m1 = jnp.where(row == 0, prev_row, pltpu.roll(x, 1, axis=0))
    x_p1 = jnp.where(row == n - 1, next_row, pltpu.roll(x, n - 1, axis=0))
    y = _silu(x_m1 * w_ref[0:1, :] + x * w_ref[1:2, :] + x_p1 * w_ref[2:3, :])
    outs = []
    for j in range(CONV_CH // LANES):
        slab = y[:, j * LANES:(j + 1) * LANES]
        if j < 2 * DN_CH // LANES:
            ss = _seg64_sum(slab * slab)
            slab = slab * lax.rsqrt(ss + EPS)
            if j < DN_CH // LANES:
                slab = slab * (HEAD_DIM ** -0.5)
        outs.append(slab)
    return jnp.concatenate(outs, axis=1)


def _gates_to_beta_g(z, alog_ref, dtb_ref):
    lane = lax.broadcasted_iota(jnp.int32, z.shape, 1)
    g = -jnp.exp(alog_ref[...]) * _softplus(z + dtb_ref[...])
    return jnp.where(lane < 2 * DN_HEADS, _sigmoid(z), g)


def _gdn_tile(direction, qkv, bg, s_ref, masks):
    cum_ref, ones_ref = masks
    cum_m = cum_ref[...]
    ones_m = ones_ref[...]
    ri = lax.broadcasted_iota(jnp.int32, (GDN_TILE, GDN_TILE), 0)
    ci = lax.broadcasted_iota(jnp.int32, (GDN_TILE, GDN_TILE), 1)
    blk = (ri // CHUNK) == (ci // CHUNK)
    lane_head = ci // HEAD_DIM
    if direction == 0:
        tri_strict, tri_incl = blk & (ri > ci), blk & (ri >= ci)
    else:
        tri_strict, tri_incl = blk & (ri < ci), blk & (ri <= ci)
    q = qkv[:, 0:DN_CH]
    k = qkv[:, DN_CH:2 * DN_CH]
    v = qkv[:, 2 * DN_CH:3 * DN_CH]

    gc = _dot_exact_lhs(cum_m, bg)
    gt = _dot_exact_lhs(ones_m, bg)
    gc_t = gc.T
    row_t = lax.broadcasted_iota(jnp.int32, gc_t.shape, 0)

    beta_cols, gc_cols, gt_cols = [], [], []
    for h in range(DN_HEADS):
        beta_cols.append(_lane_col(bg, direction * DN_HEADS + h))
        gc_cols.append(_lane_col(gc, 2 * DN_HEADS + direction * DN_HEADS + h))
        gt_cols.append(_lane_col(gt, 2 * DN_HEADS + direction * DN_HEADS + h))
    beta_b = _head_bcast(beta_cols, lane_head)
    gc_b = _head_bcast(gc_cols, lane_head)
    gt_b = _head_bcast(gt_cols, lane_head)

    e_gc = jnp.exp(gc_b)
    kb = k * beta_b
    vb = v * beta_b
    kbe = kb * e_gc
    qe = q * e_gc
    kd = k * jnp.exp(gt_b - gc_b)
    e_last = jnp.exp(gt_b)
    kd_t = kd.T

    rhs_uw = jnp.concatenate([vb, kbe], axis=1)
    eye = (lax.broadcasted_iota(jnp.int32, (GDN_TILE, GDN_TILE), 0)
           == lax.broadcasted_iota(jnp.int32, (GDN_TILE, GDN_TILE), 1)).astype(F32)

    heads = range(DN_HEADS)
    neg_a, qk_heads = [], []
    for h in heads:
        in_h = lane_head == h
        r = jnp.sum(jnp.where(row_t == 2 * DN_HEADS + direction * DN_HEADS + h, gc_t, 0.0),
                    axis=0, keepdims=True)
        diff = gc_cols[h] - r
        dec_s = jnp.exp(jnp.where(tri_strict, diff, NEG_BIG))
        dec_i = jnp.exp(jnp.where(tri_incl, diff, NEG_BIG))
        kk = _dot_f32(jnp.where(in_h, kb, 0.0), k, GRAM_PASSES, nt=True)
        qk = _dot_f32(jnp.where(in_h, q, 0.0), k, GRAM_PASSES, nt=True)
        neg_a.append(-(kk * dec_s))
        qk_heads.append(qk * dec_i)

    t = [eye + jnp.where((ri // 2) == (ci // 2), neg_a[h], 0.0) for h in heads]
    b = 2
    while b < CHUNK:
        lvl = ((ri // (2 * b)) == (ci // (2 * b))) & ((ri // b) != (ci // b))
        g = [_dot_f32(jnp.where(lvl, neg_a[h], 0.0), t[h], SOLVE_PASSES) for h in heads]
        t = [t[h] + _dot_f32(t[h], g[h], SOLVE_PASSES) for h in heads]
        b *= 2
    uw = [_dot_f32(t[h], rhs_uw, UW_PASSES) for h in heads]
    u_all, w_all = uw[DN_HEADS - 1][:, :DN_CH], uw[DN_HEADS - 1][:, DN_CH:]
    for h in range(DN_HEADS - 2, -1, -1):
        u_all = jnp.where(lane_head <= h, uw[h][:, :DN_CH], u_all)
        w_all = jnp.where(lane_head <= h, uw[h][:, DN_CH:], w_all)

    order = range(CHUNKS_PER_TILE) if direction == 0 else range(CHUNKS_PER_TILE - 1, -1, -1)
    outs = [None] * CHUNKS_PER_TILE
    zeros_c = jnp.zeros((CHUNK, DN_CH), F32)
    lane_head_c = lax.broadcasted_iota(jnp.int32, (CHUNK, DN_CH), 1) // HEAD_DIM
    for c in order:
        rows = slice(c * CHUNK, (c + 1) * CHUNK)
        s = s_ref[...]
        ws = _dot_f32(jnp.concatenate([w_all[rows], qe[rows]], axis=0), s, SCAN_PASSES)
        v_new = u_all[rows] - ws[:CHUNK]
        v_tile = jnp.concatenate([v_new if i == c else zeros_c for i in range(CHUNKS_PER_TILE)],
                                 axis=0)
        lhs = jnp.concatenate([kd_t] + [qk_heads[h][rows] for h in range(DN_HEADS)], axis=0)
        r2 = _dot_f32(lhs, v_tile, SCAN_PASSES)
        s_ref[...] = s * e_last[c * CHUNK:c * CHUNK + 1, :] + jnp.where(blk, r2[:GDN_TILE], 0.0)
        o = ws[CHUNK:]
        for h in range(DN_HEADS):
            o = o + jnp.where(lane_head_c == h,
                              r2[GDN_TILE + h * CHUNK:GDN_TILE + (h + 1) * CHUNK], 0.0)
        outs[c] = o
    return jnp.concatenate(outs, axis=0)


def _gdn_kernel(n_lat, write_ctx, dc_ref, dl_ref, gc_ref, gl_ref, zc_ref, zl_ref, cw_ref,
                alog_ref, dtb_ref, on_ref, cum_f_ref, cum_b_ref, ones_ref,
                ol_ref, oc_ref, qkv_ref, bg_ref, of_ref, ob_ref, sf_ref, sb_ref):
    T = GDN_TILE
    zero_row = jnp.zeros((1, CONV_CH), F32)

    qkv_ref[0:T] = _conv_silu_norm(dc_ref[0], zero_row, zero_row, cw_ref)
    bg_ref[0:T] = _gates_to_beta_g(gc_ref[0], alog_ref, dtb_ref)

    def prep(t, carry):
        r0 = pl.multiple_of(t * T, T)
        x = dl_ref[0, pl.ds(r0, T), :]
        prev_row = dl_ref[0, pl.ds(jnp.maximum(r0 - 1, 0), 1), :]
        next_row = dl_ref[0, pl.ds(jnp.minimum(r0 + T, n_lat * T - 1), 1), :]
        prev_row = jnp.where(t == 0, zero_row, prev_row)
        next_row = jnp.where(t == n_lat - 1, zero_row, next_row)
        o0 = pl.multiple_of(r0 + T, T)
        qkv_ref[pl.ds(o0, T), :] = _conv_silu_norm(x, prev_row, next_row, cw_ref)
        bg_ref[pl.ds(o0, T), :] = _gates_to_beta_g(gl_ref[0, pl.ds(r0, T), :], alog_ref, dtb_ref)
        return carry

    lax.fori_loop(0, n_lat, prep, 0)

    sf_ref[...] = jnp.zeros_like(sf_ref)
    sb_ref[...] = jnp.zeros_like(sb_ref)

    def scan(s, carry):
        tf = pl.multiple_of(s * T, T)
        tb = pl.multiple_of(jnp.where(s == 0, 0, n_lat + 1 - s) * T, T)
        of_ref[pl.ds(tf, T), :] = _gdn_tile(0, qkv_ref[pl.ds(tf, T), :], bg_ref[pl.ds(tf, T), :],
                                            sf_ref, (cum_f_ref, ones_ref))
        ob_ref[pl.ds(tb, T), :] = _gdn_tile(1, qkv_ref[pl.ds(tb, T), :], bg_ref[pl.ds(tb, T), :],
                                            sb_ref, (cum_b_ref, ones_ref))
        return carry

    lax.fori_loop(0, n_lat + 1, scan, 0)

    def finish(o, z):
        slabs = []
        for j in range(DN_CH // LANES):
            sl = slice(j * LANES, (j + 1) * LANES)
            ms = _seg64_sum(o[:, sl] * o[:, sl]) * (1.0 / HEAD_DIM)
            slabs.append(o[:, sl] * lax.rsqrt(ms + EPS) * on_ref[...] * _silu(z[:, sl]))
        return jnp.concatenate(slabs, axis=1).astype(BF16)

    if write_ctx:
        oc_ref[0] = finish(of_ref[0:T] + ob_ref[0:T], zc_ref[0])
    else:
        oc_ref[0] = jnp.zeros(oc_ref.shape[1:], BF16)

    def fin(t, carry):
        r0 = pl.multiple_of(t * T, T)
        o0 = pl.multiple_of(r0 + T, T)
        ol_ref[0, pl.ds(r0, T), :] = finish(of_ref[pl.ds(o0, T), :] + ob_ref[pl.ds(o0, T), :],
                                            zl_ref[0, pl.ds(r0, T), :])
        return carry

    lax.fori_loop(0, n_lat, fin, 0)


def _gdn(d_ctx, d_lat, g_ctx, g_lat, z_ctx, z_lat, conv_w, alog_row, dtb_row, on_row, consts,
         write_ctx):
    B, L, _ = d_lat.shape
    Lc = d_ctx.shape[1]
    assert Lc == GDN_TILE and L % GDN_TILE == 0
    n_lat = L // GDN_TILE
    cum_f, cum_b, ones_m = consts
    per_b = lambda n, c: pl.BlockSpec((1, n, c), lambda b: (b, 0, 0))
    full = lambda a: pl.BlockSpec(a.shape, lambda b: (0,) * a.ndim)
    tot = L + Lc
    return pl.pallas_call(
        functools.partial(_gdn_kernel, n_lat, write_ctx),
        out_shape=(jax.ShapeDtypeStruct((B, L, DN_CH), BF16),
                   jax.ShapeDtypeStruct((B, Lc, DN_CH), BF16)),
        grid=(B,),
        in_specs=[per_b(Lc, CONV_CH), per_b(L, CONV_CH), per_b(Lc, LANES), per_b(L, LANES),
                  per_b(Lc, DN_CH), per_b(L, DN_CH), full(conv_w), full(alog_row), full(dtb_row),
                  full(on_row), full(cum_f), full(cum_b), full(ones_m)],
        out_specs=(per_b(L, DN_CH), per_b(Lc, DN_CH)),
        scratch_shapes=[pltpu.VMEM((tot, CONV_CH), F32), pltpu.VMEM((tot, LANES), F32),
                        pltpu.VMEM((tot, DN_CH), F32), pltpu.VMEM((tot, DN_CH), F32),
                        pltpu.VMEM((GDN_TILE, DN_CH), F32), pltpu.VMEM((GDN_TILE, DN_CH), F32)],
        compiler_params=pltpu.CompilerParams(
            dimension_semantics=("arbitrary",), vmem_limit_bytes=VMEM_LIMIT),
        name="gdn",
    )(d_ctx, d_lat, g_ctx, g_lat, z_ctx, z_lat, conv_w, alog_row, dtb_row, on_row,
      cum_f, cum_b, ones_m)


def _mix_mlp_kernel(ff_tile, x_ref, fy_ref, at_ref, dn_ref, g1_ref, sc_ref, sh_ref, g2_ref, ng_ref,
                    wof_ref, woa_ref, wod_ref, w1_ref, w2_ref, o_ref):
    mix = (_dot(fy_ref[0], wof_ref[...]) + _dot(at_ref[0], woa_ref[...])
           + _dot(dn_ref[0], wod_ref[...]))
    x1 = x_ref[0] + g1_ref[0] * mix
    ms = jnp.mean(x1 * x1, axis=-1, keepdims=True)
    gain = ng_ref[...] * (1.0 + sc_ref[0])
    h = (x1 * lax.rsqrt(ms + EPS) * gain + sh_ref[0]).astype(BF16)
    acc = jnp.zeros_like(x1)
    for c in range(w1_ref.shape[1] // ff_tile):
        u = jnp.maximum(_dot(h, w1_ref[:, c * ff_tile:(c + 1) * ff_tile]), 0.0)
        acc = acc + _dot((u * u).astype(BF16), w2_ref[c * ff_tile:(c + 1) * ff_tile, :])
    o_ref[0] = x1 + g2_ref[0] * acc


def _mix_mlp(x, fy, att, dn, g1, sc, sh, g2, norm_g, wts, tm):
    B, L, D = x.shape
    per_batch = g1.shape[0] > 1
    mod_map = (lambda b, i: (b, 0, 0)) if per_batch else (lambda b, i: (0, 0, 0))
    mod = pl.BlockSpec((1, 1, D), mod_map)
    full = lambda a: pl.BlockSpec(a.shape, lambda b, i: (0,) * a.ndim,
                                  pipeline_mode=pl.Buffered(1))
    row = lambda n: pl.BlockSpec((1, tm, n), lambda b, i: (b, i, 0))
    wof, woa, wod, w1, w2 = wts
    return pl.pallas_call(
        functools.partial(_mix_mlp_kernel, 1024),
        out_shape=jax.ShapeDtypeStruct((B, L, D), F32),
        grid=(B, L // tm),
        in_specs=[row(D), row(FT_CH), row(ATT_CH), row(DN_CH), mod, mod, mod, mod,
                  pl.BlockSpec(norm_g.shape, lambda b, i: (0, 0)),
                  full(wof), full(woa), full(wod), full(w1), full(w2)],
        out_specs=row(D),
        compiler_params=pltpu.CompilerParams(
            dimension_semantics=("arbitrary", "arbitrary"), vmem_limit_bytes=VMEM_LIMIT),
        name="mix_mlp",
    )(x, fy, att, dn, g1, sc, sh, g2, norm_g, wof, woa, wod, w1, w2)


def _rope_tables(S):
    rows = S // GRID_W
    t_row = jnp.repeat(jnp.arange(rows), GRID_W).astype(F32)
    t_col = jnp.tile(jnp.arange(GRID_W), rows).astype(F32)
    inv_freq = ROPE_THETA ** (-jnp.arange(ROT_PAIRS, dtype=F32) * 2.0 / AXIS_DIM)
    ang_r = t_row[:, None] * inv_freq
    ang_c = t_col[:, None] * inv_freq
    ang = jnp.concatenate([ang_r, ang_r, ang_c, ang_c], axis=-1)
    cos, sin = jnp.cos(ang), jnp.sin(ang)
    sign = jnp.where((jnp.arange(HEAD_DIM) & ROT_PAIRS) == 0, -1.0, 1.0).astype(F32)
    tile2 = lambda a: jnp.concatenate([a, a], axis=-1)
    return tile2(cos), tile2(sin * sign)


def _dft_tables(n):
    k = (jnp.arange(n, dtype=jnp.int32)[:, None] * jnp.arange(n, dtype=jnp.int32)[None, :]) % n
    ang = k.astype(F32) * np.float32(2.0 * np.pi / n)
    return jnp.cos(ang), jnp.sin(ang)


def _channel_dft():
    c, s = _dft_tables(HEAD_DIM)
    eye = jnp.eye(FT_GROUPS, dtype=F32)
    return jnp.concatenate([jnp.kron(eye, c), jnp.kron(eye, s)], axis=1).astype(BF16)


def _gdn_consts():
    i = np.arange(GDN_TILE)
    blk = (i[:, None] // CHUNK) == (i[None, :] // CHUNK)
    cum_f = (blk & (i[:, None] >= i[None, :])).astype(np.float32)
    cum_b = (blk & (i[:, None] <= i[None, :])).astype(np.float32)
    return (jnp.asarray(cum_f, BF16), jnp.asarray(cum_b, BF16), jnp.asarray(blk.astype(np.float32), BF16))


def _pad_lanes(a):
    flat = a.reshape(1, -1).astype(F32)
    return jnp.pad(flat, ((0, 0), (0, LANES - flat.shape[1])))


def kernel(x, c, ctx, c_ctx, norm1_g, norm2_g, w_mod, b_mod, w_in, conv_w, q_norm_g, k_norm_g,
           a_log, dt_bias, o_norm_g, w_out, w_ff1, w_ff2):
    B, S, D = x.shape
    Lc = ctx.shape[1]
    depth = w_mod.shape[0]

    rows = ((B + 1 + 7) // 8) * 8
    c_all = jnp.concatenate([c, c_ctx[None, :], jnp.zeros((rows - B - 1, D), F32)], axis=0)
    mod_all = _modulation(c_all, w_mod, b_mod)

    cos_t, sin_t = _rope_tables(S)
    dft_ch = _channel_dft()
    dft_lat = tuple(t.astype(BF16) for t in _dft_tables(S))
    dft_ctx = tuple(t.astype(BF16) for t in _dft_tables(Lc))
    gdn_consts = _gdn_consts()

    o0 = FT_CH
    o1 = o0 + ATT_CH
    o2 = o1 + 2 * KV_CH
    o3 = o2 + CONV_CH
    o4 = o3 + DN_CH

    x_lat, x_ctx = x, ctx
    for l in range(depth):
        last = l == depth - 1
        w = w_in[l]
        wg = jnp.pad(w[:, o4:], ((0, 0), (0, LANES - N_GATES)))
        in_wts = (w[:, :o0].astype(BF16), dft_ch, w[:, o0:o1].astype(BF16), w[:, o1:o2].astype(BF16),
                  w[:, o2:o3].astype(BF16), w[:, o3:o4].astype(BF16), wg.astype(BF16))
        qg = jnp.tile(q_norm_g[l], 2).reshape(1, LANES)
        kg = jnp.tile(k_norm_g[l], 2).reshape(1, LANES)
        n1 = norm1_g[l].reshape(1, D)
        n2 = norm2_g[l].reshape(1, D)
        wo = w_out[l].astype(BF16)
        out_wts = (wo[:FT_CH], wo[FT_CH:FT_CH + ATT_CH], wo[FT_CH + ATT_CH:],
                   w_ff1[l].astype(BF16), w_ff2[l].astype(BF16))
        alog_row = jnp.pad(a_log[l].reshape(1, -1), ((0, 0), (2 * DN_HEADS, LANES - N_GATES)))
        dtb_row = jnp.pad(dt_bias[l].reshape(1, -1), ((0, 0), (2 * DN_HEADS, LANES - N_GATES)))
        on_row = jnp.tile(o_norm_g[l], 2).reshape(1, LANES)

        mod = mod_all[l, :B].reshape(B, 1, 6 * D)
        modc = mod_all[l, B:B + 1].reshape(1, 1, 6 * D)
        sh1, sc1, g1, sh2, sc2, g2 = [mod[:, :, i * D:(i + 1) * D] for i in range(6)]
        csh1, csc1, cg1, csh2, csc2, cg2 = [modc[:, :, i * D:(i + 1) * D] for i in range(6)]

        fl, ql, kl, vl, dl, zl, gl = _project(x_lat, sc1, sh1, n1, in_wts, qg, kg, cos_t, sin_t,
                                              True, 512)
        fc, qc, kc, vc, dc, zc, gc = _project(x_ctx, csc1, csh1, n1, in_wts, qg, kg,
                                              cos_t[:Lc], sin_t[:Lc], False, Lc)

        dn_lat, dn_ctx = _gdn(dc, dl, gc, gl, zc, zl, conv_w[l], alog_row, dtb_row, on_row,
                              gdn_consts, not last)
        att_lat = _attention(ql, [(kl, vl), (kc, vc)], 128)
        fy_lat = _fourier(fl, dft_lat[0], dft_lat[1], 512)
        x_lat = _mix_mlp(x_lat, fy_lat, att_lat, dn_lat, g1, sc2, sh2, g2, n2, out_wts, 512)
        if not last:
            att_ctx = _attention(qc, [(kc, vc)], 128)
            fy_ctx = _fourier(fc, dft_ctx[0], dft_ctx[1], Lc)
            x_ctx = _mix_mlp(x_ctx, fy_ctx, att_ctx, dn_ctx, cg1, csc2, csh2, cg2, n2, out_wts, Lc)
    return x_lat
```

```python
import functools

import jax
import jax.numpy as jnp
import numpy as np
from jax import lax
from jax.experimental import pallas as pl
from jax.experimental.pallas import tpu as pltpu

HEAD_DIM = 64
FT_GROUPS = 4
FT_CH = FT_GROUPS * HEAD_DIM
ATT_HQ = 8
ATT_HKV = 2
ATT_GROUP = ATT_HQ // ATT_HKV
ATT_CH = ATT_HQ * HEAD_DIM
KV_CH = ATT_HKV * HEAD_DIM
DN_HEADS = 4
DN_CH = DN_HEADS * HEAD_DIM
CONV_CH = 3 * DN_CH
N_GATES = 4 * DN_HEADS
GRID_W = 64
ROPE_THETA = 10000.0
AXIS_DIM = HEAD_DIM // 2
ROT_PAIRS = AXIS_DIM // 2
EPS = 1e-6
CHUNK = 64

LANES = 128
MXU_DIM = 256
V7X_VMEM_BYTES = 64 * 1024 * 1024
VMEM_LIMIT = V7X_VMEM_BYTES - 8 * 1024 * 1024

GDN_TILE = 256
CHUNKS_PER_TILE = GDN_TILE // CHUNK
NEG_BIG = -1e30
SCAN_STAGE_PERIOD = 1

F32 = jnp.float32
BF16 = jnp.bfloat16


def _dot(a, b):
    return jnp.dot(a, b, preferred_element_type=F32)


def _dot_nt(a, b):
    return lax.dot_general(a, b, (((1,), (1,)), ((), ())), preferred_element_type=F32)


def _split3(x):
    hi = x.astype(BF16)
    r1 = x - hi.astype(F32)
    mid = r1.astype(BF16)
    lo = (r1 - mid.astype(F32)).astype(BF16)
    return hi, mid, lo


def _dot_exact_lhs(m_bf16, x):
    hi, mid, lo = _split3(x)
    return _dot(m_bf16, hi) + _dot(m_bf16, mid) + _dot(m_bf16, lo)


def _seg64_sum(x, passes):
    w = x.shape[1]
    blk = min(w, MXU_DIM)
    r = lax.broadcasted_iota(jnp.int32, (blk, blk), 0) // HEAD_DIM
    c = lax.broadcasted_iota(jnp.int32, (blk, blk), 1) // HEAD_DIM
    ones = jnp.where(r == c, 1.0, 0.0).astype(BF16)
    outs = []
    for j in range(w // blk):
        xs = x[:, j * blk:(j + 1) * blk]
        hi = xs.astype(BF16)
        acc = _dot(hi, ones)
        if passes == 2:
            acc = acc + _dot((xs - hi.astype(F32)).astype(BF16), ones)
        outs.append(acc)
    return outs[0] if len(outs) == 1 else jnp.concatenate(outs, axis=1)


def _sigmoid(x):
    return 1.0 / (1.0 + jnp.exp(-x))


def _silu(x):
    return x * _sigmoid(x)


def _softplus(x):
    return jnp.maximum(x, 0.0) + jnp.log1p(jnp.exp(-jnp.abs(x)))


def _mod_kernel(c_ref, w_ref, b_ref, o_ref):
    a = _silu(c_ref[...])
    o_ref[0] = jnp.dot(a, w_ref[0], preferred_element_type=F32,
                       precision=lax.Precision.HIGHEST) + b_ref[0]


def _modulation(c_all, w_mod, b_mod):
    depth, d, n = w_mod.shape
    rows = c_all.shape[0]
    tn = 1024
    return pl.pallas_call(
        _mod_kernel,
        out_shape=jax.ShapeDtypeStruct((depth, rows, n), F32),
        grid=(depth, n // tn),
        in_specs=[pl.BlockSpec((rows, d), lambda l, j: (0, 0)),
                  pl.BlockSpec((1, d, tn), lambda l, j: (l, 0, j)),
                  pl.BlockSpec((1, 1, tn), lambda l, j: (l, 0, j))],
        out_specs=pl.BlockSpec((1, rows, tn), lambda l, j: (l, 0, j)),
        compiler_params=pltpu.CompilerParams(
            dimension_semantics=("arbitrary", "arbitrary"), vmem_limit_bytes=VMEM_LIMIT),
        name="modulation",
    )(c_all, w_mod, b_mod.reshape(depth, 1, n))


def _head_rms_rope(z, gain, cos, sin_signed, scale):
    ms = _seg64_sum(z * z, 1) * (1.0 / HEAD_DIM)
    y = z * lax.rsqrt(ms + EPS) * gain
    if scale != 1.0:
        y = y * scale
    if cos is None:
        return y
    lane = lax.broadcasted_iota(jnp.int32, cos.shape, 1)
    slabs = []
    for j in range(z.shape[1] // LANES):
        ys = y[:, j * LANES:(j + 1) * LANES]
        partner = jnp.where((lane & ROT_PAIRS) == 0,
                            pltpu.roll(ys, LANES - ROT_PAIRS, axis=1),
                            pltpu.roll(ys, ROT_PAIRS, axis=1))
        slabs.append(ys * cos + partner * sin_signed)
    return slabs[0] if len(slabs) == 1 else jnp.concatenate(slabs, axis=1)


def _project_kernel(use_rope, x_ref, sc_ref, sh_ref, g_ref, wf_ref, dft_ref, wq_ref, wkv_ref,
                    wd_ref, wz_ref, wg_ref, qg_ref, kg_ref, cos_ref, sin_ref,
                    f_ref, q_ref, k_ref, v_ref, d_ref, z_ref, gt_ref):
    x = x_ref[0]
    ms = jnp.mean(x * x, axis=-1, keepdims=True)
    gain = g_ref[...] * (1.0 + sc_ref[0])
    h = (x * lax.rsqrt(ms + EPS) * gain + sh_ref[0]).astype(BF16)

    cos = cos_ref[...] if use_rope else None
    sin = sin_ref[...] if use_rope else None

    f = _dot(h, wf_ref[...]).astype(BF16)
    f_ref[0] = _dot(f, dft_ref[...]).astype(BF16)

    zq = _dot(h, wq_ref[...])
    q_ref[0] = _head_rms_rope(zq, qg_ref[...], cos, sin, HEAD_DIM ** -0.5).astype(BF16)

    zkv = _dot(h, wkv_ref[...])
    k_ref[0] = _head_rms_rope(zkv[:, :KV_CH], kg_ref[...], cos, sin, 1.0).astype(BF16)
    v_ref[0] = zkv[:, KV_CH:].astype(BF16)

    d_ref[0] = _dot(h, wd_ref[...])
    z_ref[0] = _dot(h, wz_ref[...])
    gt_ref[0] = _dot(h, wg_ref[...])


def _project(x, sc, sh, norm_g, wts, qg, kg, cos_t, sin_t, use_rope, tm):
    B, L, D = x.shape
    per_batch = sc.shape[0] > 1
    mod_map = (lambda b, i: (b, 0, 0)) if per_batch else (lambda b, i: (0, 0, 0))
    full = lambda a: pl.BlockSpec(a.shape, lambda b, i: (0,) * a.ndim)
    row = lambda n: pl.BlockSpec((1, tm, n), lambda b, i: (b, i, 0))
    wf, dft, wq, wkv, wd, wz, wg = wts
    out_shape = (jax.ShapeDtypeStruct((B, L, 2 * FT_CH), BF16),
                 jax.ShapeDtypeStruct((B, L, ATT_CH), BF16),
                 jax.ShapeDtypeStruct((B, L, KV_CH), BF16),
                 jax.ShapeDtypeStruct((B, L, KV_CH), BF16),
                 jax.ShapeDtypeStruct((B, L, CONV_CH), F32),
                 jax.ShapeDtypeStruct((B, L, DN_CH), F32),
                 jax.ShapeDtypeStruct((B, L, LANES), F32))
    return pl.pallas_call(
        functools.partial(_project_kernel, use_rope),
        out_shape=out_shape,
        grid=(B, L // tm),
        in_specs=[row(D),
                  pl.BlockSpec((1, 1, D), mod_map), pl.BlockSpec((1, 1, D), mod_map),
                  full(norm_g), full(wf), full(dft), full(wq), full(wkv), full(wd), full(wz),
                  full(wg), full(qg), full(kg),
                  pl.BlockSpec((tm, LANES), lambda b, i: (i, 0)),
                  pl.BlockSpec((tm, LANES), lambda b, i: (i, 0))],
        out_specs=(row(2 * FT_CH), row(ATT_CH), row(KV_CH), row(KV_CH), row(CONV_CH),
                   row(DN_CH), row(LANES)),
        compiler_params=pltpu.CompilerParams(
            dimension_semantics=("arbitrary", "arbitrary"), vmem_limit_bytes=VMEM_LIMIT),
        name="project_rope" if use_rope else "project",
    )(x, sc, sh, norm_g, wf, dft, wq, wkv, wd, wz, wg, qg, kg, cos_t, sin_t)


def _fourier_kernel(scale, dc_ref, ds_ref, f_ref, o_ref):
    fcs = f_ref[0]
    y = _dot(dc_ref[...], fcs[:, :FT_CH]) - _dot(ds_ref[...], fcs[:, FT_CH:])
    o_ref[0] = (y * scale).astype(BF16)


def _fourier(fcs, dft_c, dft_s, tn):
    B, L, _ = fcs.shape
    scale = float(1.0 / np.sqrt(L * HEAD_DIM))
    return pl.pallas_call(
        functools.partial(_fourier_kernel, scale),
        out_shape=jax.ShapeDtypeStruct((B, L, FT_CH), BF16),
        grid=(L // tn, B),
        in_specs=[pl.BlockSpec((tn, L), lambda n, b: (n, 0)),
                  pl.BlockSpec((tn, L), lambda n, b: (n, 0)),
                  pl.BlockSpec((1, L, 2 * FT_CH), lambda n, b: (b, 0, 0))],
        out_specs=pl.BlockSpec((1, tn, FT_CH), lambda n, b: (b, n, 0)),
        compiler_params=pltpu.CompilerParams(
            dimension_semantics=("arbitrary", "arbitrary"), vmem_limit_bytes=VMEM_LIMIT),
        name="fourier",
    )(dft_c, dft_s, fcs)


def _attention_kernel(n_src, q_ref, *refs):
    kv_refs = refs[:2 * n_src]
    o_ref = refs[2 * n_src]
    tq = q_ref.shape[1]
    kv_heads = range(ATT_HKV)
    scores, row_max = [], []
    for h in kv_heads:
        q = q_ref[0, :, h * ATT_GROUP * HEAD_DIM:(h + 1) * ATT_GROUP * HEAD_DIM]
        q4 = jnp.concatenate([q[:, g * HEAD_DIM:(g + 1) * HEAD_DIM] for g in range(ATT_GROUP)],
                             axis=0)
        sc_h, m = [], None
        for s in range(n_src):
            k = kv_refs[2 * s][0, :, h * HEAD_DIM:(h + 1) * HEAD_DIM]
            sc = _dot_nt(q4, k)
            sc_h.append(sc)
            ms = jnp.max(sc, axis=-1, keepdims=True)
            m = ms if m is None else jnp.maximum(m, ms)
        scores.append(sc_h)
        row_max.append(m)
    accs = []
    for h in kv_heads:
        acc = None
        for s in range(n_src):
            v = kv_refs[2 * s + 1][0]
            lane = lax.broadcasted_iota(jnp.int32, v.shape, 1)
            in_head = (lane >= h * HEAD_DIM) & (lane < (h + 1) * HEAD_DIM)
            v_aug = jnp.where(in_head, v, jnp.ones_like(v))
            p = jnp.exp(scores[h][s] - row_max[h]).astype(BF16)
            pv = _dot(p, v_aug)
            acc = pv if acc is None else acc + pv
        accs.append(acc)
    for h in kv_heads:
        o = accs[h][:, h * HEAD_DIM:(h + 1) * HEAD_DIM]
        den = accs[h][:, (1 - h) * HEAD_DIM:(2 - h) * HEAD_DIM]
        o = (o / den).astype(BF16)
        for g in range(ATT_GROUP):
            c0 = (h * ATT_GROUP + g) * HEAD_DIM
            o_ref[0, :, c0:c0 + HEAD_DIM] = o[g * tq:(g + 1) * tq]


def _attention(q, kv_sources, tq):
    B, L, _ = q.shape
    n_src = len(kv_sources)
    in_specs = [pl.BlockSpec((1, tq, ATT_CH), lambda b, i: (b, i, 0))]
    args = [q]
    for k, v in kv_sources:
        lk = k.shape[1]
        in_specs += [pl.BlockSpec((1, lk, KV_CH), lambda b, i: (b, 0, 0)),
                     pl.BlockSpec((1, lk, KV_CH), lambda b, i: (b, 0, 0))]
        args += [k, v]
    return pl.pallas_call(
        functools.partial(_attention_kernel, n_src),
        out_shape=jax.ShapeDtypeStruct((B, L, ATT_CH), BF16),
        grid=(B, L // tq),
        in_specs=in_specs,
        out_specs=pl.BlockSpec((1, tq, ATT_CH), lambda b, i: (b, i, 0)),
        compiler_params=pltpu.CompilerParams(
            dimension_semantics=("arbitrary", "arbitrary"), vmem_limit_bytes=VMEM_LIMIT),
        name="attention",
    )(*args)


def _lane_col(x, c):
    lane = lax.broadcasted_iota(jnp.int32, x.shape, 1)
    return jnp.sum(jnp.where(lane == c, x, 0.0), axis=1, keepdims=True)


def _head_bcast(cols, lane_head):
    out = cols[DN_HEADS - 1]
    for h in range(DN_HEADS - 2, -1, -1):
        out = jnp.where(lane_head <= h, cols[h], out)
    return out


def _conv_silu_norm(x, prev_row, next_row, w_ref):
    n = x.shape[0]
    row = lax.broadcasted_iota(jnp.int32, x.shape, 0)
    x_m1 = jnp.where(row == 0, prev_row, pltpu.roll(x, 1, axis=0))
    x_p1 = jnp.where(row == n - 1, next_row, pltpu.roll(x, n - 1, axis=0))
    y = _silu(x_m1 * w_ref[0:1, :] + x * w_ref[1:2, :] + x_p1 * w_ref[2:3, :])
    qk = y[:, :2 * DN_CH]
    qk = qk * lax.rsqrt(_seg64_sum(qk * qk, 2) + EPS)
    return jnp.concatenate([qk[:, :DN_CH] * (HEAD_DIM ** -0.5), qk[:, DN_CH:], y[:, 2 * DN_CH:]],
                           axis=1)


def _gates_to_beta_g(z, alog_ref, dtb_ref):
    lane = lax.broadcasted_iota(jnp.int32, z.shape, 1)
    g = -jnp.exp(alog_ref[...]) * _softplus(z + dtb_ref[...])
    return jnp.where(lane < 2 * DN_HEADS, _sigmoid(z), g)


def _gdn_prep(direction, qkv, bg, cum_ref, ones_ref, pu_ref, pwq_ref, pk_ref, pe_ref):
    cum_m = cum_ref[...]
    ones_m = ones_ref[...]
    ri = lax.broadcasted_iota(jnp.int32, (GDN_TILE, GDN_TILE), 0)
    ci = lax.broadcasted_iota(jnp.int32, (GDN_TILE, GDN_TILE), 1)
    blk = (ri // CHUNK) == (ci // CHUNK)
    lane_head = ci // HEAD_DIM
    if direction == 0:
        tri_strict, tri_incl = blk & (ri > ci), blk & (ri >= ci)
    else:
        tri_strict, tri_incl = blk & (ri < ci), blk & (ri <= ci)
    q = qkv[:, 0:DN_CH]
    k = qkv[:, DN_CH:2 * DN_CH]
    v = qkv[:, 2 * DN_CH:3 * DN_CH]

    gc = _dot_exact_lhs(cum_m, bg)
    gt = _dot_exact_lhs(ones_m, bg)
    gc_t = gc.T
    row_t = lax.broadcasted_iota(jnp.int32, gc_t.shape, 0)

    beta_cols, gc_cols, gt_cols = [], [], []
    for h in range(DN_HEADS):
        beta_cols.append(_lane_col(bg, direction * DN_HEADS + h))
        gc_cols.append(_lane_col(gc, 2 * DN_HEADS + direction * DN_HEADS + h))
        gt_cols.append(_lane_col(gt, 2 * DN_HEADS + direction * DN_HEADS + h))
    beta_b = _head_bcast(beta_cols, lane_head)
    gc_b = _head_bcast(gc_cols, lane_head)
    gt_b = _head_bcast(gt_cols, lane_head)

    e_gc = jnp.exp(gc_b)
    kb = k * beta_b
    vb = v * beta_b
    kbe = kb * e_gc
    qe = q * e_gc
    kd = k * jnp.exp(gt_b - gc_b)
    pe_ref[...] = jnp.exp(gt_b)
    pk_ref[0] = kd.T.astype(BF16)
    pwq_ref[1] = qe.astype(BF16)

    k16 = k.astype(BF16)
    rhs_uw = jnp.concatenate([vb, kbe], axis=1).astype(BF16)
    eye = (ri == ci).astype(F32)
    yield

    heads = range(DN_HEADS)
    neg_a = []
    for h in heads:
        in_h = lane_head == h
        r = jnp.sum(jnp.where(row_t == 2 * DN_HEADS + direction * DN_HEADS + h, gc_t, 0.0),
                    axis=0, keepdims=True)
        diff = gc_cols[h] - r
        dec_i = jnp.exp(jnp.where(tri_incl, diff, NEG_BIG))
        dec_s = jnp.where(tri_strict, dec_i, 0.0)
        kk = _dot_nt(jnp.where(in_h, kb, 0.0).astype(BF16), k16)
        qk = _dot_nt(jnp.where(in_h, q, 0.0).astype(BF16), k16)
        neg_a.append(-(kk * dec_s))
        pk_ref[1 + h] = (qk * dec_i).astype(BF16)
        yield

    t = [eye + jnp.where((ri // 2) == (ci // 2), neg_a[h], 0.0) for h in heads]
    b = 2
    while b < CHUNK:
        lvl = ((ri // (2 * b)) == (ci // (2 * b))) & ((ri // b) != (ci // b))
        t16 = [t[h].astype(BF16) for h in heads]
        g = [_dot(jnp.where(lvl, neg_a[h], 0.0).astype(BF16), t16[h]) for h in heads]
        yield
        t = [t[h] + _dot(t16[h], g[h].astype(BF16)) for h in heads]
        yield
        b *= 2
    uw = [_dot(t[h].astype(BF16), rhs_uw) for h in heads]
    u_all, w_all = uw[DN_HEADS - 1][:, :DN_CH], uw[DN_HEADS - 1][:, DN_CH:]
    for h in range(DN_HEADS - 2, -1, -1):
        u_all = jnp.where(lane_head <= h, uw[h][:, :DN_CH], u_all)
        w_all = jnp.where(lane_head <= h, uw[h][:, DN_CH:], w_all)
    pu_ref[...] = u_all
    pwq_ref[0] = w_all.astype(BF16)


def _gdn_scan(direction, s_ref, pu_ref, pwq_ref, pk_ref, pe_ref, o_ref, o_rows):
    ri = lax.broadcasted_iota(jnp.int32, (GDN_TILE, GDN_TILE), 0)
    ci = lax.broadcasted_iota(jnp.int32, (GDN_TILE, GDN_TILE), 1)
    blk = (ri // HEAD_DIM) == (ci // HEAD_DIM)
    order = range(CHUNKS_PER_TILE) if direction == 0 else range(CHUNKS_PER_TILE - 1, -1, -1)
    outs = [None] * CHUNKS_PER_TILE
    zeros_c = jnp.zeros((CHUNK, DN_CH), BF16)
    lane_head_c = lax.broadcasted_iota(jnp.int32, (CHUNK, DN_CH), 1) // HEAD_DIM
    for c in order:
        rows = pl.ds(c * CHUNK, CHUNK)
        s = s_ref[...]
        wq = jnp.concatenate([pwq_ref[0, rows, :], pwq_ref[1, rows, :]], axis=0)
        ws = _dot(wq, s.astype(BF16))
        yield
        v_new = (pu_ref[rows, :] - ws[:CHUNK]).astype(BF16)
        v_tile = jnp.concatenate([v_new if i == c else zeros_c for i in range(CHUNKS_PER_TILE)],
                                 axis=0)
        lhs = jnp.concatenate([pk_ref[0]] + [pk_ref[1 + h, rows, :] for h in range(DN_HEADS)],
                              axis=0)
        r2 = _dot(lhs, v_tile)
        s_ref[...] = s * pe_ref[pl.ds(c * CHUNK, 1), :] + jnp.where(blk, r2[:GDN_TILE], 0.0)
        o = ws[CHUNK:]
        for h in range(DN_HEADS):
            o = o + jnp.where(lane_head_c == h,
                              r2[GDN_TILE + h * CHUNK:GDN_TILE + (h + 1) * CHUNK], 0.0)
        outs[c] = o
        yield
    o_ref[o_rows, :] = jnp.concatenate(outs, axis=0)


def _trace_interleaved(stages):
    live = [[g, p] for g, p in stages]
    r = 0
    while live:
        for item in list(live):
            if r % item[1] == 0 and next(item[0], StopIteration) is StopIteration:
                live.remove(item)
        r += 1


def _gdn_kernel(n_lat, write_ctx, dc_ref, dl_ref, gc_ref, gl_ref, zc_ref, zl_ref, cw_ref,
                alog_ref, dtb_ref, on_ref, cum_f_ref, cum_b_ref, ones_ref,
                ol_ref, oc_ref, qkv_ref, bg_ref, of_ref, ob_ref, sf_ref, sb_ref,
                pu_ref, pwq_ref, pk_ref, pe_ref):
    T = GDN_TILE
    zero_row = jnp.zeros((1, CONV_CH), F32)

    qkv_ref[0:T] = _conv_silu_norm(dc_ref[0], zero_row, zero_row, cw_ref)
    bg_ref[0:T] = _gates_to_beta_g(gc_ref[0], alog_ref, dtb_ref)

    def prep(t, carry):
        r0 = pl.multiple_of(t * T, T)
        x = dl_ref[0, pl.ds(r0, T), :]
        prev_row = dl_ref[0, pl.ds(jnp.maximum(r0 - 1, 0), 1), :]
        next_row = dl_ref[0, pl.ds(jnp.minimum(r0 + T, n_lat * T - 1), 1), :]
        prev_row = jnp.where(t == 0, zero_row, prev_row)
        next_row = jnp.where(t == n_lat - 1, zero_row, next_row)
        o0 = pl.multiple_of(r0 + T, T)
        qkv_ref[pl.ds(o0, T), :] = _conv_silu_norm(x, prev_row, next_row, cw_ref)
        bg_ref[pl.ds(o0, T), :] = _gates_to_beta_g(gl_ref[0, pl.ds(r0, T), :], alog_ref, dtb_ref)
        return carry

    lax.fori_loop(0, n_lat, prep, 0)

    sf_ref[...] = jnp.zeros_like(sf_ref)
    sb_ref[...] = jnp.zeros_like(sb_ref)

    s_refs = (sf_ref, sb_ref)
    cum_refs = (cum_f_ref, cum_b_ref)
    o_refs = (of_ref, ob_ref)

    def tile_rows(direction, s):
        if isinstance(s, int):
            t = s if (direction == 0 or s == 0) else n_lat + 1 - s
            return pl.ds(t * T, T)
        t = s if direction == 0 else jnp.where(s == 0, 0, n_lat + 1 - s)
        return pl.ds(pl.multiple_of(t * T, T), T)

    def prep_stages(s, slot):
        return [(_gdn_prep(d, qkv_ref[tile_rows(d, s), :], bg_ref[tile_rows(d, s), :], cum_refs[d],
                           ones_ref, pu_ref.at[slot, d], pwq_ref.at[slot, d], pk_ref.at[slot, d],
                           pe_ref.at[slot, d]), 1) for d in range(2)]

    def scan_stages(s, slot):
        return [(_gdn_scan(d, s_refs[d], pu_ref.at[slot, d], pwq_ref.at[slot, d],
                           pk_ref.at[slot, d], pe_ref.at[slot, d], o_refs[d], tile_rows(d, s)),
                 SCAN_STAGE_PERIOD) for d in range(2)]

    _trace_interleaved(prep_stages(0, 0))

    def scan_pair(j, carry):
        s = 2 * j
        _trace_interleaved(scan_stages(s, 0) + prep_stages(s + 1, 1))
        _trace_interleaved(scan_stages(s + 1, 1) + prep_stages(s + 2, 0))
        return carry

    lax.fori_loop(0, n_lat // 2, scan_pair, 0)
    _trace_interleaved(scan_stages(n_lat, 0))

    def finish(o, z):
        ms = _seg64_sum(o * o, 2) * (1.0 / HEAD_DIM)
        return (o * lax.rsqrt(ms + EPS) * on_ref[...] * _silu(z)).astype(BF16)

    if write_ctx:
        oc_ref[0] = finish(of_ref[0:T] + ob_ref[0:T], zc_ref[0])
    else:
        oc_ref[0] = jnp.zeros(oc_ref.shape[1:], BF16)

    def fin(t, carry):
        r0 = pl.multiple_of(t * T, T)
        o0 = pl.multiple_of(r0 + T, T)
        ol_ref[0, pl.ds(r0, T), :] = finish(of_ref[pl.ds(o0, T), :] + ob_ref[pl.ds(o0, T), :],
                                            zl_ref[0, pl.ds(r0, T), :])
        return carry

    lax.fori_loop(0, n_lat, fin, 0)


def _gdn(d_ctx, d_lat, g_ctx, g_lat, z_ctx, z_lat, conv_w, alog_row, dtb_row, on_row, consts,
         write_ctx):
    B, L, _ = d_lat.shape
    Lc = d_ctx.shape[1]
    assert Lc == GDN_TILE and L % (2 * GDN_TILE) == 0
    n_lat = L // GDN_TILE
    T = GDN_TILE
    cum_f, cum_b, ones_m = consts
    per_b = lambda n, c: pl.BlockSpec((1, n, c), lambda b: (b, 0, 0))
    single = lambda n, c: pl.BlockSpec((1, n, c), lambda b: (b, 0, 0), pipeline_mode=pl.Buffered(1))
    full = lambda a: pl.BlockSpec(a.shape, lambda b: (0,) * a.ndim)
    tot = L + Lc
    return pl.pallas_call(
        functools.partial(_gdn_kernel, n_lat, write_ctx),
        out_shape=(jax.ShapeDtypeStruct((B, L, DN_CH), BF16),
                   jax.ShapeDtypeStruct((B, Lc, DN_CH), BF16)),
        grid=(B,),
        in_specs=[per_b(Lc, CONV_CH), per_b(L, CONV_CH), per_b(Lc, LANES), single(L, LANES),
                  per_b(Lc, DN_CH), single(L, DN_CH), full(conv_w), full(alog_row), full(dtb_row),
                  full(on_row), full(cum_f), full(cum_b), full(ones_m)],
        out_specs=(per_b(L, DN_CH), per_b(Lc, DN_CH)),
        scratch_shapes=[pltpu.VMEM((tot, CONV_CH), F32), pltpu.VMEM((tot, LANES), F32),
                        pltpu.VMEM((tot, DN_CH), F32), pltpu.VMEM((tot, DN_CH), F32),
                        pltpu.VMEM((T, DN_CH), F32), pltpu.VMEM((T, DN_CH), F32),
                        pltpu.VMEM((2, 2, T, DN_CH), F32), pltpu.VMEM((2, 2, 2, T, DN_CH), BF16),
                        pltpu.VMEM((2, 2, 1 + DN_HEADS, T, T), BF16), pltpu.VMEM((2, 2, T, DN_CH), F32)],
        compiler_params=pltpu.CompilerParams(
            dimension_semantics=("arbitrary",), vmem_limit_bytes=VMEM_LIMIT),
        name="gdn",
    )(d_ctx, d_lat, g_ctx, g_lat, z_ctx, z_lat, conv_w, alog_row, dtb_row, on_row,
      cum_f, cum_b, ones_m)


def _mix_mlp_kernel(ff_tile, x_ref, fy_ref, at_ref, dn_ref, g1_ref, sc_ref, sh_ref, g2_ref, ng_ref,
                    wof_ref, woa_ref, wod_ref, w1_ref, w2_ref, o_ref):
    mix = (_dot(fy_ref[0], wof_ref[...]) + _dot(at_ref[0], woa_ref[...])
           + _dot(dn_ref[0], wod_ref[...]))
    x1 = x_ref[0] + g1_ref[0] * mix
    ms = jnp.mean(x1 * x1, axis=-1, keepdims=True)
    gain = ng_ref[...] * (1.0 + sc_ref[0])
    h = (x1 * lax.rsqrt(ms + EPS) * gain + sh_ref[0]).astype(BF16)
    acc = jnp.zeros_like(x1)
    for c in range(w1_ref.shape[1] // ff_tile):
        u = jnp.maximum(_dot(h, w1_ref[:, c * ff_tile:(c + 1) * ff_tile]), 0.0)
        acc = acc + _dot((u * u).astype(BF16), w2_ref[c * ff_tile:(c + 1) * ff_tile, :])
    o_ref[0] = x1 + g2_ref[0] * acc


def _mix_mlp(x, fy, att, dn, g1, sc, sh, g2, norm_g, wts, tm):
    B, L, D = x.shape
    per_batch = g1.shape[0] > 1
    mod_map = (lambda b, i: (b, 0, 0)) if per_batch else (lambda b, i: (0, 0, 0))
    mod = pl.BlockSpec((1, 1, D), mod_map)
    full = lambda a: pl.BlockSpec(a.shape, lambda b, i: (0,) * a.ndim,
                                  pipeline_mode=pl.Buffered(1))
    row = lambda n: pl.BlockSpec((1, tm, n), lambda b, i: (b, i, 0))
    wof, woa, wod, w1, w2 = wts
    return pl.pallas_call(
        functools.partial(_mix_mlp_kernel, 1024),
        out_shape=jax.ShapeDtypeStruct((B, L, D), F32),
        grid=(B, L // tm),
        in_specs=[row(D), row(FT_CH), row(ATT_CH), row(DN_CH), mod, mod, mod, mod,
                  pl.BlockSpec(norm_g.shape, lambda b, i: (0, 0)),
                  full(wof), full(woa), full(wod), full(w1), full(w2)],
        out_specs=row(D),
        compiler_params=pltpu.CompilerParams(
            dimension_semantics=("arbitrary", "arbitrary"), vmem_limit_bytes=VMEM_LIMIT),
        name="mix_mlp",
    )(x, fy, att, dn, g1, sc, sh, g2, norm_g, wof, woa, wod, w1, w2)


def _rope_tables(S):
    rows = S // GRID_W
    t_row = jnp.repeat(jnp.arange(rows), GRID_W).astype(F32)
    t_col = jnp.tile(jnp.arange(GRID_W), rows).astype(F32)
    inv_freq = ROPE_THETA ** (-jnp.arange(ROT_PAIRS, dtype=F32) * 2.0 / AXIS_DIM)
    ang_r = t_row[:, None] * inv_freq
    ang_c = t_col[:, None] * inv_freq
    ang = jnp.concatenate([ang_r, ang_r, ang_c, ang_c], axis=-1)
    cos, sin = jnp.cos(ang), jnp.sin(ang)
    sign = jnp.where((jnp.arange(HEAD_DIM) & ROT_PAIRS) == 0, -1.0, 1.0).astype(F32)
    tile2 = lambda a: jnp.concatenate([a, a], axis=-1)
    return tile2(cos), tile2(sin * sign)


def _dft_tables(n):
    k = (jnp.arange(n, dtype=jnp.int32)[:, None] * jnp.arange(n, dtype=jnp.int32)[None, :]) % n
    ang = k.astype(F32) * np.float32(2.0 * np.pi / n)
    return jnp.cos(ang), jnp.sin(ang)


def _channel_dft():
    c, s = _dft_tables(HEAD_DIM)
    eye = jnp.eye(FT_GROUPS, dtype=F32)
    return jnp.concatenate([jnp.kron(eye, c), jnp.kron(eye, s)], axis=1).astype(BF16)


def _gdn_consts():
    i = np.arange(GDN_TILE)
    blk = (i[:, None] // CHUNK) == (i[None, :] // CHUNK)
    cum_f = (blk & (i[:, None] >= i[None, :])).astype(np.float32)
    cum_b = (blk & (i[:, None] <= i[None, :])).astype(np.float32)
    return (jnp.asarray(cum_f, BF16), jnp.asarray(cum_b, BF16), jnp.asarray(blk.astype(np.float32), BF16))


def _pad_lanes(a):
    flat = a.reshape(1, -1).astype(F32)
    return jnp.pad(flat, ((0, 0), (0, LANES - flat.shape[1])))


def kernel(x, c, ctx, c_ctx, norm1_g, norm2_g, w_mod, b_mod, w_in, conv_w, q_norm_g, k_norm_g,
           a_log, dt_bias, o_norm_g, w_out, w_ff1, w_ff2):
    B, S, D = x.shape
    Lc = ctx.shape[1]
    depth = w_mod.shape[0]

    rows = ((B + 1 + 7) // 8) * 8
    c_all = jnp.concatenate([c, c_ctx[None, :], jnp.zeros((rows - B - 1, D), F32)], axis=0)
    mod_all = _modulation(c_all, w_mod, b_mod)

    cos_t, sin_t = _rope_tables(S)
    dft_ch = _channel_dft()
    dft_lat = tuple(t.astype(BF16) for t in _dft_tables(S))
    dft_ctx = tuple(t.astype(BF16) for t in _dft_tables(Lc))
    gdn_consts = _gdn_consts()

    o0 = FT_CH
    o1 = o0 + ATT_CH
    o2 = o1 + 2 * KV_CH
    o3 = o2 + CONV_CH
    o4 = o3 + DN_CH

    x_lat, x_ctx = x, ctx
    for l in range(depth):
        last = l == depth - 1
        w = w_in[l]
        wg = jnp.pad(w[:, o4:], ((0, 0), (0, LANES - N_GATES)))
        in_wts = (w[:, :o0].astype(BF16), dft_ch, w[:, o0:o1].astype(BF16), w[:, o1:o2].astype(BF16),
                  w[:, o2:o3].astype(BF16), w[:, o3:o4].astype(BF16), wg.astype(BF16))
        qg = jnp.tile(q_norm_g[l], ATT_HQ).reshape(1, ATT_CH)
        kg = jnp.tile(k_norm_g[l], ATT_HKV).reshape(1, KV_CH)
        n1 = norm1_g[l].reshape(1, D)
        n2 = norm2_g[l].reshape(1, D)
        wo = w_out[l].astype(BF16)
        out_wts = (wo[:FT_CH], wo[FT_CH:FT_CH + ATT_CH], wo[FT_CH + ATT_CH:],
                   w_ff1[l].astype(BF16), w_ff2[l].astype(BF16))
        alog_row = jnp.pad(a_log[l].reshape(1, -1), ((0, 0), (2 * DN_HEADS, LANES - N_GATES)))
        dtb_row = jnp.pad(dt_bias[l].reshape(1, -1), ((0, 0), (2 * DN_HEADS, LANES - N_GATES)))
        on_row = jnp.tile(o_norm_g[l], DN_HEADS).reshape(1, DN_CH)

        mod = mod_all[l, :B].reshape(B, 1, 6 * D)
        modc = mod_all[l, B:B + 1].reshape(1, 1, 6 * D)
        sh1, sc1, g1, sh2, sc2, g2 = [mod[:, :, i * D:(i + 1) * D] for i in range(6)]
        csh1, csc1, cg1, csh2, csc2, cg2 = [modc[:, :, i * D:(i + 1) * D] for i in range(6)]

        fl, ql, kl, vl, dl, zl, gl = _project(x_lat, sc1, sh1, n1, in_wts, qg, kg, cos_t, sin_t,
                                              True, 512)
        fc, qc, kc, vc, dc, zc, gc = _project(x_ctx, csc1, csh1, n1, in_wts, qg, kg,
                                              cos_t[:Lc], sin_t[:Lc], False, Lc)

        dn_lat, dn_ctx = _gdn(dc, dl, gc, gl, zc, zl, conv_w[l], alog_row, dtb_row, on_row,
                              gdn_consts, not last)
        att_lat = _attention(ql, [(kl, vl), (kc, vc)], 128)
        fy_lat = _fourier(fl, dft_lat[0], dft_lat[1], 512)
        x_lat = _mix_mlp(x_lat, fy_lat, att_lat, dn_lat, g1, sc2, sh2, g2, n2, out_wts, 512)
        if not last:
            att_ctx = _attention(qc, [(kc, vc)], 128)
            fy_ctx = _fourier(fc, dft_ctx[0], dft_ctx[1], Lc)
            x_ctx = _mix_mlp(x_ctx, fy_ctx, att_ctx, dn_ctx, cg1, csc2, csh2, cg2, n2, out_wts, Lc)
    return x_lat
```

```python
import functools

import jax
import jax.numpy as jnp
import numpy as np
from jax import lax
from jax.experimental import pallas as pl
from jax.experimental.pallas import tpu as pltpu

HEAD_DIM = 64
FT_GROUPS = 4
FT_CH = FT_GROUPS * HEAD_DIM
ATT_HQ = 8
ATT_HKV = 2
ATT_GROUP = ATT_HQ // ATT_HKV
ATT_CH = ATT_HQ * HEAD_DIM
KV_CH = ATT_HKV * HEAD_DIM
DN_HEADS = 4
DN_CH = DN_HEADS * HEAD_DIM
CONV_CH = 3 * DN_CH
N_GATES = 4 * DN_HEADS
GRID_W = 64
ROPE_THETA = 10000.0
AXIS_DIM = HEAD_DIM // 2
ROT_PAIRS = AXIS_DIM // 2
EPS = 1e-6
CHUNK = 64

LANES = 128
MXU_DIM = 256
V7X_VMEM_BYTES = 64 * 1024 * 1024
VMEM_LIMIT = V7X_VMEM_BYTES - 8 * 1024 * 1024

GDN_TILE = 256
CHUNKS_PER_TILE = GDN_TILE // CHUNK
NEG_BIG = -1e30
SCAN_STAGE_PERIOD = 1

F32 = jnp.float32
BF16 = jnp.bfloat16


def _dot(a, b):
    return jnp.dot(a, b, preferred_element_type=F32)


def _dot_nt(a, b):
    return lax.dot_general(a, b, (((1,), (1,)), ((), ())), preferred_element_type=F32)


def _split3(x):
    hi = x.astype(BF16)
    r1 = x - hi.astype(F32)
    mid = r1.astype(BF16)
    lo = (r1 - mid.astype(F32)).astype(BF16)
    return hi, mid, lo


def _dot_exact_lhs(m_bf16, x):
    hi, mid, lo = _split3(x)
    return _dot(m_bf16, hi) + _dot(m_bf16, mid) + _dot(m_bf16, lo)


def _seg64_sum(x, passes):
    w = x.shape[1]
    blk = min(w, MXU_DIM)
    r = lax.broadcasted_iota(jnp.int32, (blk, blk), 0) // HEAD_DIM
    c = lax.broadcasted_iota(jnp.int32, (blk, blk), 1) // HEAD_DIM
    ones = jnp.where(r == c, 1.0, 0.0).astype(BF16)
    outs = []
    for j in range(w // blk):
        xs = x[:, j * blk:(j + 1) * blk]
        hi = xs.astype(BF16)
        acc = _dot(hi, ones)
        if passes == 2:
            acc = acc + _dot((xs - hi.astype(F32)).astype(BF16), ones)
        outs.append(acc)
    return outs[0] if len(outs) == 1 else jnp.concatenate(outs, axis=1)


def _sigmoid(x):
    return 1.0 / (1.0 + jnp.exp(-x))


def _silu(x):
    return x * _sigmoid(x)


def _softplus(x):
    return jnp.maximum(x, 0.0) + jnp.log1p(jnp.exp(-jnp.abs(x)))


def _mod_kernel(c_ref, w_ref, b_ref, o_ref):
    a = _silu(c_ref[...])
    o_ref[0] = jnp.dot(a, w_ref[0], preferred_element_type=F32,
                       precision=lax.Precision.HIGHEST) + b_ref[0]


def _modulation(c_all, w_mod, b_mod):
    depth, d, n = w_mod.shape
    rows = c_all.shape[0]
    tn = 1024
    return pl.pallas_call(
        _mod_kernel,
        out_shape=jax.ShapeDtypeStruct((depth, rows, n), F32),
        grid=(depth, n // tn),
        in_specs=[pl.BlockSpec((rows, d), lambda l, j: (0, 0)),
                  pl.BlockSpec((1, d, tn), lambda l, j: (l, 0, j)),
                  pl.BlockSpec((1, 1, tn), lambda l, j: (l, 0, j))],
        out_specs=pl.BlockSpec((1, rows, tn), lambda l, j: (l, 0, j)),
        compiler_params=pltpu.CompilerParams(
            dimension_semantics=("arbitrary", "arbitrary"), vmem_limit_bytes=VMEM_LIMIT),
        name="modulation",
    )(c_all, w_mod, b_mod.reshape(depth, 1, n))


def _head_rms_rope(z, gain, cos, sin_signed, scale):
    ms = _seg64_sum(z * z, 1) * (1.0 / HEAD_DIM)
    y = z * lax.rsqrt(ms + EPS) * gain
    if scale != 1.0:
        y = y * scale
    if cos is None:
        return y
    lane = lax.broadcasted_iota(jnp.int32, cos.shape, 1)
    slabs = []
    for j in range(z.shape[1] // LANES):
        ys = y[:, j * LANES:(j + 1) * LANES]
        partner = jnp.where((lane & ROT_PAIRS) == 0,
                            pltpu.roll(ys, LANES - ROT_PAIRS, axis=1),
                            pltpu.roll(ys, ROT_PAIRS, axis=1))
        slabs.append(ys * cos + partner * sin_signed)
    return slabs[0] if len(slabs) == 1 else jnp.concatenate(slabs, axis=1)


def _project_kernel(use_rope, x_ref, sc_ref, sh_ref, g_ref, wf_ref, dft_ref, wq_ref, wkv_ref,
                    wd_ref, wz_ref, wg_ref, qg_ref, kg_ref, cos_ref, sin_ref,
                    f_ref, q_ref, k_ref, v_ref, d_ref, z_ref, gt_ref):
    x = x_ref[0]
    ms = jnp.mean(x * x, axis=-1, keepdims=True)
    gain = g_ref[...] * (1.0 + sc_ref[0])
    h = (x * lax.rsqrt(ms + EPS) * gain + sh_ref[0]).astype(BF16)

    cos = cos_ref[...] if use_rope else None
    sin = sin_ref[...] if use_rope else None

    zq = _dot(h, wq_ref[...])
    zkv = _dot(h, wkv_ref[...])
    f = _dot(h, wf_ref[...]).astype(BF16)
    d_ref[0] = _dot(h, wd_ref[...])
    q_ref[0] = _head_rms_rope(zq, qg_ref[...], cos, sin, HEAD_DIM ** -0.5).astype(BF16)
    z_ref[0] = _dot(h, wz_ref[...])
    k_ref[0] = _head_rms_rope(zkv[:, :KV_CH], kg_ref[...], cos, sin, 1.0).astype(BF16)
    v_ref[0] = zkv[:, KV_CH:].astype(BF16)
    f_ref[0] = _dot(f, dft_ref[...]).astype(BF16)
    gt_ref[0] = _dot(h, wg_ref[...])


def _project(x, sc, sh, norm_g, wts, qg, kg, cos_t, sin_t, use_rope, tm):
    B, L, D = x.shape
    per_batch = sc.shape[0] > 1
    mod_map = (lambda b, i: (b, 0, 0)) if per_batch else (lambda b, i: (0, 0, 0))
    full = lambda a: pl.BlockSpec(a.shape, lambda b, i: (0,) * a.ndim)
    row = lambda n: pl.BlockSpec((1, tm, n), lambda b, i: (b, i, 0))
    wf, dft, wq, wkv, wd, wz, wg = wts
    out_shape = (jax.ShapeDtypeStruct((B, L, 2 * FT_CH), BF16),
                 jax.ShapeDtypeStruct((B, L, ATT_CH), BF16),
                 jax.ShapeDtypeStruct((B, L, KV_CH), BF16),
                 jax.ShapeDtypeStruct((B, L, KV_CH), BF16),
                 jax.ShapeDtypeStruct((B, L, CONV_CH), F32),
                 jax.ShapeDtypeStruct((B, L, DN_CH), F32),
                 jax.ShapeDtypeStruct((B, L, LANES), F32))
    return pl.pallas_call(
        functools.partial(_project_kernel, use_rope),
        out_shape=out_shape,
        grid=(B, L // tm),
        in_specs=[row(D),
                  pl.BlockSpec((1, 1, D), mod_map), pl.BlockSpec((1, 1, D), mod_map),
                  full(norm_g), full(wf), full(dft), full(wq), full(wkv), full(wd), full(wz),
                  full(wg), full(qg), full(kg),
                  pl.BlockSpec((tm, LANES), lambda b, i: (i, 0)),
                  pl.BlockSpec((tm, LANES), lambda b, i: (i, 0))],
        out_specs=(row(2 * FT_CH), row(ATT_CH), row(KV_CH), row(KV_CH), row(CONV_CH),
                   row(DN_CH), row(LANES)),
        compiler_params=pltpu.CompilerParams(
            dimension_semantics=("arbitrary", "arbitrary"), vmem_limit_bytes=VMEM_LIMIT),
        name="project_rope" if use_rope else "project",
    )(x, sc, sh, norm_g, wf, dft, wq, wkv, wd, wz, wg, qg, kg, cos_t, sin_t)


def _fourier_kernel(scale, dc_ref, ds_ref, f_ref, o_ref):
    fcs = f_ref[0]
    y = _dot(dc_ref[...], fcs[:, :FT_CH]) - _dot(ds_ref[...], fcs[:, FT_CH:])
    o_ref[0] = (y * scale).astype(BF16)


def _fourier(fcs, dft_c, dft_s, tn):
    B, L, _ = fcs.shape
    scale = float(1.0 / np.sqrt(L * HEAD_DIM))
    return pl.pallas_call(
        functools.partial(_fourier_kernel, scale),
        out_shape=jax.ShapeDtypeStruct((B, L, FT_CH), BF16),
        grid=(L // tn, B),
        in_specs=[pl.BlockSpec((tn, L), lambda n, b: (n, 0)),
                  pl.BlockSpec((tn, L), lambda n, b: (n, 0)),
                  pl.BlockSpec((1, L, 2 * FT_CH), lambda n, b: (b, 0, 0))],
        out_specs=pl.BlockSpec((1, tn, FT_CH), lambda n, b: (b, n, 0)),
        compiler_params=pltpu.CompilerParams(
            dimension_semantics=("arbitrary", "arbitrary"), vmem_limit_bytes=VMEM_LIMIT),
        name="fourier",
    )(dft_c, dft_s, fcs)


def _attention_kernel(n_src, q_ref, *refs):
    kv_refs = refs[:2 * n_src]
    o_ref = refs[2 * n_src]
    tq = q_ref.shape[1]
    kv_heads = range(ATT_HKV)
    scores, row_max = [], []
    for h in kv_heads:
        q = q_ref[0, :, h * ATT_GROUP * HEAD_DIM:(h + 1) * ATT_GROUP * HEAD_DIM]
        q4 = jnp.concatenate([q[:, g * HEAD_DIM:(g + 1) * HEAD_DIM] for g in range(ATT_GROUP)],
                             axis=0)
        sc_h, m = [], None
        for s in range(n_src):
            k = kv_refs[2 * s][0, :, h * HEAD_DIM:(h + 1) * HEAD_DIM]
            sc = _dot_nt(q4, k)
            sc_h.append(sc)
            ms = jnp.max(sc, axis=-1, keepdims=True)
            m = ms if m is None else jnp.maximum(m, ms)
        scores.append(sc_h)
        row_max.append(m)
    accs = []
    for h in kv_heads:
        acc = None
        for s in range(n_src):
            v = kv_refs[2 * s + 1][0]
            lane = lax.broadcasted_iota(jnp.int32, v.shape, 1)
            in_head = (lane >= h * HEAD_DIM) & (lane < (h + 1) * HEAD_DIM)
            v_aug = jnp.where(in_head, v, jnp.ones_like(v))
            p = jnp.exp(scores[h][s] - row_max[h]).astype(BF16)
            pv = _dot(p, v_aug)
            acc = pv if acc is None else acc + pv
        accs.append(acc)
    for h in kv_heads:
        o = accs[h][:, h * HEAD_DIM:(h + 1) * HEAD_DIM]
        den = accs[h][:, (1 - h) * HEAD_DIM:(2 - h) * HEAD_DIM]
        o = (o / den).astype(BF16)
        for g in range(ATT_GROUP):
            c0 = (h * ATT_GROUP + g) * HEAD_DIM
            o_ref[0, :, c0:c0 + HEAD_DIM] = o[g * tq:(g + 1) * tq]


def _attention(q, kv_sources, tq):
    B, L, _ = q.shape
    n_src = len(kv_sources)
    in_specs = [pl.BlockSpec((1, tq, ATT_CH), lambda b, i: (b, i, 0))]
    args = [q]
    for k, v in kv_sources:
        lk = k.shape[1]
        in_specs += [pl.BlockSpec((1, lk, KV_CH), lambda b, i: (b, 0, 0)),
                     pl.BlockSpec((1, lk, KV_CH), lambda b, i: (b, 0, 0))]
        args += [k, v]
    return pl.pallas_call(
        functools.partial(_attention_kernel, n_src),
        out_shape=jax.ShapeDtypeStruct((B, L, ATT_CH), BF16),
        grid=(B, L // tq),
        in_specs=in_specs,
        out_specs=pl.BlockSpec((1, tq, ATT_CH), lambda b, i: (b, i, 0)),
        compiler_params=pltpu.CompilerParams(
            dimension_semantics=("arbitrary", "arbitrary"), vmem_limit_bytes=VMEM_LIMIT),
        name="attention",
    )(*args)


def _lane_col(x, c):
    lane = lax.broadcasted_iota(jnp.int32, x.shape, 1)
    return jnp.sum(jnp.where(lane == c, x, 0.0), axis=1, keepdims=True)


def _head_bcast(cols, lane_head):
    out = cols[DN_HEADS - 1]
    for h in range(DN_HEADS - 2, -1, -1):
        out = jnp.where(lane_head <= h, cols[h], out)
    return out


def _conv_silu_norm(x, prev_row, next_row, w_ref):
    n = x.shape[0]
    row = lax.broadcasted_iota(jnp.int32, x.shape, 0)
    x_m1 = jnp.where(row == 0, prev_row, pltpu.roll(x, 1, axis=0))
    x_p1 = jnp.where(row == n - 1, next_row, pltpu.roll(x, n - 1, axis=0))
    y = _silu(x_m1 * w_ref[0:1, :] + x * w_ref[1:2, :] + x_p1 * w_ref[2:3, :])
    qk = y[:, :2 * DN_CH]
    qk = qk * lax.rsqrt(_seg64_sum(qk * qk, 2) + EPS)
    return jnp.concatenate([qk[:, :DN_CH] * (HEAD_DIM ** -0.5), qk[:, DN_CH:], y[:, 2 * DN_CH:]],
                           axis=1)


def _gates_to_beta_g(z, alog_ref, dtb_ref):
    lane = lax.broadcasted_iota(jnp.int32, z.shape, 1)
    g = -jnp.exp(alog_ref[...]) * _softplus(z + dtb_ref[...])
    return jnp.where(lane < 2 * DN_HEADS, _sigmoid(z), g)


def _gdn_prep(direction, qkv, bg, cum_ref, ones_ref, pu_ref, pwq_ref, pk_ref, pe_ref):
    cum_m = cum_ref[...]
    ones_m = ones_ref[...]
    ri = lax.broadcasted_iota(jnp.int32, (GDN_TILE, GDN_TILE), 0)
    ci = lax.broadcasted_iota(jnp.int32, (GDN_TILE, GDN_TILE), 1)
    blk = (ri // CHUNK) == (ci // CHUNK)
    lane_head = ci // HEAD_DIM
    if direction == 0:
        tri_strict, tri_incl = blk & (ri > ci), blk & (ri >= ci)
    else:
        tri_strict, tri_incl = blk & (ri < ci), blk & (ri <= ci)
    q = qkv[:, 0:DN_CH]
    k = qkv[:, DN_CH:2 * DN_CH]
    v = qkv[:, 2 * DN_CH:3 * DN_CH]

    gc = _dot_exact_lhs(cum_m, bg)
    gt = _dot_exact_lhs(ones_m, bg)
    gc_t = gc.T
    row_t = lax.broadcasted_iota(jnp.int32, gc_t.shape, 0)

    beta_cols, gc_cols, gt_cols = [], [], []
    for h in range(DN_HEADS):
        beta_cols.append(_lane_col(bg, direction * DN_HEADS + h))
        gc_cols.append(_lane_col(gc, 2 * DN_HEADS + direction * DN_HEADS + h))
        gt_cols.append(_lane_col(gt, 2 * DN_HEADS + direction * DN_HEADS + h))
    beta_b = _head_bcast(beta_cols, lane_head)
    gc_b = _head_bcast(gc_cols, lane_head)
    gt_b = _head_bcast(gt_cols, lane_head)

    e_gc = jnp.exp(gc_b)
    kb = k * beta_b
    vb = v * beta_b
    kbe = kb * e_gc
    qe = q * e_gc
    kd = k * jnp.exp(gt_b - gc_b)
    pe_ref[...] = jnp.exp(gt_b)
    pk_ref[0] = kd.T.astype(BF16)
    pwq_ref[1] = qe.astype(BF16)

    k16 = k.astype(BF16)
    rhs_uw = jnp.concatenate([vb, kbe], axis=1).astype(BF16)
    eye = (ri == ci).astype(F32)
    yield

    heads = range(DN_HEADS)
    neg_a = []
    for h in heads:
        in_h = lane_head == h
        r = jnp.sum(jnp.where(row_t == 2 * DN_HEADS + direction * DN_HEADS + h, gc_t, 0.0),
                    axis=0, keepdims=True)
        diff = gc_cols[h] - r
        dec_i = jnp.exp(jnp.where(tri_incl, diff, NEG_BIG))
        dec_s = jnp.where(tri_strict, dec_i, 0.0)
        kk = _dot_nt(jnp.where(in_h, kb, 0.0).astype(BF16), k16)
        qk = _dot_nt(jnp.where(in_h, q, 0.0).astype(BF16), k16)
        neg_a.append(-(kk * dec_s))
        pk_ref[1 + h] = (qk * dec_i).astype(BF16)
        yield

    t = [eye + jnp.where((ri // 2) == (ci // 2), neg_a[h], 0.0) for h in heads]
    b = 2
    while b < CHUNK:
        lvl = ((ri // (2 * b)) == (ci // (2 * b))) & ((ri // b) != (ci // b))
        t16 = [t[h].astype(BF16) for h in heads]
        g = [_dot(jnp.where(lvl, neg_a[h], 0.0).astype(BF16), t16[h]) for h in heads]
        yield
        t = [t[h] + _dot(t16[h], g[h].astype(BF16)) for h in heads]
        yield
        b *= 2
    uw = [_dot(t[h].astype(BF16), rhs_uw) for h in heads]
    u_all, w_all = uw[DN_HEADS - 1][:, :DN_CH], uw[DN_HEADS - 1][:, DN_CH:]
    for h in range(DN_HEADS - 2, -1, -1):
        u_all = jnp.where(lane_head <= h, uw[h][:, :DN_CH], u_all)
        w_all = jnp.where(lane_head <= h, uw[h][:, DN_CH:], w_all)
    pu_ref[...] = u_all
    pwq_ref[0] = w_all.astype(BF16)


def _gdn_scan(direction, s_ref, pu_ref, pwq_ref, pk_ref, pe_ref, o_ref, o_rows):
    ri = lax.broadcasted_iota(jnp.int32, (GDN_TILE, GDN_TILE), 0)
    ci = lax.broadcasted_iota(jnp.int32, (GDN_TILE, GDN_TILE), 1)
    blk = (ri // HEAD_DIM) == (ci // HEAD_DIM)
    order = range(CHUNKS_PER_TILE) if direction == 0 else range(CHUNKS_PER_TILE - 1, -1, -1)
    outs = [None] * CHUNKS_PER_TILE
    zeros_c = jnp.zeros((CHUNK, DN_CH), BF16)
    lane_head_c = lax.broadcasted_iota(jnp.int32, (CHUNK, DN_CH), 1) // HEAD_DIM
    for c in order:
        rows = pl.ds(c * CHUNK, CHUNK)
        s = s_ref[...]
        wq = jnp.concatenate([pwq_ref[0, rows, :], pwq_ref[1, rows, :]], axis=0)
        ws = _dot(wq, s.astype(BF16))
        yield
        v_new = (pu_ref[rows, :] - ws[:CHUNK]).astype(BF16)
        v_tile = jnp.concatenate([v_new if i == c else zeros_c for i in range(CHUNKS_PER_TILE)],
                                 axis=0)
        lhs = jnp.concatenate([pk_ref[0]] + [pk_ref[1 + h, rows, :] for h in range(DN_HEADS)],
                              axis=0)
        r2 = _dot(lhs, v_tile)
        s_ref[...] = s * pe_ref[pl.ds(c * CHUNK, 1), :] + jnp.where(blk, r2[:GDN_TILE], 0.0)
        o = ws[CHUNK:]
        for h in range(DN_HEADS):
            o = o + jnp.where(lane_head_c == h,
                              r2[GDN_TILE + h * CHUNK:GDN_TILE + (h + 1) * CHUNK], 0.0)
        outs[c] = o
        yield
    o_ref[o_rows, :] = jnp.concatenate(outs, axis=0)


def _trace_interleaved(stages):
    live = [[g, p] for g, p in stages]
    r = 0
    while live:
        for item in list(live):
            if r % item[1] == 0 and next(item[0], StopIteration) is StopIteration:
                live.remove(item)
        r += 1


def _gdn_kernel(n_lat, write_ctx, dc_ref, dl_ref, gc_ref, gl_ref, zc_ref, zl_ref, cw_ref,
                alog_ref, dtb_ref, on_ref, cum_f_ref, cum_b_ref, ones_ref,
                ol_ref, oc_ref, qkv_ref, bg_ref, of_ref, ob_ref, sf_ref, sb_ref,
                pu_ref, pwq_ref, pk_ref, pe_ref):
    T = GDN_TILE
    zero_row = jnp.zeros((1, CONV_CH), F32)

    qkv_ref[0:T] = _conv_silu_norm(dc_ref[0], zero_row, zero_row, cw_ref)
    bg_ref[0:T] = _gates_to_beta_g(gc_ref[0], alog_ref, dtb_ref)

    def conv_stages():
        for t in range(n_lat):
            r0 = t * T
            prev_row = zero_row if t == 0 else dl_ref[0, pl.ds(r0 - 1, 1), :]
            next_row = zero_row if t == n_lat - 1 else dl_ref[0, pl.ds(r0 + T, 1), :]
            qkv_ref[pl.ds(r0 + T, T), :] = _conv_silu_norm(dl_ref[0, pl.ds(r0, T), :], prev_row,
                                                           next_row, cw_ref)
            bg_ref[pl.ds(r0 + T, T), :] = _gates_to_beta_g(gl_ref[0, pl.ds(r0, T), :], alog_ref,
                                                           dtb_ref)
            yield

    sf_ref[...] = jnp.zeros_like(sf_ref)
    sb_ref[...] = jnp.zeros_like(sb_ref)

    s_refs = (sf_ref, sb_ref)
    cum_refs = (cum_f_ref, cum_b_ref)
    o_refs = (of_ref, ob_ref)

    def tile_rows(direction, s):
        if isinstance(s, int):
            t = s if (direction == 0 or s == 0) else n_lat + 1 - s
            return pl.ds(t * T, T)
        t = s if direction == 0 else jnp.where(s == 0, 0, n_lat + 1 - s)
        return pl.ds(pl.multiple_of(t * T, T), T)

    def prep_stages(s, slot):
        return [(_gdn_prep(d, qkv_ref[tile_rows(d, s), :], bg_ref[tile_rows(d, s), :], cum_refs[d],
                           ones_ref, pu_ref.at[slot, d], pwq_ref.at[slot, d], pk_ref.at[slot, d],
                           pe_ref.at[slot, d]), 1) for d in range(2)]

    def scan_stages(s, slot):
        return [(_gdn_scan(d, s_refs[d], pu_ref.at[slot, d], pwq_ref.at[slot, d],
                           pk_ref.at[slot, d], pe_ref.at[slot, d], o_refs[d], tile_rows(d, s)),
                 SCAN_STAGE_PERIOD) for d in range(2)]

    _trace_interleaved(prep_stages(0, 0) + [(conv_stages(), 2)])

    def scan_pair(j, carry):
        s = 2 * j
        _trace_interleaved(scan_stages(s, 0) + prep_stages(s + 1, 1))
        _trace_interleaved(scan_stages(s + 1, 1) + prep_stages(s + 2, 0))
        return carry

    lax.fori_loop(0, n_lat // 2, scan_pair, 0)

    def finish(o, z):
        ms = _seg64_sum(o * o, 2) * (1.0 / HEAD_DIM)
        return (o * lax.rsqrt(ms + EPS) * on_ref[...] * _silu(z)).astype(BF16)

    def finish_stages(tiles):
        for t in tiles:
            if t == 0:
                if write_ctx:
                    oc_ref[0] = finish(of_ref[0:T] + ob_ref[0:T], zc_ref[0])
                else:
                    oc_ref[0] = jnp.zeros(oc_ref.shape[1:], BF16)
            else:
                rows = pl.ds(t * T, T)
                ol_ref[0, pl.ds((t - 1) * T, T), :] = finish(of_ref[rows, :] + ob_ref[rows, :],
                                                             zl_ref[0, pl.ds((t - 1) * T, T), :])
            yield

    last_tiles = sorted({1, n_lat})
    _trace_interleaved(scan_stages(n_lat, 0)
                       + [(finish_stages([t for t in range(n_lat + 1) if t not in last_tiles]), 1)])
    _trace_interleaved([(finish_stages(last_tiles), 1)])


def _gdn(d_ctx, d_lat, g_ctx, g_lat, z_ctx, z_lat, conv_w, alog_row, dtb_row, on_row, consts,
         write_ctx):
    B, L, _ = d_lat.shape
    Lc = d_ctx.shape[1]
    assert Lc == GDN_TILE and L % (2 * GDN_TILE) == 0
    n_lat = L // GDN_TILE
    T = GDN_TILE
    cum_f, cum_b, ones_m = consts
    per_b = lambda n, c: pl.BlockSpec((1, n, c), lambda b: (b, 0, 0))
    single = lambda n, c: pl.BlockSpec((1, n, c), lambda b: (b, 0, 0), pipeline_mode=pl.Buffered(1))
    full = lambda a: pl.BlockSpec(a.shape, lambda b: (0,) * a.ndim)
    tot = L + Lc
    return pl.pallas_call(
        functools.partial(_gdn_kernel, n_lat, write_ctx),
        out_shape=(jax.ShapeDtypeStruct((B, L, DN_CH), BF16),
                   jax.ShapeDtypeStruct((B, Lc, DN_CH), BF16)),
        grid=(B,),
        in_specs=[per_b(Lc, CONV_CH), per_b(L, CONV_CH), per_b(Lc, LANES), single(L, LANES),
                  per_b(Lc, DN_CH), single(L, DN_CH), full(conv_w), full(alog_row), full(dtb_row),
                  full(on_row), full(cum_f), full(cum_b), full(ones_m)],
        out_specs=(per_b(L, DN_CH), per_b(Lc, DN_CH)),
        scratch_shapes=[pltpu.VMEM((tot, CONV_CH), F32), pltpu.VMEM((tot, LANES), F32),
                        pltpu.VMEM((tot, DN_CH), F32), pltpu.VMEM((tot, DN_CH), F32),
                        pltpu.VMEM((T, DN_CH), F32), pltpu.VMEM((T, DN_CH), F32),
                        pltpu.VMEM((2, 2, T, DN_CH), F32), pltpu.VMEM((2, 2, 2, T, DN_CH), BF16),
                        pltpu.VMEM((2, 2, 1 + DN_HEADS, T, T), BF16), pltpu.VMEM((2, 2, T, DN_CH), F32)],
        compiler_params=pltpu.CompilerParams(
            dimension_semantics=("arbitrary",), vmem_limit_bytes=VMEM_LIMIT),
        name="gdn",
    )(d_ctx, d_lat, g_ctx, g_lat, z_ctx, z_lat, conv_w, alog_row, dtb_row, on_row,
      cum_f, cum_b, ones_m)


def _mix_mlp_kernel(ff_tile, x_ref, fy_ref, at_ref, dn_ref, g1_ref, sc_ref, sh_ref, g2_ref, ng_ref,
                    wof_ref, woa_ref, wod_ref, w1_ref, w2_ref, o_ref):
    mix = (_dot(fy_ref[0], wof_ref[...]) + _dot(at_ref[0], woa_ref[...])
           + _dot(dn_ref[0], wod_ref[...]))
    x1 = x_ref[0] + g1_ref[0] * mix
    ms = jnp.mean(x1 * x1, axis=-1, keepdims=True)
    gain = ng_ref[...] * (1.0 + sc_ref[0])
    h = (x1 * lax.rsqrt(ms + EPS) * gain + sh_ref[0]).astype(BF16)
    acc = jnp.zeros_like(x1)
    for c in range(w1_ref.shape[1] // ff_tile):
        u = jnp.maximum(_dot(h, w1_ref[:, c * ff_tile:(c + 1) * ff_tile]), 0.0)
        acc = acc + _dot((u * u).astype(BF16), w2_ref[c * ff_tile:(c + 1) * ff_tile, :])
    o_ref[0] = x1 + g2_ref[0] * acc


def _mix_mlp(x, fy, att, dn, g1, sc, sh, g2, norm_g, wts, tm):
    B, L, D = x.shape
    per_batch = g1.shape[0] > 1
    mod_map = (lambda b, i: (b, 0, 0)) if per_batch else (lambda b, i: (0, 0, 0))
    mod = pl.BlockSpec((1, 1, D), mod_map)
    full = lambda a: pl.BlockSpec(a.shape, lambda b, i: (0,) * a.ndim,
                                  pipeline_mode=pl.Buffered(1))
    row = lambda n: pl.BlockSpec((1, tm, n), lambda b, i: (b, i, 0))
    wof, woa, wod, w1, w2 = wts
    return pl.pallas_call(
        functools.partial(_mix_mlp_kernel, 1024),
        out_shape=jax.ShapeDtypeStruct((B, L, D), F32),
        grid=(B, L // tm),
        in_specs=[row(D), row(FT_CH), row(ATT_CH), row(DN_CH), mod, mod, mod, mod,
                  pl.BlockSpec(norm_g.shape, lambda b, i: (0, 0)),
                  full(wof), full(woa), full(wod), full(w1), full(w2)],
        out_specs=row(D),
        compiler_params=pltpu.CompilerParams(
            dimension_semantics=("arbitrary", "arbitrary"), vmem_limit_bytes=VMEM_LIMIT),
        name="mix_mlp",
    )(x, fy, att, dn, g1, sc, sh, g2, norm_g, wof, woa, wod, w1, w2)


def _rope_tables(S):
    rows = S // GRID_W
    t_row = jnp.repeat(jnp.arange(rows), GRID_W).astype(F32)
    t_col = jnp.tile(jnp.arange(GRID_W), rows).astype(F32)
    inv_freq = ROPE_THETA ** (-jnp.arange(ROT_PAIRS, dtype=F32) * 2.0 / AXIS_DIM)
    ang_r = t_row[:, None] * inv_freq
    ang_c = t_col[:, None] * inv_freq
    ang = jnp.concatenate([ang_r, ang_r, ang_c, ang_c], axis=-1)
    cos, sin = jnp.cos(ang), jnp.sin(ang)
    sign = jnp.where((jnp.arange(HEAD_DIM) & ROT_PAIRS) == 0, -1.0, 1.0).astype(F32)
    tile2 = lambda a: jnp.concatenate([a, a], axis=-1)
    return tile2(cos), tile2(sin * sign)


def _cos_sin(rows, cols, period):
    k = (jnp.arange(rows, dtype=jnp.int32)[:, None] * jnp.arange(cols, dtype=jnp.int32)[None, :]) % period
    ang = k.astype(F32) * np.float32(2.0 * np.pi / period)
    return jnp.cos(ang), jnp.sin(ang)


def _dft_tables(n):
    if n <= HEAD_DIM:
        return _cos_sin(n, n, n)
    assert n % HEAD_DIM == 0
    ca, sa = _cos_sin(n, n // HEAD_DIM, n // HEAD_DIM)
    cb, sb = _cos_sin(n, HEAD_DIM, n)
    cos = ca[:, :, None] * cb[:, None, :] - sa[:, :, None] * sb[:, None, :]
    sin = sa[:, :, None] * cb[:, None, :] + ca[:, :, None] * sb[:, None, :]
    return cos.reshape(n, n), sin.reshape(n, n)


def _channel_dft():
    c, s = _dft_tables(HEAD_DIM)
    eye = jnp.eye(FT_GROUPS, dtype=F32)
    return jnp.concatenate([jnp.kron(eye, c), jnp.kron(eye, s)], axis=1).astype(BF16)


def _gdn_consts():
    i = np.arange(GDN_TILE)
    blk = (i[:, None] // CHUNK) == (i[None, :] // CHUNK)
    cum_f = (blk & (i[:, None] >= i[None, :])).astype(np.float32)
    cum_b = (blk & (i[:, None] <= i[None, :])).astype(np.float32)
    return (jnp.asarray(cum_f, BF16), jnp.asarray(cum_b, BF16), jnp.asarray(blk.astype(np.float32), BF16))


def _pad_lanes(a):
    flat = a.reshape(1, -1).astype(F32)
    return jnp.pad(flat, ((0, 0), (0, LANES - flat.shape[1])))


def kernel(x, c, ctx, c_ctx, norm1_g, norm2_g, w_mod, b_mod, w_in, conv_w, q_norm_g, k_norm_g,
           a_log, dt_bias, o_norm_g, w_out, w_ff1, w_ff2):
    B, S, D = x.shape
    Lc = ctx.shape[1]
    depth = w_mod.shape[0]

    rows = ((B + 1 + 7) // 8) * 8
    c_all = jnp.concatenate([c, c_ctx[None, :], jnp.zeros((rows - B - 1, D), F32)], axis=0)
    mod_all = _modulation(c_all, w_mod, b_mod)

    cos_t, sin_t = _rope_tables(S)
    dft_ch = _channel_dft()
    dft_lat = tuple(t.astype(BF16) for t in _dft_tables(S))
    dft_ctx = tuple(t.astype(BF16) for t in _dft_tables(Lc))
    gdn_consts = _gdn_consts()

    o0 = FT_CH
    o1 = o0 + ATT_CH
    o2 = o1 + 2 * KV_CH
    o3 = o2 + CONV_CH
    o4 = o3 + DN_CH

    x_lat, x_ctx = x, ctx
    for l in range(depth):
        last = l == depth - 1
        w = w_in[l]
        wg = jnp.pad(w[:, o4:], ((0, 0), (0, LANES - N_GATES)))
        in_wts = (w[:, :o0].astype(BF16), dft_ch, w[:, o0:o1].astype(BF16), w[:, o1:o2].astype(BF16),
                  w[:, o2:o3].astype(BF16), w[:, o3:o4].astype(BF16), wg.astype(BF16))
        qg = jnp.tile(q_norm_g[l], ATT_HQ).reshape(1, ATT_CH)
        kg = jnp.tile(k_norm_g[l], ATT_HKV).reshape(1, KV_CH)
        n1 = norm1_g[l].reshape(1, D)
        n2 = norm2_g[l].reshape(1, D)
        wo = w_out[l].astype(BF16)
        out_wts = (wo[:FT_CH], wo[FT_CH:FT_CH + ATT_CH], wo[FT_CH + ATT_CH:],
                   w_ff1[l].astype(BF16), w_ff2[l].astype(BF16))
        alog_row = jnp.pad(a_log[l].reshape(1, -1), ((0, 0), (2 * DN_HEADS, LANES - N_GATES)))
        dtb_row = jnp.pad(dt_bias[l].reshape(1, -1), ((0, 0), (2 * DN_HEADS, LANES - N_GATES)))
        on_row = jnp.tile(o_norm_g[l], DN_HEADS).reshape(1, DN_CH)

        mod = mod_all[l, :B].reshape(B, 1, 6 * D)
        modc = mod_all[l, B:B + 1].reshape(1, 1, 6 * D)
        sh1, sc1, g1, sh2, sc2, g2 = [mod[:, :, i * D:(i + 1) * D] for i in range(6)]
        csh1, csc1, cg1, csh2, csc2, cg2 = [modc[:, :, i * D:(i + 1) * D] for i in range(6)]

        fl, ql, kl, vl, dl, zl, gl = _project(x_lat, sc1, sh1, n1, in_wts, qg, kg, cos_t, sin_t,
                                              True, 512)
        fc, qc, kc, vc, dc, zc, gc = _project(x_ctx, csc1, csh1, n1, in_wts, qg, kg,
                                              cos_t[:Lc], sin_t[:Lc], False, Lc)

        dn_lat, dn_ctx = _gdn(dc, dl, gc, gl, zc, zl, conv_w[l], alog_row, dtb_row, on_row,
                              gdn_consts, not last)
        att_lat = _attention(ql, [(kl, vl), (kc, vc)], 256)
        fy_lat = _fourier(fl, dft_lat[0], dft_lat[1], 512)
        x_lat = _mix_mlp(x_lat, fy_lat, att_lat, dn_lat, g1, sc2, sh2, g2, n2, out_wts, 512)
        if not last:
            att_ctx = _attention(qc, [(kc, vc)], 128)
            fy_ctx = _fourier(fc, dft_ctx[0], dft_ctx[1], Lc)
            x_ctx = _mix_mlp(x_ctx, fy_ctx, att_ctx, dn_ctx, cg1, csc2, csh2, cg2, n2, out_wts, Lc)
    return x_lat
```

```python
import functools

import jax
import jax.numpy as jnp
import numpy as np
from jax import lax
from jax.experimental import pallas as pl
from jax.experimental.pallas import tpu as pltpu

HEAD_DIM = 64
FT_GROUPS = 4
FT_CH = FT_GROUPS * HEAD_DIM
ATT_HQ = 8
ATT_HKV = 2
ATT_GROUP = ATT_HQ // ATT_HKV
ATT_CH = ATT_HQ * HEAD_DIM
KV_CH = ATT_HKV * HEAD_DIM
DN_HEADS = 4
DN_CH = DN_HEADS * HEAD_DIM
CONV_CH = 3 * DN_CH
N_GATES = 4 * DN_HEADS
GRID_W = 64
ROPE_THETA = 10000.0
AXIS_DIM = HEAD_DIM // 2
ROT_PAIRS = AXIS_DIM // 2
EPS = 1e-6
CHUNK = 64

LANES = 128
MXU_DIM = 256
V7X_VMEM_BYTES = 64 * 1024 * 1024
VMEM_LIMIT = V7X_VMEM_BYTES - 8 * 1024 * 1024

GDN_TILE = 256
CHUNKS_PER_TILE = GDN_TILE // CHUNK
NEG_BIG = -1e30
SCAN_STAGE_PERIOD = 1
CONV_STAGE_PERIOD = 8

F32 = jnp.float32
BF16 = jnp.bfloat16


def _dot(a, b):
    return jnp.dot(a, b, preferred_element_type=F32)


def _dot_nt(a, b):
    return lax.dot_general(a, b, (((1,), (1,)), ((), ())), preferred_element_type=F32)


def _split3(x):
    hi = x.astype(BF16)
    r1 = x - hi.astype(F32)
    mid = r1.astype(BF16)
    lo = (r1 - mid.astype(F32)).astype(BF16)
    return hi, mid, lo


def _dot_exact_lhs(m_bf16, x):
    hi, mid, lo = _split3(x)
    return _dot(m_bf16, hi) + _dot(m_bf16, mid) + _dot(m_bf16, lo)


def _seg64_sum(x, passes):
    w = x.shape[1]
    blk = min(w, MXU_DIM)
    r = lax.broadcasted_iota(jnp.int32, (blk, blk), 0) // HEAD_DIM
    c = lax.broadcasted_iota(jnp.int32, (blk, blk), 1) // HEAD_DIM
    ones = jnp.where(r == c, 1.0, 0.0).astype(BF16)
    outs = []
    for j in range(w // blk):
        xs = x[:, j * blk:(j + 1) * blk]
        hi = xs.astype(BF16)
        acc = _dot(hi, ones)
        if passes == 2:
            acc = acc + _dot((xs - hi.astype(F32)).astype(BF16), ones)
        outs.append(acc)
    return outs[0] if len(outs) == 1 else jnp.concatenate(outs, axis=1)


def _sigmoid(x):
    return 1.0 / (1.0 + jnp.exp(-x))


def _silu(x):
    return x * _sigmoid(x)


def _softplus(x):
    return jnp.maximum(x, 0.0) + jnp.log1p(jnp.exp(-jnp.abs(x)))


def _mod_kernel(c_ref, w_ref, b_ref, o_ref):
    a = _silu(c_ref[...])
    o_ref[0] = jnp.dot(a, w_ref[0], preferred_element_type=F32,
                       precision=lax.Precision.HIGHEST) + b_ref[0]


def _modulation(c_all, w_mod, b_mod):
    depth, d, n = w_mod.shape
    rows = c_all.shape[0]
    tn = 1024
    return pl.pallas_call(
        _mod_kernel,
        out_shape=jax.ShapeDtypeStruct((depth, rows, n), F32),
        grid=(depth, n // tn),
        in_specs=[pl.BlockSpec((rows, d), lambda l, j: (0, 0)),
                  pl.BlockSpec((1, d, tn), lambda l, j: (l, 0, j)),
                  pl.BlockSpec((1, 1, tn), lambda l, j: (l, 0, j))],
        out_specs=pl.BlockSpec((1, rows, tn), lambda l, j: (l, 0, j)),
        compiler_params=pltpu.CompilerParams(
            dimension_semantics=("arbitrary", "arbitrary"), vmem_limit_bytes=VMEM_LIMIT),
        name="modulation",
    )(c_all, w_mod, b_mod.reshape(depth, 1, n))


def _head_rms_rope(z, gain, cos, sin_signed, scale):
    ms = _seg64_sum(z * z, 1) * (1.0 / HEAD_DIM)
    y = z * lax.rsqrt(ms + EPS) * gain
    if scale != 1.0:
        y = y * scale
    if cos is None:
        return y
    lane = lax.broadcasted_iota(jnp.int32, cos.shape, 1)
    slabs = []
    for j in range(z.shape[1] // LANES):
        ys = y[:, j * LANES:(j + 1) * LANES]
        partner = jnp.where((lane & ROT_PAIRS) == 0,
                            pltpu.roll(ys, LANES - ROT_PAIRS, axis=1),
                            pltpu.roll(ys, ROT_PAIRS, axis=1))
        slabs.append(ys * cos + partner * sin_signed)
    return slabs[0] if len(slabs) == 1 else jnp.concatenate(slabs, axis=1)


def _project_kernel(use_rope, x_ref, sc_ref, sh_ref, g_ref, wf_ref, dft_ref, wq_ref, wkv_ref,
                    wd_ref, wz_ref, wg_ref, qg_ref, kg_ref, cos_ref, sin_ref,
                    f_ref, q_ref, k_ref, v_ref, d_ref, z_ref, gt_ref):
    x = x_ref[0]
    ms = jnp.mean(x * x, axis=-1, keepdims=True)
    gain = g_ref[...] * (1.0 + sc_ref[0])
    h = (x * lax.rsqrt(ms + EPS) * gain + sh_ref[0]).astype(BF16)

    cos = cos_ref[...] if use_rope else None
    sin = sin_ref[...] if use_rope else None

    zq = _dot(h, wq_ref[...])
    zkv = _dot(h, wkv_ref[...])
    f = _dot(h, wf_ref[...]).astype(BF16)
    d_ref[0] = _dot(h, wd_ref[...])
    q_ref[0] = _head_rms_rope(zq, qg_ref[...], cos, sin, HEAD_DIM ** -0.5).astype(BF16)
    z_ref[0] = _dot(h, wz_ref[...])
    k_ref[0] = _head_rms_rope(zkv[:, :KV_CH], kg_ref[...], cos, sin, 1.0).astype(BF16)
    v_ref[0] = zkv[:, KV_CH:].astype(BF16)
    f_ref[0] = _dot(f, dft_ref[...]).astype(BF16)
    gt_ref[0] = _dot(h, wg_ref[...])


def _project(x, sc, sh, norm_g, wts, qg, kg, cos_t, sin_t, use_rope, tm):
    B, L, D = x.shape
    per_batch = sc.shape[0] > 1
    mod_map = (lambda b, i: (b, 0, 0)) if per_batch else (lambda b, i: (0, 0, 0))
    full = lambda a: pl.BlockSpec(a.shape, lambda b, i: (0,) * a.ndim)
    row = lambda n: pl.BlockSpec((1, tm, n), lambda b, i: (b, i, 0))
    wf, dft, wq, wkv, wd, wz, wg = wts
    out_shape = (jax.ShapeDtypeStruct((B, L, 2 * FT_CH), BF16),
                 jax.ShapeDtypeStruct((B, L, ATT_CH), BF16),
                 jax.ShapeDtypeStruct((B, L, KV_CH), BF16),
                 jax.ShapeDtypeStruct((B, L, KV_CH), BF16),
                 jax.ShapeDtypeStruct((B, L, CONV_CH), F32),
                 jax.ShapeDtypeStruct((B, L, DN_CH), F32),
                 jax.ShapeDtypeStruct((B, L, LANES), F32))
    return pl.pallas_call(
        functools.partial(_project_kernel, use_rope),
        out_shape=out_shape,
        grid=(B, L // tm),
        in_specs=[row(D),
                  pl.BlockSpec((1, 1, D), mod_map), pl.BlockSpec((1, 1, D), mod_map),
                  full(norm_g), full(wf), full(dft), full(wq), full(wkv), full(wd), full(wz),
                  full(wg), full(qg), full(kg),
                  pl.BlockSpec((tm, LANES), lambda b, i: (i, 0)),
                  pl.BlockSpec((tm, LANES), lambda b, i: (i, 0))],
        out_specs=(row(2 * FT_CH), row(ATT_CH), row(KV_CH), row(KV_CH), row(CONV_CH),
                   row(DN_CH), row(LANES)),
        compiler_params=pltpu.CompilerParams(
            dimension_semantics=("arbitrary", "arbitrary"), vmem_limit_bytes=VMEM_LIMIT),
        name="project_rope" if use_rope else "project",
    )(x, sc, sh, norm_g, wf, dft, wq, wkv, wd, wz, wg, qg, kg, cos_t, sin_t)


def _fourier_kernel(scale, dc_ref, ds_ref, f_ref, o_ref):
    fcs = f_ref[0]
    y = _dot(dc_ref[...], fcs[:, :FT_CH]) - _dot(ds_ref[...], fcs[:, FT_CH:])
    o_ref[0] = (y * scale).astype(BF16)


def _fourier(fcs, dft_c, dft_s, tn):
    B, L, _ = fcs.shape
    scale = float(1.0 / np.sqrt(L * HEAD_DIM))
    return pl.pallas_call(
        functools.partial(_fourier_kernel, scale),
        out_shape=jax.ShapeDtypeStruct((B, L, FT_CH), BF16),
        grid=(L // tn, B),
        in_specs=[pl.BlockSpec((tn, L), lambda n, b: (n, 0)),
                  pl.BlockSpec((tn, L), lambda n, b: (n, 0)),
                  pl.BlockSpec((1, L, 2 * FT_CH), lambda n, b: (b, 0, 0))],
        out_specs=pl.BlockSpec((1, tn, FT_CH), lambda n, b: (b, n, 0)),
        compiler_params=pltpu.CompilerParams(
            dimension_semantics=("arbitrary", "arbitrary"), vmem_limit_bytes=VMEM_LIMIT),
        name="fourier",
    )(dft_c, dft_s, fcs)


def _attention_kernel(n_src, q_ref, *refs):
    kv_refs = refs[:2 * n_src]
    o_ref = refs[2 * n_src]
    tq = q_ref.shape[1]
    kv_heads = range(ATT_HKV)
    scores, row_max = [], []
    for h in kv_heads:
        q = q_ref[0, :, h * ATT_GROUP * HEAD_DIM:(h + 1) * ATT_GROUP * HEAD_DIM]
        q4 = jnp.concatenate([q[:, g * HEAD_DIM:(g + 1) * HEAD_DIM] for g in range(ATT_GROUP)],
                             axis=0)
        sc_h, m = [], None
        for s in range(n_src):
            k = kv_refs[2 * s][0, :, h * HEAD_DIM:(h + 1) * HEAD_DIM]
            sc = _dot_nt(q4, k)
            sc_h.append(sc)
            ms = jnp.max(sc, axis=-1, keepdims=True)
            m = ms if m is None else jnp.maximum(m, ms)
        scores.append(sc_h)
        row_max.append(m)
    accs = []
    for h in kv_heads:
        acc = None
        for s in range(n_src):
            v = kv_refs[2 * s + 1][0]
            lane = lax.broadcasted_iota(jnp.int32, v.shape, 1)
            in_head = (lane >= h * HEAD_DIM) & (lane < (h + 1) * HEAD_DIM)
            v_aug = jnp.where(in_head, v, jnp.ones_like(v))
            p = jnp.exp(scores[h][s] - row_max[h]).astype(BF16)
            pv = _dot(p, v_aug)
            acc = pv if acc is None else acc + pv
        accs.append(acc)
    for h in kv_heads:
        o = accs[h][:, h * HEAD_DIM:(h + 1) * HEAD_DIM]
        den = accs[h][:, (1 - h) * HEAD_DIM:(2 - h) * HEAD_DIM]
        o = (o / den).astype(BF16)
        for g in range(ATT_GROUP):
            c0 = (h * ATT_GROUP + g) * HEAD_DIM
            o_ref[0, :, c0:c0 + HEAD_DIM] = o[g * tq:(g + 1) * tq]


def _attention(q, kv_sources, tq):
    B, L, _ = q.shape
    n_src = len(kv_sources)
    in_specs = [pl.BlockSpec((1, tq, ATT_CH), lambda b, i: (b, i, 0))]
    args = [q]
    for k, v in kv_sources:
        lk = k.shape[1]
        in_specs += [pl.BlockSpec((1, lk, KV_CH), lambda b, i: (b, 0, 0)),
                     pl.BlockSpec((1, lk, KV_CH), lambda b, i: (b, 0, 0))]
        args += [k, v]
    return pl.pallas_call(
        functools.partial(_attention_kernel, n_src),
        out_shape=jax.ShapeDtypeStruct((B, L, ATT_CH), BF16),
        grid=(B, L // tq),
        in_specs=in_specs,
        out_specs=pl.BlockSpec((1, tq, ATT_CH), lambda b, i: (b, i, 0)),
        compiler_params=pltpu.CompilerParams(
            dimension_semantics=("arbitrary", "arbitrary"), vmem_limit_bytes=VMEM_LIMIT),
        name="attention",
    )(*args)


def _lane_col(x, c):
    lane = lax.broadcasted_iota(jnp.int32, x.shape, 1)
    return jnp.sum(jnp.where(lane == c, x, 0.0), axis=1, keepdims=True)


def _head_bcast(cols, lane_head):
    out = cols[DN_HEADS - 1]
    for h in range(DN_HEADS - 2, -1, -1):
        out = jnp.where(lane_head <= h, cols[h], out)
    return out


def _conv_silu_norm(x, prev_row, next_row, w_ref):
    n = x.shape[0]
    row = lax.broadcasted_iota(jnp.int32, x.shape, 0)
    x_m1 = jnp.where(row == 0, prev_row, pltpu.roll(x, 1, axis=0))
    x_p1 = jnp.where(row == n - 1, next_row, pltpu.roll(x, n - 1, axis=0))
    y = _silu(x_m1 * w_ref[0:1, :] + x * w_ref[1:2, :] + x_p1 * w_ref[2:3, :])
    qk = y[:, :2 * DN_CH]
    qk = qk * lax.rsqrt(_seg64_sum(qk * qk, 2) + EPS)
    return jnp.concatenate([qk[:, :DN_CH] * (HEAD_DIM ** -0.5), qk[:, DN_CH:], y[:, 2 * DN_CH:]],
                           axis=1)


def _gates_to_beta_g(z, alog_ref, dtb_ref):
    lane = lax.broadcasted_iota(jnp.int32, z.shape, 1)
    g = -jnp.exp(alog_ref[...]) * _softplus(z + dtb_ref[...])
    return jnp.where(lane < 2 * DN_HEADS, _sigmoid(z), g)


def _gdn_prep(direction, qkv, bg, cum_ref, ones_ref, pu_ref, pwq_ref, pk_ref, pe_ref):
    cum_m = cum_ref[...]
    ones_m = ones_ref[...]
    ri = lax.broadcasted_iota(jnp.int32, (GDN_TILE, GDN_TILE), 0)
    ci = lax.broadcasted_iota(jnp.int32, (GDN_TILE, GDN_TILE), 1)
    blk = (ri // CHUNK) == (ci // CHUNK)
    lane_head = ci // HEAD_DIM
    if direction == 0:
        tri_strict, tri_incl = blk & (ri > ci), blk & (ri >= ci)
    else:
        tri_strict, tri_incl = blk & (ri < ci), blk & (ri <= ci)
    q = qkv[:, 0:DN_CH]
    k = qkv[:, DN_CH:2 * DN_CH]
    v = qkv[:, 2 * DN_CH:3 * DN_CH]

    gc = _dot_exact_lhs(cum_m, bg)
    gt = _dot_exact_lhs(ones_m, bg)
    gc_t = gc.T
    row_t = lax.broadcasted_iota(jnp.int32, gc_t.shape, 0)

    beta_cols, gc_cols, gt_cols = [], [], []
    for h in range(DN_HEADS):
        beta_cols.append(_lane_col(bg, direction * DN_HEADS + h))
        gc_cols.append(_lane_col(gc, 2 * DN_HEADS + direction * DN_HEADS + h))
        gt_cols.append(_lane_col(gt, 2 * DN_HEADS + direction * DN_HEADS + h))
    beta_b = _head_bcast(beta_cols, lane_head)
    gc_b = _head_bcast(gc_cols, lane_head)
    gt_b = _head_bcast(gt_cols, lane_head)

    e_gc = jnp.exp(gc_b)
    kb = k * beta_b
    vb = v * beta_b
    kbe = kb * e_gc
    qe = q * e_gc
    kd = k * jnp.exp(gt_b - gc_b)
    pe_ref[...] = jnp.exp(gt_b)
    pk_ref[0] = kd.T.astype(BF16)
    pwq_ref[1] = qe.astype(BF16)

    k16 = k.astype(BF16)
    rhs_uw = jnp.concatenate([vb, kbe], axis=1).astype(BF16)
    eye = (ri == ci).astype(F32)
    yield

    heads = range(DN_HEADS)
    neg_a = []
    for h in heads:
        in_h = lane_head == h
        r = jnp.sum(jnp.where(row_t == 2 * DN_HEADS + direction * DN_HEADS + h, gc_t, 0.0),
                    axis=0, keepdims=True)
        diff = gc_cols[h] - r
        dec_i = jnp.exp(jnp.where(tri_incl, diff, NEG_BIG))
        dec_s = jnp.where(tri_strict, dec_i, 0.0)
        kk = _dot_nt(jnp.where(in_h, kb, 0.0).astype(BF16), k16)
        qk = _dot_nt(jnp.where(in_h, q, 0.0).astype(BF16), k16)
        neg_a.append(-(kk * dec_s))
        pk_ref[1 + h] = (qk * dec_i).astype(BF16)
        yield

    t = [eye + jnp.where((ri // 2) == (ci // 2), neg_a[h], 0.0) for h in heads]
    b = 2
    while b < CHUNK:
        lvl = ((ri // (2 * b)) == (ci // (2 * b))) & ((ri // b) != (ci // b))
        t16 = [t[h].astype(BF16) for h in heads]
        g = [_dot(jnp.where(lvl, neg_a[h], 0.0).astype(BF16), t16[h]) for h in heads]
        yield
        t = [t[h] + _dot(t16[h], g[h].astype(BF16)) for h in heads]
        yield
        b *= 2
    uw = [_dot(t[h].astype(BF16), rhs_uw) for h in heads]
    u_all, w_all = uw[DN_HEADS - 1][:, :DN_CH], uw[DN_HEADS - 1][:, DN_CH:]
    for h in range(DN_HEADS - 2, -1, -1):
        u_all = jnp.where(lane_head <= h, uw[h][:, :DN_CH], u_all)
        w_all = jnp.where(lane_head <= h, uw[h][:, DN_CH:], w_all)
    pu_ref[...] = u_all
    pwq_ref[0] = w_all.astype(BF16)


def _gdn_scan(direction, s_ref, pu_ref, pwq_ref, pk_ref, pe_ref, o_ref, o_rows):
    ri = lax.broadcasted_iota(jnp.int32, (GDN_TILE, GDN_TILE), 0)
    ci = lax.broadcasted_iota(jnp.int32, (GDN_TILE, GDN_TILE), 1)
    blk = (ri // HEAD_DIM) == (ci // HEAD_DIM)
    order = range(CHUNKS_PER_TILE) if direction == 0 else range(CHUNKS_PER_TILE - 1, -1, -1)
    outs = [None] * CHUNKS_PER_TILE
    zeros_c = jnp.zeros((CHUNK, DN_CH), BF16)
    lane_head_c = lax.broadcasted_iota(jnp.int32, (CHUNK, DN_CH), 1) // HEAD_DIM
    for c in order:
        rows = pl.ds(c * CHUNK, CHUNK)
        s = s_ref[...]
        wq = jnp.concatenate([pwq_ref[0, rows, :], pwq_ref[1, rows, :]], axis=0)
        ws = _dot(wq, s.astype(BF16))
        yield
        v_new = (pu_ref[rows, :] - ws[:CHUNK]).astype(BF16)
        v_tile = jnp.concatenate([v_new if i == c else zeros_c for i in range(CHUNKS_PER_TILE)],
                                 axis=0)
        lhs = jnp.concatenate([pk_ref[0]] + [pk_ref[1 + h, rows, :] for h in range(DN_HEADS)],
                              axis=0)
        r2 = _dot(lhs, v_tile)
        s_ref[...] = s * pe_ref[pl.ds(c * CHUNK, 1), :] + jnp.where(blk, r2[:GDN_TILE], 0.0)
        o = ws[CHUNK:]
        for h in range(DN_HEADS):
            o = o + jnp.where(lane_head_c == h,
                              r2[GDN_TILE + h * CHUNK:GDN_TILE + (h + 1) * CHUNK], 0.0)
        outs[c] = o
        yield
    o_ref[o_rows, :] = jnp.concatenate(outs, axis=0)


def _trace_interleaved(stages):
    live = [[g, p] for g, p in stages]
    r = 0
    while live:
        for item in list(live):
            if r % item[1] == 0 and next(item[0], StopIteration) is StopIteration:
                live.remove(item)
        r += 1


def _gdn_kernel(n_lat, write_ctx, dc_ref, dl_ref, gc_ref, gl_ref, zc_ref, zl_ref, cw_ref,
                alog_ref, dtb_ref, on_ref, cum_f_ref, cum_b_ref, ones_ref,
                ol_ref, oc_ref, qkv_ref, bg_ref, of_ref, ob_ref, sf_ref, sb_ref,
                pu_ref, pwq_ref, pk_ref, pe_ref):
    T = GDN_TILE
    zero_row = jnp.zeros((1, CONV_CH), F32)

    qkv_ref[0:T] = _conv_silu_norm(dc_ref[0], zero_row, zero_row, cw_ref)
    bg_ref[0:T] = _gates_to_beta_g(gc_ref[0], alog_ref, dtb_ref)

    def conv_stages(tiles):
        for t in tiles:
            r0 = (t - 1) * T
            prev_row = zero_row if t == 1 else dl_ref[0, pl.ds(r0 - 1, 1), :]
            next_row = zero_row if t == n_lat else dl_ref[0, pl.ds(r0 + T, 1), :]
            qkv_ref[pl.ds(t * T, T), :] = _conv_silu_norm(dl_ref[0, pl.ds(r0, T), :], prev_row,
                                                          next_row, cw_ref)
            bg_ref[pl.ds(t * T, T), :] = _gates_to_beta_g(gl_ref[0, pl.ds(r0, T), :], alog_ref,
                                                          dtb_ref)
            yield

    sf_ref[...] = jnp.zeros_like(sf_ref)
    sb_ref[...] = jnp.zeros_like(sb_ref)

    s_refs = (sf_ref, sb_ref)
    cum_refs = (cum_f_ref, cum_b_ref)
    o_refs = (of_ref, ob_ref)

    def tile_rows(direction, s):
        if isinstance(s, int):
            t = s if (direction == 0 or s == 0) else n_lat + 1 - s
            return pl.ds(t * T, T)
        t = s if direction == 0 else jnp.where(s == 0, 0, n_lat + 1 - s)
        return pl.ds(pl.multiple_of(t * T, T), T)

    def prep_stages(s, slot):
        return [(_gdn_prep(d, qkv_ref[tile_rows(d, s), :], bg_ref[tile_rows(d, s), :], cum_refs[d],
                           ones_ref, pu_ref.at[slot, d], pwq_ref.at[slot, d], pk_ref.at[slot, d],
                           pe_ref.at[slot, d]), 1) for d in range(2)]

    def scan_stages(s, slot):
        return [(_gdn_scan(d, s_refs[d], pu_ref.at[slot, d], pwq_ref.at[slot, d],
                           pk_ref.at[slot, d], pe_ref.at[slot, d], o_refs[d], tile_rows(d, s)),
                 SCAN_STAGE_PERIOD) for d in range(2)]

    def step(s, slot, conv_tiles=()):
        conv = [(conv_stages(conv_tiles), CONV_STAGE_PERIOD)] if conv_tiles else []
        _trace_interleaved(scan_stages(s, slot) + prep_stages(s + 1, 1 - slot) + conv)

    _trace_interleaved(prep_stages(0, 0) + [(conv_stages(sorted({1, n_lat})), CONV_STAGE_PERIOD)])
    conv_steps = n_lat // 2 - 1
    static_pairs = (conv_steps + 1) // 2
    for s in range(2 * static_pairs):
        step(s, s % 2, (s + 2, n_lat - 1 - s) if s < conv_steps else ())

    def scan_pair(j, carry):
        step(2 * j, 0)
        step(2 * j + 1, 1)
        return carry

    lax.fori_loop(static_pairs, n_lat // 2, scan_pair, 0)

    def finish(o, z):
        ms = _seg64_sum(o * o, 2) * (1.0 / HEAD_DIM)
        return (o * lax.rsqrt(ms + EPS) * on_ref[...] * _silu(z)).astype(BF16)

    def finish_stages(tiles):
        for t in tiles:
            if t == 0:
                if write_ctx:
                    oc_ref[0] = finish(of_ref[0:T] + ob_ref[0:T], zc_ref[0])
                else:
                    oc_ref[0] = jnp.zeros(oc_ref.shape[1:], BF16)
            else:
                rows = pl.ds(t * T, T)
                ol_ref[0, pl.ds((t - 1) * T, T), :] = finish(of_ref[rows, :] + ob_ref[rows, :],
                                                             zl_ref[0, pl.ds((t - 1) * T, T), :])
            yield

    last_tiles = sorted({1, n_lat})
    _trace_interleaved(scan_stages(n_lat, 0)
                       + [(finish_stages([t for t in range(n_lat + 1) if t not in last_tiles]), 1)])
    _trace_interleaved([(finish_stages(last_tiles), 1)])


def _gdn(d_ctx, d_lat, g_ctx, g_lat, z_ctx, z_lat, conv_w, alog_row, dtb_row, on_row, consts,
         write_ctx):
    B, L, _ = d_lat.shape
    Lc = d_ctx.shape[1]
    assert Lc == GDN_TILE and L % (2 * GDN_TILE) == 0
    n_lat = L // GDN_TILE
    T = GDN_TILE
    cum_f, cum_b, ones_m = consts
    per_b = lambda n, c: pl.BlockSpec((1, n, c), lambda b: (b, 0, 0))
    single = lambda n, c: pl.BlockSpec((1, n, c), lambda b: (b, 0, 0), pipeline_mode=pl.Buffered(1))
    full = lambda a: pl.BlockSpec(a.shape, lambda b: (0,) * a.ndim)
    tot = L + Lc
    return pl.pallas_call(
        functools.partial(_gdn_kernel, n_lat, write_ctx),
        out_shape=(jax.ShapeDtypeStruct((B, L, DN_CH), BF16),
                   jax.ShapeDtypeStruct((B, Lc, DN_CH), BF16)),
        grid=(B,),
        in_specs=[per_b(Lc, CONV_CH), per_b(L, CONV_CH), per_b(Lc, LANES), single(L, LANES),
                  per_b(Lc, DN_CH), single(L, DN_CH), full(conv_w), full(alog_row), full(dtb_row),
                  full(on_row), full(cum_f), full(cum_b), full(ones_m)],
        out_specs=(per_b(L, DN_CH), per_b(Lc, DN_CH)),
        scratch_shapes=[pltpu.VMEM((tot, CONV_CH), F32), pltpu.VMEM((tot, LANES), F32),
                        pltpu.VMEM((tot, DN_CH), F32), pltpu.VMEM((tot, DN_CH), F32),
                        pltpu.VMEM((T, DN_CH), F32), pltpu.VMEM((T, DN_CH), F32),
                        pltpu.VMEM((2, 2, T, DN_CH), F32), pltpu.VMEM((2, 2, 2, T, DN_CH), BF16),
                        pltpu.VMEM((2, 2, 1 + DN_HEADS, T, T), BF16), pltpu.VMEM((2, 2, T, DN_CH), F32)],
        compiler_params=pltpu.CompilerParams(
            dimension_semantics=("arbitrary",), vmem_limit_bytes=VMEM_LIMIT),
        name="gdn",
    )(d_ctx, d_lat, g_ctx, g_lat, z_ctx, z_lat, conv_w, alog_row, dtb_row, on_row,
      cum_f, cum_b, ones_m)


def _mix_mlp_kernel(ff_tile, x_ref, fy_ref, at_ref, dn_ref, g1_ref, sc_ref, sh_ref, g2_ref, ng_ref,
                    wof_ref, woa_ref, wod_ref, w1_ref, w2_ref, o_ref):
    mix = (_dot(fy_ref[0], wof_ref[...]) + _dot(at_ref[0], woa_ref[...])
           + _dot(dn_ref[0], wod_ref[...]))
    x1 = x_ref[0] + g1_ref[0] * mix
    ms = jnp.mean(x1 * x1, axis=-1, keepdims=True)
    gain = ng_ref[...] * (1.0 + sc_ref[0])
    h = (x1 * lax.rsqrt(ms + EPS) * gain + sh_ref[0]).astype(BF16)
    acc = jnp.zeros_like(x1)
    for c in range(w1_ref.shape[1] // ff_tile):
        u = jnp.maximum(_dot(h, w1_ref[:, c * ff_tile:(c + 1) * ff_tile]), 0.0)
        acc = acc + _dot((u * u).astype(BF16), w2_ref[c * ff_tile:(c + 1) * ff_tile, :])
    o_ref[0] = x1 + g2_ref[0] * acc


def _mix_mlp(x, fy, att, dn, g1, sc, sh, g2, norm_g, wts, tm):
    B, L, D = x.shape
    per_batch = g1.shape[0] > 1
    mod_map = (lambda b, i: (b, 0, 0)) if per_batch else (lambda b, i: (0, 0, 0))
    mod = pl.BlockSpec((1, 1, D), mod_map)
    full = lambda a: pl.BlockSpec(a.shape, lambda b, i: (0,) * a.ndim,
                                  pipeline_mode=pl.Buffered(1))
    row = lambda n: pl.BlockSpec((1, tm, n), lambda b, i: (b, i, 0))
    wof, woa, wod, w1, w2 = wts
    return pl.pallas_call(
        functools.partial(_mix_mlp_kernel, 1024),
        out_shape=jax.ShapeDtypeStruct((B, L, D), F32),
        grid=(B, L // tm),
        in_specs=[row(D), row(FT_CH), row(ATT_CH), row(DN_CH), mod, mod, mod, mod,
                  pl.BlockSpec(norm_g.shape, lambda b, i: (0, 0)),
                  full(wof), full(woa), full(wod), full(w1), full(w2)],
        out_specs=row(D),
        compiler_params=pltpu.CompilerParams(
            dimension_semantics=("arbitrary", "arbitrary"), vmem_limit_bytes=VMEM_LIMIT),
        name="mix_mlp",
    )(x, fy, att, dn, g1, sc, sh, g2, norm_g, wof, woa, wod, w1, w2)


def _rope_tables(S):
    rows = S // GRID_W
    t_row = jnp.repeat(jnp.arange(rows), GRID_W).astype(F32)
    t_col = jnp.tile(jnp.arange(GRID_W), rows).astype(F32)
    inv_freq = ROPE_THETA ** (-jnp.arange(ROT_PAIRS, dtype=F32) * 2.0 / AXIS_DIM)
    ang_r = t_row[:, None] * inv_freq
    ang_c = t_col[:, None] * inv_freq
    ang = jnp.concatenate([ang_r, ang_r, ang_c, ang_c], axis=-1)
    cos, sin = jnp.cos(ang), jnp.sin(ang)
    sign = jnp.where((jnp.arange(HEAD_DIM) & ROT_PAIRS) == 0, -1.0, 1.0).astype(F32)
    tile2 = lambda a: jnp.concatenate([a, a], axis=-1)
    return tile2(cos), tile2(sin * sign)


def _cos_sin(rows, cols, period):
    k = (jnp.arange(rows, dtype=jnp.int32)[:, None] * jnp.arange(cols, dtype=jnp.int32)[None, :]) % period
    ang = k.astype(F32) * np.float32(2.0 * np.pi / period)
    return jnp.cos(ang), jnp.sin(ang)


def _dft_tables(n):
    if n <= HEAD_DIM:
        return _cos_sin(n, n, n)
    assert n % HEAD_DIM == 0
    ca, sa = _cos_sin(n, n // HEAD_DIM, n // HEAD_DIM)
    cb, sb = _cos_sin(n, HEAD_DIM, n)
    cos = ca[:, :, None] * cb[:, None, :] - sa[:, :, None] * sb[:, None, :]
    sin = sa[:, :, None] * cb[:, None, :] + ca[:, :, None] * sb[:, None, :]
    return cos.reshape(n, n), sin.reshape(n, n)


def _channel_dft():
    c, s = _dft_tables(HEAD_DIM)
    eye = jnp.eye(FT_GROUPS, dtype=F32)
    return jnp.concatenate([jnp.kron(eye, c), jnp.kron(eye, s)], axis=1).astype(BF16)


def _gdn_consts():
    i = np.arange(GDN_TILE)
    blk = (i[:, None] // CHUNK) == (i[None, :] // CHUNK)
    cum_f = (blk & (i[:, None] >= i[None, :])).astype(np.float32)
    cum_b = (blk & (i[:, None] <= i[None, :])).astype(np.float32)
    return (jnp.asarray(cum_f, BF16), jnp.asarray(cum_b, BF16), jnp.asarray(blk.astype(np.float32), BF16))


def _pad_lanes(a):
    flat = a.reshape(1, -1).astype(F32)
    return jnp.pad(flat, ((0, 0), (0, LANES - flat.shape[1])))


def kernel(x, c, ctx, c_ctx, norm1_g, norm2_g, w_mod, b_mod, w_in, conv_w, q_norm_g, k_norm_g,
           a_log, dt_bias, o_norm_g, w_out, w_ff1, w_ff2):
    B, S, D = x.shape
    Lc = ctx.shape[1]
    depth = w_mod.shape[0]

    rows = ((B + 1 + 7) // 8) * 8
    c_all = jnp.concatenate([c, c_ctx[None, :], jnp.zeros((rows - B - 1, D), F32)], axis=0)
    mod_all = _modulation(c_all, w_mod, b_mod)

    cos_t, sin_t = _rope_tables(S)
    dft_ch = _channel_dft()
    dft_lat = tuple(t.astype(BF16) for t in _dft_tables(S))
    dft_ctx = tuple(t.astype(BF16) for t in _dft_tables(Lc))
    gdn_consts = _gdn_consts()

    o0 = FT_CH
    o1 = o0 + ATT_CH
    o2 = o1 + 2 * KV_CH
    o3 = o2 + CONV_CH
    o4 = o3 + DN_CH

    x_lat, x_ctx = x, ctx
    for l in range(depth):
        last = l == depth - 1
        w = w_in[l]
        wg = jnp.pad(w[:, o4:], ((0, 0), (0, LANES - N_GATES)))
        in_wts = (w[:, :o0].astype(BF16), dft_ch, w[:, o0:o1].astype(BF16), w[:, o1:o2].astype(BF16),
                  w[:, o2:o3].astype(BF16), w[:, o3:o4].astype(BF16), wg.astype(BF16))
        qg = jnp.tile(q_norm_g[l], ATT_HQ).reshape(1, ATT_CH)
        kg = jnp.tile(k_norm_g[l], ATT_HKV).reshape(1, KV_CH)
        n1 = norm1_g[l].reshape(1, D)
        n2 = norm2_g[l].reshape(1, D)
        wo = w_out[l].astype(BF16)
        out_wts = (wo[:FT_CH], wo[FT_CH:FT_CH + ATT_CH], wo[FT_CH + ATT_CH:],
                   w_ff1[l].astype(BF16), w_ff2[l].astype(BF16))
        alog_row = jnp.pad(a_log[l].reshape(1, -1), ((0, 0), (2 * DN_HEADS, LANES - N_GATES)))
        dtb_row = jnp.pad(dt_bias[l].reshape(1, -1), ((0, 0), (2 * DN_HEADS, LANES - N_GATES)))
        on_row = jnp.tile(o_norm_g[l], DN_HEADS).reshape(1, DN_CH)

        mod = mod_all[l, :B].reshape(B, 1, 6 * D)
        modc = mod_all[l, B:B + 1].reshape(1, 1, 6 * D)
        sh1, sc1, g1, sh2, sc2, g2 = [mod[:, :, i * D:(i + 1) * D] for i in range(6)]
        csh1, csc1, cg1, csh2, csc2, cg2 = [modc[:, :, i * D:(i + 1) * D] for i in range(6)]

        fl, ql, kl, vl, dl, zl, gl = _project(x_lat, sc1, sh1, n1, in_wts, qg, kg, cos_t, sin_t,
                                              True, 512)
        fc, qc, kc, vc, dc, zc, gc = _project(x_ctx, csc1, csh1, n1, in_wts, qg, kg,
                                              cos_t[:Lc], sin_t[:Lc], False, Lc)

        dn_lat, dn_ctx = _gdn(dc, dl, gc, gl, zc, zl, conv_w[l], alog_row, dtb_row, on_row,
                              gdn_consts, not last)
        att_lat = _attention(ql, [(kl, vl), (kc, vc)], 256)
        fy_lat = _fourier(fl, dft_lat[0], dft_lat[1], min(S, 1024))
        x_lat = _mix_mlp(x_lat, fy_lat, att_lat, dn_lat, g1, sc2, sh2, g2, n2, out_wts, 512)
        if not last:
            att_ctx = _attention(qc, [(kc, vc)], 128)
            fy_ctx = _fourier(fc, dft_ctx[0], dft_ctx[1], Lc)
            x_ctx = _mix_mlp(x_ctx, fy_ctx, att_ctx, dn_ctx, cg1, csc2, csh2, cg2, n2, out_wts, Lc)
    return x_lat
```

```python
import functools

import jax
import jax.numpy as jnp
import numpy as np
from jax import lax
from jax.experimental import pallas as pl
from jax.experimental.pallas import tpu as pltpu

HEAD_DIM = 64
FT_GROUPS = 4
FT_CH = FT_GROUPS * HEAD_DIM
ATT_HQ = 8
ATT_HKV = 2
ATT_GROUP = ATT_HQ // ATT_HKV
ATT_CH = ATT_HQ * HEAD_DIM
KV_CH = ATT_HKV * HEAD_DIM
DN_HEADS = 4
DN_CH = DN_HEADS * HEAD_DIM
CONV_CH = 3 * DN_CH
N_GATES = 4 * DN_HEADS
GRID_W = 64
ROPE_THETA = 10000.0
AXIS_DIM = HEAD_DIM // 2
ROT_PAIRS = AXIS_DIM // 2
EPS = 1e-6
CHUNK = 64

LANES = 128
MXU_DIM = 256
V7X_VMEM_BYTES = 64 * 1024 * 1024
VMEM_LIMIT = V7X_VMEM_BYTES - 8 * 1024 * 1024

GDN_TILE = 256
CHUNKS_PER_TILE = GDN_TILE // CHUNK
NEG_BIG = -1e30
SCAN_STAGE_PERIOD = 1
CONV_STAGE_PERIOD = 2

F32 = jnp.float32
BF16 = jnp.bfloat16


def _dot(a, b):
    return jnp.dot(a, b, preferred_element_type=F32)


def _dot_nt(a, b):
    return lax.dot_general(a, b, (((1,), (1,)), ((), ())), preferred_element_type=F32)


def _split3(x):
    hi = x.astype(BF16)
    r1 = x - hi.astype(F32)
    mid = r1.astype(BF16)
    lo = (r1 - mid.astype(F32)).astype(BF16)
    return hi, mid, lo


def _dot_exact_lhs(m_bf16, x):
    hi, mid, lo = _split3(x)
    return _dot(m_bf16, hi) + _dot(m_bf16, mid) + _dot(m_bf16, lo)


def _seg64_sum(x, passes):
    w = x.shape[1]
    blk = min(w, MXU_DIM)
    r = lax.broadcasted_iota(jnp.int32, (blk, blk), 0) // HEAD_DIM
    c = lax.broadcasted_iota(jnp.int32, (blk, blk), 1) // HEAD_DIM
    ones = jnp.where(r == c, 1.0, 0.0).astype(BF16)
    outs = []
    for j in range(w // blk):
        xs = x[:, j * blk:(j + 1) * blk]
        hi = xs.astype(BF16)
        acc = _dot(hi, ones)
        if passes == 2:
            acc = acc + _dot((xs - hi.astype(F32)).astype(BF16), ones)
        outs.append(acc)
    return outs[0] if len(outs) == 1 else jnp.concatenate(outs, axis=1)


def _sigmoid(x):
    return 1.0 / (1.0 + jnp.exp(-x))


def _silu(x):
    return x * _sigmoid(x)


def _softplus(x):
    return jnp.maximum(x, 0.0) + jnp.log1p(jnp.exp(-jnp.abs(x)))


def _mod_kernel(c_ref, w_ref, b_ref, o_ref):
    a = _silu(c_ref[...])
    o_ref[0] = jnp.dot(a, w_ref[0], preferred_element_type=F32,
                       precision=lax.Precision.HIGHEST) + b_ref[0]


def _modulation(c_all, w_mod, b_mod):
    depth, d, n = w_mod.shape
    rows = c_all.shape[0]
    tn = 1024
    return pl.pallas_call(
        _mod_kernel,
        out_shape=jax.ShapeDtypeStruct((depth, rows, n), F32),
        grid=(depth, n // tn),
        in_specs=[pl.BlockSpec((rows, d), lambda l, j: (0, 0)),
                  pl.BlockSpec((1, d, tn), lambda l, j: (l, 0, j)),
                  pl.BlockSpec((1, 1, tn), lambda l, j: (l, 0, j))],
        out_specs=pl.BlockSpec((1, rows, tn), lambda l, j: (l, 0, j)),
        compiler_params=pltpu.CompilerParams(
            dimension_semantics=("arbitrary", "arbitrary"), vmem_limit_bytes=VMEM_LIMIT),
        name="modulation",
    )(c_all, w_mod, b_mod.reshape(depth, 1, n))


def _head_rms_rope(z, gain, cos, sin_signed, scale):
    ms = _seg64_sum(z * z, 1) * (1.0 / HEAD_DIM)
    y = z * lax.rsqrt(ms + EPS) * gain
    if scale != 1.0:
        y = y * scale
    if cos is None:
        return y
    lane = lax.broadcasted_iota(jnp.int32, cos.shape, 1)
    slabs = []
    for j in range(z.shape[1] // LANES):
        ys = y[:, j * LANES:(j + 1) * LANES]
        partner = jnp.where((lane & ROT_PAIRS) == 0,
                            pltpu.roll(ys, LANES - ROT_PAIRS, axis=1),
                            pltpu.roll(ys, ROT_PAIRS, axis=1))
        slabs.append(ys * cos + partner * sin_signed)
    return slabs[0] if len(slabs) == 1 else jnp.concatenate(slabs, axis=1)


def _project_kernel(use_rope, x_ref, sc_ref, sh_ref, g_ref, wf_ref, dft_ref, wq_ref, wkv_ref,
                    wd_ref, wz_ref, wg_ref, qg_ref, kg_ref, cos_ref, sin_ref,
                    f_ref, q_ref, k_ref, v_ref, d_ref, z_ref, gt_ref):
    x = x_ref[0]
    ms = jnp.mean(x * x, axis=-1, keepdims=True)
    gain = g_ref[...] * (1.0 + sc_ref[0])
    h = (x * lax.rsqrt(ms + EPS) * gain + sh_ref[0]).astype(BF16)

    cos = cos_ref[...] if use_rope else None
    sin = sin_ref[...] if use_rope else None

    zq = _dot(h, wq_ref[...])
    zkv = _dot(h, wkv_ref[...])
    f = _dot(h, wf_ref[...]).astype(BF16)
    d_ref[0] = _dot(h, wd_ref[...])
    q_ref[0] = _head_rms_rope(zq, qg_ref[...], cos, sin, HEAD_DIM ** -0.5).astype(BF16)
    z_ref[0] = _dot(h, wz_ref[...])
    k_ref[0] = _head_rms_rope(zkv[:, :KV_CH], kg_ref[...], cos, sin, 1.0).astype(BF16)
    v_ref[0] = zkv[:, KV_CH:].astype(BF16)
    f_ref[0] = _dot(f, dft_ref[...]).astype(BF16)
    gt_ref[0] = _dot(h, wg_ref[...])


def _project(x, sc, sh, norm_g, wts, qg, kg, cos_t, sin_t, use_rope, tm):
    B, L, D = x.shape
    per_batch = sc.shape[0] > 1
    mod_map = (lambda b, i: (b, 0, 0)) if per_batch else (lambda b, i: (0, 0, 0))
    full = lambda a: pl.BlockSpec(a.shape, lambda b, i: (0,) * a.ndim)
    row = lambda n: pl.BlockSpec((1, tm, n), lambda b, i: (b, i, 0))
    wf, dft, wq, wkv, wd, wz, wg = wts
    out_shape = (jax.ShapeDtypeStruct((B, L, 2 * FT_CH), BF16),
                 jax.ShapeDtypeStruct((B, L, ATT_CH), BF16),
                 jax.ShapeDtypeStruct((B, L, KV_CH), BF16),
                 jax.ShapeDtypeStruct((B, L, KV_CH), BF16),
                 jax.ShapeDtypeStruct((B, L, CONV_CH), F32),
                 jax.ShapeDtypeStruct((B, L, DN_CH), F32),
                 jax.ShapeDtypeStruct((B, L, LANES), F32))
    return pl.pallas_call(
        functools.partial(_project_kernel, use_rope),
        out_shape=out_shape,
        grid=(B, L // tm),
        in_specs=[row(D),
                  pl.BlockSpec((1, 1, D), mod_map), pl.BlockSpec((1, 1, D), mod_map),
                  full(norm_g), full(wf), full(dft), full(wq), full(wkv), full(wd), full(wz),
                  full(wg), full(qg), full(kg),
                  pl.BlockSpec((tm, LANES), lambda b, i: (i, 0)),
                  pl.BlockSpec((tm, LANES), lambda b, i: (i, 0))],
        out_specs=(row(2 * FT_CH), row(ATT_CH), row(KV_CH), row(KV_CH), row(CONV_CH),
                   row(DN_CH), row(LANES)),
        compiler_params=pltpu.CompilerParams(
            dimension_semantics=("arbitrary", "arbitrary"), vmem_limit_bytes=VMEM_LIMIT),
        name="project_rope" if use_rope else "project",
    )(x, sc, sh, norm_g, wf, dft, wq, wkv, wd, wz, wg, qg, kg, cos_t, sin_t)


def _fourier_kernel(scale, dc_ref, ds_ref, f_ref, o_ref):
    fcs = f_ref[0]
    y = _dot(dc_ref[...], fcs[:, :FT_CH]) - _dot(ds_ref[...], fcs[:, FT_CH:])
    o_ref[0] = (y * scale).astype(BF16)


def _fourier(fcs, dft_c, dft_s, tn):
    B, L, _ = fcs.shape
    scale = float(1.0 / np.sqrt(L * HEAD_DIM))
    return pl.pallas_call(
        functools.partial(_fourier_kernel, scale),
        out_shape=jax.ShapeDtypeStruct((B, L, FT_CH), BF16),
        grid=(L // tn, B),
        in_specs=[pl.BlockSpec((tn, L), lambda n, b: (n, 0)),
                  pl.BlockSpec((tn, L), lambda n, b: (n, 0)),
                  pl.BlockSpec((1, L, 2 * FT_CH), lambda n, b: (b, 0, 0))],
        out_specs=pl.BlockSpec((1, tn, FT_CH), lambda n, b: (b, n, 0)),
        compiler_params=pltpu.CompilerParams(
            dimension_semantics=("arbitrary", "arbitrary"), vmem_limit_bytes=VMEM_LIMIT),
        name="fourier",
    )(dft_c, dft_s, fcs)


def _attention_kernel(n_src, q_ref, *refs):
    kv_refs = refs[:2 * n_src]
    o_ref = refs[2 * n_src]
    tq = q_ref.shape[1]
    kv_heads = range(ATT_HKV)
    scores, row_max = [], []
    for h in kv_heads:
        q = q_ref[0, :, h * ATT_GROUP * HEAD_DIM:(h + 1) * ATT_GROUP * HEAD_DIM]
        q4 = jnp.concatenate([q[:, g * HEAD_DIM:(g + 1) * HEAD_DIM] for g in range(ATT_GROUP)],
                             axis=0)
        sc_h, m = [], None
        for s in range(n_src):
            k = kv_refs[2 * s][0, :, h * HEAD_DIM:(h + 1) * HEAD_DIM]
            sc = _dot_nt(q4, k)
            sc_h.append(sc)
            ms = jnp.max(sc, axis=-1, keepdims=True)
            m = ms if m is None else jnp.maximum(m, ms)
        scores.append(sc_h)
        row_max.append(m)
    accs = []
    for h in kv_heads:
        acc = None
        for s in range(n_src):
            v = kv_refs[2 * s + 1][0]
            lane = lax.broadcasted_iota(jnp.int32, v.shape, 1)
            in_head = (lane >= h * HEAD_DIM) & (lane < (h + 1) * HEAD_DIM)
            v_aug = jnp.where(in_head, v, jnp.ones_like(v))
            p = jnp.exp(scores[h][s] - row_max[h]).astype(BF16)
            pv = _dot(p, v_aug)
            acc = pv if acc is None else acc + pv
        accs.append(acc)
    for h in kv_heads:
        o = accs[h][:, h * HEAD_DIM:(h + 1) * HEAD_DIM]
        den = accs[h][:, (1 - h) * HEAD_DIM:(2 - h) * HEAD_DIM]
        o = (o / den).astype(BF16)
        for g in range(ATT_GROUP):
            c0 = (h * ATT_GROUP + g) * HEAD_DIM
            o_ref[0, :, c0:c0 + HEAD_DIM] = o[g * tq:(g + 1) * tq]


def _attention(q, kv_sources, tq):
    B, L, _ = q.shape
    n_src = len(kv_sources)
    in_specs = [pl.BlockSpec((1, tq, ATT_CH), lambda b, i: (b, i, 0))]
    args = [q]
    for k, v in kv_sources:
        lk = k.shape[1]
        in_specs += [pl.BlockSpec((1, lk, KV_CH), lambda b, i: (b, 0, 0)),
                     pl.BlockSpec((1, lk, KV_CH), lambda b, i: (b, 0, 0))]
        args += [k, v]
    return pl.pallas_call(
        functools.partial(_attention_kernel, n_src),
        out_shape=jax.ShapeDtypeStruct((B, L, ATT_CH), BF16),
        grid=(B, L // tq),
        in_specs=in_specs,
        out_specs=pl.BlockSpec((1, tq, ATT_CH), lambda b, i: (b, i, 0)),
        compiler_params=pltpu.CompilerParams(
            dimension_semantics=("arbitrary", "arbitrary"), vmem_limit_bytes=VMEM_LIMIT),
        name="attention",
    )(*args)


def _lane_col(x, c):
    lane = lax.broadcasted_iota(jnp.int32, x.shape, 1)
    return jnp.sum(jnp.where(lane == c, x, 0.0), axis=1, keepdims=True)


def _head_bcast(cols, lane_head):
    out = cols[DN_HEADS - 1]
    for h in range(DN_HEADS - 2, -1, -1):
        out = jnp.where(lane_head <= h, cols[h], out)
    return out


def _conv_silu_norm(x, prev_row, next_row, w_ref):
    n = x.shape[0]
    row = lax.broadcasted_iota(jnp.int32, x.shape, 0)
    x_m1 = jnp.where(row == 0, prev_row, pltpu.roll(x, 1, axis=0))
    x_p1 = jnp.where(row == n - 1, next_row, pltpu.roll(x, n - 1, axis=0))
    y = _silu(x_m1 * w_ref[0:1, :] + x * w_ref[1:2, :] + x_p1 * w_ref[2:3, :])
    qk = y[:, :2 * DN_CH]
    qk = qk * lax.rsqrt(_seg64_sum(qk * qk, 2) + EPS)
    return jnp.concatenate([qk[:, :DN_CH] * (HEAD_DIM ** -0.5), qk[:, DN_CH:], y[:, 2 * DN_CH:]],
                           axis=1)


def _gates_to_beta_g(z, alog_ref, dtb_ref):
    lane = lax.broadcasted_iota(jnp.int32, z.shape, 1)
    g = -jnp.exp(alog_ref[...]) * _softplus(z + dtb_ref[...])
    return jnp.where(lane < 2 * DN_HEADS, _sigmoid(z), g)


def _gdn_prep(direction, qkv, bg, cum_ref, ones_ref, pu_ref, pwq_ref, pk_ref, pe_ref):
    cum_m = cum_ref[...]
    ones_m = ones_ref[...]
    ri = lax.broadcasted_iota(jnp.int32, (GDN_TILE, GDN_TILE), 0)
    ci = lax.broadcasted_iota(jnp.int32, (GDN_TILE, GDN_TILE), 1)
    blk = (ri // CHUNK) == (ci // CHUNK)
    lane_head = ci // HEAD_DIM
    if direction == 0:
        tri_strict, tri_incl = blk & (ri > ci), blk & (ri >= ci)
    else:
        tri_strict, tri_incl = blk & (ri < ci), blk & (ri <= ci)
    q = qkv[:, 0:DN_CH]
    k = qkv[:, DN_CH:2 * DN_CH]
    v = qkv[:, 2 * DN_CH:3 * DN_CH]

    gc = _dot_exact_lhs(cum_m, bg)
    gt = _dot_exact_lhs(ones_m, bg)
    gc_t = gc.T
    row_t = lax.broadcasted_iota(jnp.int32, gc_t.shape, 0)

    beta_cols, gc_cols, gt_cols = [], [], []
    for h in range(DN_HEADS):
        beta_cols.append(_lane_col(bg, direction * DN_HEADS + h))
        gc_cols.append(_lane_col(gc, 2 * DN_HEADS + direction * DN_HEADS + h))
        gt_cols.append(_lane_col(gt, 2 * DN_HEADS + direction * DN_HEADS + h))
    beta_b = _head_bcast(beta_cols, lane_head)
    gc_b = _head_bcast(gc_cols, lane_head)
    gt_b = _head_bcast(gt_cols, lane_head)

    e_gc = jnp.exp(gc_b)
    kb = k * beta_b
    vb = v * beta_b
    kbe = kb * e_gc
    qe = q * e_gc
    kd = k * jnp.exp(gt_b - gc_b)
    pe_ref[...] = jnp.exp(gt_b)
    pk_ref[0] = kd.T.astype(BF16)
    pwq_ref[1] = qe.astype(BF16)

    k16 = k.astype(BF16)
    rhs_uw = jnp.concatenate([vb, kbe], axis=1).astype(BF16)
    eye = (ri == ci).astype(F32)
    yield

    heads = range(DN_HEADS)
    neg_a = []
    for h in heads:
        in_h = lane_head == h
        r = jnp.sum(jnp.where(row_t == 2 * DN_HEADS + direction * DN_HEADS + h, gc_t, 0.0),
                    axis=0, keepdims=True)
        diff = gc_cols[h] - r
        dec_i = jnp.exp(jnp.where(tri_incl, diff, NEG_BIG))
        dec_s = jnp.where(tri_strict, dec_i, 0.0)
        kk = _dot_nt(jnp.where(in_h, kb, 0.0).astype(BF16), k16)
        qk = _dot_nt(jnp.where(in_h, q, 0.0).astype(BF16), k16)
        neg_a.append(-(kk * dec_s))
        pk_ref[1 + h] = (qk * dec_i).astype(BF16)
        yield

    t = [eye + jnp.where((ri // 2) == (ci // 2), neg_a[h], 0.0) for h in heads]
    b = 2
    while b < CHUNK:
        lvl = ((ri // (2 * b)) == (ci // (2 * b))) & ((ri // b) != (ci // b))
        t16 = [t[h].astype(BF16) for h in heads]
        g = [_dot(jnp.where(lvl, neg_a[h], 0.0).astype(BF16), t16[h]) for h in heads]
        yield
        t = [t[h] + _dot(t16[h], g[h].astype(BF16)) for h in heads]
        yield
        b *= 2
    uw = [_dot(t[h].astype(BF16), rhs_uw) for h in heads]
    u_all, w_all = uw[DN_HEADS - 1][:, :DN_CH], uw[DN_HEADS - 1][:, DN_CH:]
    for h in range(DN_HEADS - 2, -1, -1):
        u_all = jnp.where(lane_head <= h, uw[h][:, :DN_CH], u_all)
        w_all = jnp.where(lane_head <= h, uw[h][:, DN_CH:], w_all)
    pu_ref[...] = u_all
    pwq_ref[0] = w_all.astype(BF16)


def _gdn_scan(direction, s_ref, pu_ref, pwq_ref, pk_ref, pe_ref, o_ref, o_rows):
    ri = lax.broadcasted_iota(jnp.int32, (GDN_TILE, GDN_TILE), 0)
    ci = lax.broadcasted_iota(jnp.int32, (GDN_TILE, GDN_TILE), 1)
    blk = (ri // HEAD_DIM) == (ci // HEAD_DIM)
    order = range(CHUNKS_PER_TILE) if direction == 0 else range(CHUNKS_PER_TILE - 1, -1, -1)
    outs = [None] * CHUNKS_PER_TILE
    zeros_c = jnp.zeros((CHUNK, DN_CH), BF16)
    lane_head_c = lax.broadcasted_iota(jnp.int32, (CHUNK, DN_CH), 1) // HEAD_DIM
    for c in order:
        rows = pl.ds(c * CHUNK, CHUNK)
        s = s_ref[...]
        wq = jnp.concatenate([pwq_ref[0, rows, :], pwq_ref[1, rows, :]], axis=0)
        ws = _dot(wq, s.astype(BF16))
        yield
        v_new = (pu_ref[rows, :] - ws[:CHUNK]).astype(BF16)
        v_tile = jnp.concatenate([v_new if i == c else zeros_c for i in range(CHUNKS_PER_TILE)],
                                 axis=0)
        lhs = jnp.concatenate([pk_ref[0]] + [pk_ref[1 + h, rows, :] for h in range(DN_HEADS)],
                              axis=0)
        r2 = _dot(lhs, v_tile)
        s_ref[...] = s * pe_ref[pl.ds(c * CHUNK, 1), :] + jnp.where(blk, r2[:GDN_TILE], 0.0)
        o = ws[CHUNK:]
        for h in range(DN_HEADS):
            o = o + jnp.where(lane_head_c == h,
                              r2[GDN_TILE + h * CHUNK:GDN_TILE + (h + 1) * CHUNK], 0.0)
        outs[c] = o
        yield
    o_ref[o_rows, :] = jnp.concatenate(outs, axis=0)


def _trace_interleaved(stages):
    live = [[g, p] for g, p in stages]
    r = 0
    while live:
        for item in list(live):
            if r % item[1] == 0 and next(item[0], StopIteration) is StopIteration:
                live.remove(item)
        r += 1


def _gdn_kernel(n_lat, write_ctx, dc_ref, dl_ref, gc_ref, gl_ref, zc_ref, zl_ref, cw_ref,
                alog_ref, dtb_ref, on_ref, cum_f_ref, cum_b_ref, ones_ref,
                ol_ref, oc_ref, qkv_ref, bg_ref, of_ref, ob_ref, sf_ref, sb_ref,
                pu_ref, pwq_ref, pk_ref, pe_ref):
    T = GDN_TILE
    zero_row = jnp.zeros((1, CONV_CH), F32)

    qkv_ref[0:T] = _conv_silu_norm(dc_ref[0], zero_row, zero_row, cw_ref)
    bg_ref[0:T] = _gates_to_beta_g(gc_ref[0], alog_ref, dtb_ref)

    def conv_stages(tiles):
        for t in tiles:
            r0 = (t - 1) * T
            prev_row = zero_row if t == 1 else dl_ref[0, pl.ds(r0 - 1, 1), :]
            next_row = zero_row if t == n_lat else dl_ref[0, pl.ds(r0 + T, 1), :]
            qkv_ref[pl.ds(t * T, T), :] = _conv_silu_norm(dl_ref[0, pl.ds(r0, T), :], prev_row,
                                                          next_row, cw_ref)
            bg_ref[pl.ds(t * T, T), :] = _gates_to_beta_g(gl_ref[0, pl.ds(r0, T), :], alog_ref,
                                                          dtb_ref)
            yield

    sf_ref[...] = jnp.zeros_like(sf_ref)
    sb_ref[...] = jnp.zeros_like(sb_ref)

    s_refs = (sf_ref, sb_ref)
    cum_refs = (cum_f_ref, cum_b_ref)
    o_refs = (of_ref, ob_ref)

    def tile_rows(direction, s):
        if isinstance(s, int):
            t = s if (direction == 0 or s == 0) else n_lat + 1 - s
            return pl.ds(t * T, T)
        t = s if direction == 0 else jnp.where(s == 0, 0, n_lat + 1 - s)
        return pl.ds(pl.multiple_of(t * T, T), T)

    def prep_stages(s, slot):
        return [(_gdn_prep(d, qkv_ref[tile_rows(d, s), :], bg_ref[tile_rows(d, s), :], cum_refs[d],
                           ones_ref, pu_ref.at[slot, d], pwq_ref.at[slot, d], pk_ref.at[slot, d],
                           pe_ref.at[slot, d]), 1) for d in range(2)]

    def scan_stages(s, slot):
        return [(_gdn_scan(d, s_refs[d], pu_ref.at[slot, d], pwq_ref.at[slot, d],
                           pk_ref.at[slot, d], pe_ref.at[slot, d], o_refs[d], tile_rows(d, s)),
                 SCAN_STAGE_PERIOD) for d in range(2)]

    _trace_interleaved(prep_stages(0, 0)
                       + [(conv_stages(range(1, n_lat + 1)), CONV_STAGE_PERIOD)])

    def scan_pair(j, carry):
        s = 2 * j
        _trace_interleaved(scan_stages(s, 0) + prep_stages(s + 1, 1))
        _trace_interleaved(scan_stages(s + 1, 1) + prep_stages(s + 2, 0))
        return carry

    lax.fori_loop(0, n_lat // 2, scan_pair, 0)

    def finish(o, z):
        ms = _seg64_sum(o * o, 2) * (1.0 / HEAD_DIM)
        return (o * lax.rsqrt(ms + EPS) * on_ref[...] * _silu(z)).astype(BF16)

    def finish_stages(tiles):
        for t in tiles:
            if t == 0:
                if write_ctx:
                    oc_ref[0] = finish(of_ref[0:T] + ob_ref[0:T], zc_ref[0])
                else:
                    oc_ref[0] = jnp.zeros(oc_ref.shape[1:], BF16)
            else:
                rows = pl.ds(t * T, T)
                ol_ref[0, pl.ds((t - 1) * T, T), :] = finish(of_ref[rows, :] + ob_ref[rows, :],
                                                             zl_ref[0, pl.ds((t - 1) * T, T), :])
            yield

    last_tiles = sorted({1, n_lat})
    _trace_interleaved(scan_stages(n_lat, 0)
                       + [(finish_stages([t for t in range(n_lat + 1) if t not in last_tiles]), 1)])
    _trace_interleaved([(finish_stages(last_tiles), 1)])


def _gdn(d_ctx, d_lat, g_ctx, g_lat, z_ctx, z_lat, conv_w, alog_row, dtb_row, on_row, consts,
         write_ctx):
    B, L, _ = d_lat.shape
    Lc = d_ctx.shape[1]
    assert Lc == GDN_TILE and L % (2 * GDN_TILE) == 0
    n_lat = L // GDN_TILE
    T = GDN_TILE
    cum_f, cum_b, ones_m = consts
    per_b = lambda n, c: pl.BlockSpec((1, n, c), lambda b: (b, 0, 0))
    single = lambda n, c: pl.BlockSpec((1, n, c), lambda b: (b, 0, 0), pipeline_mode=pl.Buffered(1))
    full = lambda a: pl.BlockSpec(a.shape, lambda b: (0,) * a.ndim)
    tot = L + Lc
    return pl.pallas_call(
        functools.partial(_gdn_kernel, n_lat, write_ctx),
        out_shape=(jax.ShapeDtypeStruct((B, L, DN_CH), BF16),
                   jax.ShapeDtypeStruct((B, Lc, DN_CH), BF16)),
        grid=(B,),
        in_specs=[per_b(Lc, CONV_CH), per_b(L, CONV_CH), per_b(Lc, LANES), single(L, LANES),
                  per_b(Lc, DN_CH), single(L, DN_CH), full(conv_w), full(alog_row), full(dtb_row),
                  full(on_row), full(cum_f), full(cum_b), full(ones_m)],
        out_specs=(per_b(L, DN_CH), per_b(Lc, DN_CH)),
        scratch_shapes=[pltpu.VMEM((tot, CONV_CH), F32), pltpu.VMEM((tot, LANES), F32),
                        pltpu.VMEM((tot, DN_CH), F32), pltpu.VMEM((tot, DN_CH), F32),
                        pltpu.VMEM((T, DN_CH), F32), pltpu.VMEM((T, DN_CH), F32),
                        pltpu.VMEM((2, 2, T, DN_CH), F32), pltpu.VMEM((2, 2, 2, T, DN_CH), BF16),
                        pltpu.VMEM((2, 2, 1 + DN_HEADS, T, T), BF16), pltpu.VMEM((2, 2, T, DN_CH), F32)],
        compiler_params=pltpu.CompilerParams(
            dimension_semantics=("arbitrary",), vmem_limit_bytes=VMEM_LIMIT),
        name="gdn",
    )(d_ctx, d_lat, g_ctx, g_lat, z_ctx, z_lat, conv_w, alog_row, dtb_row, on_row,
      cum_f, cum_b, ones_m)


def _mix_mlp_kernel(ff_tile, x_ref, fy_ref, at_ref, dn_ref, g1_ref, sc_ref, sh_ref, g2_ref, ng_ref,
                    wof_ref, woa_ref, wod_ref, w1_ref, w2_ref, o_ref):
    mix = (_dot(fy_ref[0], wof_ref[...]) + _dot(at_ref[0], woa_ref[...])
           + _dot(dn_ref[0], wod_ref[...]))
    x1 = x_ref[0] + g1_ref[0] * mix
    ms = jnp.mean(x1 * x1, axis=-1, keepdims=True)
    gain = ng_ref[...] * (1.0 + sc_ref[0])
    h = (x1 * lax.rsqrt(ms + EPS) * gain + sh_ref[0]).astype(BF16)
    acc = jnp.zeros_like(x1)
    for c in range(w1_ref.shape[1] // ff_tile):
        u = jnp.maximum(_dot(h, w1_ref[:, c * ff_tile:(c + 1) * ff_tile]), 0.0)
        acc = acc + _dot((u * u).astype(BF16), w2_ref[c * ff_tile:(c + 1) * ff_tile, :])
    o_ref[0] = x1 + g2_ref[0] * acc


def _mix_mlp(x, fy, att, dn, g1, sc, sh, g2, norm_g, wts, tm):
    B, L, D = x.shape
    per_batch = g1.shape[0] > 1
    mod_map = (lambda b, i: (b, 0, 0)) if per_batch else (lambda b, i: (0, 0, 0))
    mod = pl.BlockSpec((1, 1, D), mod_map)
    full = lambda a: pl.BlockSpec(a.shape, lambda b, i: (0,) * a.ndim,
                                  pipeline_mode=pl.Buffered(1))
    row = lambda n: pl.BlockSpec((1, tm, n), lambda b, i: (b, i, 0))
    wof, woa, wod, w1, w2 = wts
    return pl.pallas_call(
        functools.partial(_mix_mlp_kernel, 1024),
        out_shape=jax.ShapeDtypeStruct((B, L, D), F32),
        grid=(B, L // tm),
        in_specs=[row(D), row(FT_CH), row(ATT_CH), row(DN_CH), mod, mod, mod, mod,
                  pl.BlockSpec(norm_g.shape, lambda b, i: (0, 0)),
                  full(wof), full(woa), full(wod), full(w1), full(w2)],
        out_specs=row(D),
        compiler_params=pltpu.CompilerParams(
            dimension_semantics=("arbitrary", "arbitrary"), vmem_limit_bytes=VMEM_LIMIT),
        name="mix_mlp",
    )(x, fy, att, dn, g1, sc, sh, g2, norm_g, wof, woa, wod, w1, w2)


def _rope_tables(S):
    rows = S // GRID_W
    t_row = jnp.repeat(jnp.arange(rows), GRID_W).astype(F32)
    t_col = jnp.tile(jnp.arange(GRID_W), rows).astype(F32)
    inv_freq = ROPE_THETA ** (-jnp.arange(ROT_PAIRS, dtype=F32) * 2.0 / AXIS_DIM)
    ang_r = t_row[:, None] * inv_freq
    ang_c = t_col[:, None] * inv_freq
    ang = jnp.concatenate([ang_r, ang_r, ang_c, ang_c], axis=-1)
    cos, sin = jnp.cos(ang), jnp.sin(ang)
    sign = jnp.where((jnp.arange(HEAD_DIM) & ROT_PAIRS) == 0, -1.0, 1.0).astype(F32)
    tile2 = lambda a: jnp.concatenate([a, a], axis=-1)
    return tile2(cos), tile2(sin * sign)


def _cos_sin(rows, cols, period):
    k = (jnp.arange(rows, dtype=jnp.int32)[:, None] * jnp.arange(cols, dtype=jnp.int32)[None, :]) % period
    ang = k.astype(F32) * np.float32(2.0 * np.pi / period)
    return jnp.cos(ang), jnp.sin(ang)


def _dft_tables(n):
    if n <= HEAD_DIM:
        return _cos_sin(n, n, n)
    assert n % HEAD_DIM == 0
    ca, sa = _cos_sin(n, n // HEAD_DIM, n // HEAD_DIM)
    cb, sb = _cos_sin(n, HEAD_DIM, n)
    cos = ca[:, :, None] * cb[:, None, :] - sa[:, :, None] * sb[:, None, :]
    sin = sa[:, :, None] * cb[:, None, :] + ca[:, :, None] * sb[:, None, :]
    return cos.reshape(n, n), sin.reshape(n, n)


def _channel_dft():
    c, s = _dft_tables(HEAD_DIM)
    eye = jnp.eye(FT_GROUPS, dtype=F32)
    return jnp.concatenate([jnp.kron(eye, c), jnp.kron(eye, s)], axis=1).astype(BF16)


def _gdn_consts():
    i = np.arange(GDN_TILE)
    blk = (i[:, None] // CHUNK) == (i[None, :] // CHUNK)
    cum_f = (blk & (i[:, None] >= i[None, :])).astype(np.float32)
    cum_b = (blk & (i[:, None] <= i[None, :])).astype(np.float32)
    return (jnp.asarray(cum_f, BF16), jnp.asarray(cum_b, BF16), jnp.asarray(blk.astype(np.float32), BF16))


def _pad_lanes(a):
    flat = a.reshape(1, -1).astype(F32)
    return jnp.pad(flat, ((0, 0), (0, LANES - flat.shape[1])))


def kernel(x, c, ctx, c_ctx, norm1_g, norm2_g, w_mod, b_mod, w_in, conv_w, q_norm_g, k_norm_g,
           a_log, dt_bias, o_norm_g, w_out, w_ff1, w_ff2):
    B, S, D = x.shape
    Lc = ctx.shape[1]
    depth = w_mod.shape[0]

    rows = ((B + 1 + 7) // 8) * 8
    c_all = jnp.concatenate([c, c_ctx[None, :], jnp.zeros((rows - B - 1, D), F32)], axis=0)
    mod_all = _modulation(c_all, w_mod, b_mod)

    cos_t, sin_t = _rope_tables(S)
    dft_ch = _channel_dft()
    dft_lat = tuple(t.astype(BF16) for t in _dft_tables(S))
    dft_ctx = tuple(t.astype(BF16) for t in _dft_tables(Lc))
    gdn_consts = _gdn_consts()

    o0 = FT_CH
    o1 = o0 + ATT_CH
    o2 = o1 + 2 * KV_CH
    o3 = o2 + CONV_CH
    o4 = o3 + DN_CH

    x_lat, x_ctx = x, ctx
    for l in range(depth):
        last = l == depth - 1
        w = w_in[l]
        wg = jnp.pad(w[:, o4:], ((0, 0), (0, LANES - N_GATES)))
        in_wts = (w[:, :o0].astype(BF16), dft_ch, w[:, o0:o1].astype(BF16), w[:, o1:o2].astype(BF16),
                  w[:, o2:o3].astype(BF16), w[:, o3:o4].astype(BF16), wg.astype(BF16))
        qg = jnp.tile(q_norm_g[l], ATT_HQ).reshape(1, ATT_CH)
        kg = jnp.tile(k_norm_g[l], ATT_HKV).reshape(1, KV_CH)
        n1 = norm1_g[l].reshape(1, D)
        n2 = norm2_g[l].reshape(1, D)
        wo = w_out[l].astype(BF16)
        out_wts = (wo[:FT_CH], wo[FT_CH:FT_CH + ATT_CH], wo[FT_CH + ATT_CH:],
                   w_ff1[l].astype(BF16), w_ff2[l].astype(BF16))
        alog_row = jnp.pad(a_log[l].reshape(1, -1), ((0, 0), (2 * DN_HEADS, LANES - N_GATES)))
        dtb_row = jnp.pad(dt_bias[l].reshape(1, -1), ((0, 0), (2 * DN_HEADS, LANES - N_GATES)))
        on_row = jnp.tile(o_norm_g[l], DN_HEADS).reshape(1, DN_CH)

        mod = mod_all[l, :B].reshape(B, 1, 6 * D)
        modc = mod_all[l, B:B + 1].reshape(1, 1, 6 * D)
        sh1, sc1, g1, sh2, sc2, g2 = [mod[:, :, i * D:(i + 1) * D] for i in range(6)]
        csh1, csc1, cg1, csh2, csc2, cg2 = [modc[:, :, i * D:(i + 1) * D] for i in range(6)]

        fl, ql, kl, vl, dl, zl, gl = _project(x_lat, sc1, sh1, n1, in_wts, qg, kg, cos_t, sin_t,
                                              True, 512)
        fc, qc, kc, vc, dc, zc, gc = _project(x_ctx, csc1, csh1, n1, in_wts, qg, kg,
                                              cos_t[:Lc], sin_t[:Lc], False, Lc)

        dn_lat, dn_ctx = _gdn(dc, dl, gc, gl, zc, zl, conv_w[l], alog_row, dtb_row, on_row,
                              gdn_consts, not last)
        att_lat = _attention(ql, [(kl, vl), (kc, vc)], 256)
        fy_lat = _fourier(fl, dft_lat[0], dft_lat[1], min(S, 1024))
        x_lat = _mix_mlp(x_lat, fy_lat, att_lat, dn_lat, g1, sc2, sh2, g2, n2, out_wts, 512)
        if not last:
            att_ctx = _attention(qc, [(kc, vc)], 128)
            fy_ctx = _fourier(fc, dft_ctx[0], dft_ctx[1], Lc)
            x_ctx = _mix_mlp(x_ctx, fy_ctx, att_ctx, dn_ctx, cg1, csc2, csh2, cg2, n2, out_wts, Lc)
    return x_lat
```

```python
import functools

import jax
import jax.numpy as jnp
import numpy as np
from jax import lax
from jax.experimental import pallas as pl
from jax.experimental.pallas import tpu as pltpu

HEAD_DIM = 64
FT_GROUPS = 4
FT_CH = FT_GROUPS * HEAD_DIM
ATT_HQ = 8
ATT_HKV = 2
ATT_GROUP = ATT_HQ // ATT_HKV
ATT_CH = ATT_HQ * HEAD_DIM
KV_CH = ATT_HKV * HEAD_DIM
DN_HEADS = 4
DN_CH = DN_HEADS * HEAD_DIM
CONV_CH = 3 * DN_CH
N_GATES = 4 * DN_HEADS
GRID_W = 64
ROPE_THETA = 10000.0
AXIS_DIM = HEAD_DIM // 2
ROT_PAIRS = AXIS_DIM // 2
EPS = 1e-6
CHUNK = 64

LANES = 128
SUBLANES = 8
HALO_ROWS = SUBLANES
MXU_DIM = 256
V7X_VMEM_BYTES = 64 * 1024 * 1024
VMEM_LIMIT = V7X_VMEM_BYTES - 8 * 1024 * 1024

GDN_TILE = 256
CHUNKS_PER_TILE = GDN_TILE // CHUNK
NEG_BIG = -1e30
SCAN_STAGE_PERIOD = 1

F32 = jnp.float32
BF16 = jnp.bfloat16


def _dot(a, b):
    return jnp.dot(a, b, preferred_element_type=F32)


def _dot_nt(a, b):
    return lax.dot_general(a, b, (((1,), (1,)), ((), ())), preferred_element_type=F32)


def _split3(x):
    hi = x.astype(BF16)
    r1 = x - hi.astype(F32)
    mid = r1.astype(BF16)
    lo = (r1 - mid.astype(F32)).astype(BF16)
    return hi, mid, lo


def _dot_exact_lhs(m_bf16, x):
    hi, mid, lo = _split3(x)
    return _dot(m_bf16, hi) + _dot(m_bf16, mid) + _dot(m_bf16, lo)


def _seg64_sum(x, passes):
    w = x.shape[1]
    blk = min(w, MXU_DIM)
    r = lax.broadcasted_iota(jnp.int32, (blk, blk), 0) // HEAD_DIM
    c = lax.broadcasted_iota(jnp.int32, (blk, blk), 1) // HEAD_DIM
    ones = jnp.where(r == c, 1.0, 0.0).astype(BF16)
    outs = []
    for j in range(w // blk):
        xs = x[:, j * blk:(j + 1) * blk]
        hi = xs.astype(BF16)
        acc = _dot(hi, ones)
        if passes == 2:
            acc = acc + _dot((xs - hi.astype(F32)).astype(BF16), ones)
        outs.append(acc)
    return outs[0] if len(outs) == 1 else jnp.concatenate(outs, axis=1)


def _sigmoid(x):
    return 1.0 / (1.0 + jnp.exp(-x))


def _silu(x):
    return x * _sigmoid(x)


def _softplus(x):
    return jnp.maximum(x, 0.0) + jnp.log1p(jnp.exp(-jnp.abs(x)))


def _mod_kernel(c_ref, w_ref, b_ref, o_ref):
    a = _silu(c_ref[...])
    o_ref[0] = jnp.dot(a, w_ref[0], preferred_element_type=F32,
                       precision=lax.Precision.HIGHEST) + b_ref[0]


def _modulation(c_all, w_mod, b_mod):
    depth, d, n = w_mod.shape
    rows = c_all.shape[0]
    tn = 1024
    return pl.pallas_call(
        _mod_kernel,
        out_shape=jax.ShapeDtypeStruct((depth, rows, n), F32),
        grid=(depth, n // tn),
        in_specs=[pl.BlockSpec((rows, d), lambda l, j: (0, 0)),
                  pl.BlockSpec((1, d, tn), lambda l, j: (l, 0, j)),
                  pl.BlockSpec((1, 1, tn), lambda l, j: (l, 0, j))],
        out_specs=pl.BlockSpec((1, rows, tn), lambda l, j: (l, 0, j)),
        compiler_params=pltpu.CompilerParams(
            dimension_semantics=("arbitrary", "arbitrary"), vmem_limit_bytes=VMEM_LIMIT),
        name="modulation",
    )(c_all, w_mod, b_mod.reshape(depth, 1, n))


def _head_rms_rope(z, gain, cos, sin_signed, scale):
    ms = _seg64_sum(z * z, 1) * (1.0 / HEAD_DIM)
    y = z * lax.rsqrt(ms + EPS) * gain
    if scale != 1.0:
        y = y * scale
    if cos is None:
        return y
    lane = lax.broadcasted_iota(jnp.int32, cos.shape, 1)
    slabs = []
    for j in range(z.shape[1] // LANES):
        ys = y[:, j * LANES:(j + 1) * LANES]
        partner = jnp.where((lane & ROT_PAIRS) == 0,
                            pltpu.roll(ys, LANES - ROT_PAIRS, axis=1),
                            pltpu.roll(ys, ROT_PAIRS, axis=1))
        slabs.append(ys * cos + partner * sin_signed)
    return slabs[0] if len(slabs) == 1 else jnp.concatenate(slabs, axis=1)


def _project_kernel(use_rope, x_ref, xp_ref, xn_ref, sc_ref, sh_ref, g_ref, wf_ref, dft_ref, wq_ref,
                    wkv_ref, wd_ref, wz_ref, wg_ref, qg_ref, kg_ref, cos_ref, sin_ref, cw_ref,
                    alog_ref, dtb_ref, f_ref, q_ref, k_ref, v_ref, d_ref, z_ref, bg_ref):
    gain = g_ref[...] * (1.0 + sc_ref[0])

    def modulated(x):
        ms = jnp.mean(x * x, axis=-1, keepdims=True)
        return (x * lax.rsqrt(ms + EPS) * gain + sh_ref[0]).astype(BF16)

    h = modulated(x_ref[0])
    cos = cos_ref[...] if use_rope else None
    sin = sin_ref[...] if use_rope else None

    i = pl.program_id(1)
    zd_halo = _dot(modulated(jnp.concatenate([xp_ref[0], xn_ref[0]], axis=0)), wd_ref[...])
    halo_row = lax.broadcasted_iota(jnp.int32, zd_halo.shape, 0)
    prev_row = jnp.sum(jnp.where(halo_row == HALO_ROWS - 1, zd_halo, 0.0), axis=0, keepdims=True)
    next_row = jnp.sum(jnp.where(halo_row == HALO_ROWS, zd_halo, 0.0), axis=0, keepdims=True)
    prev_row = jnp.where(i == 0, 0.0, prev_row)
    next_row = jnp.where(i == pl.num_programs(1) - 1, 0.0, next_row)

    zq = _dot(h, wq_ref[...])
    zkv = _dot(h, wkv_ref[...])
    zd = _dot(h, wd_ref[...])
    f = _dot(h, wf_ref[...]).astype(BF16)
    q_ref[0] = _head_rms_rope(zq, qg_ref[...], cos, sin, HEAD_DIM ** -0.5).astype(BF16)
    z_ref[0] = _dot(h, wz_ref[...])
    k_ref[0] = _head_rms_rope(zkv[:, :KV_CH], kg_ref[...], cos, sin, 1.0).astype(BF16)
    v_ref[0] = zkv[:, KV_CH:].astype(BF16)
    f_ref[0] = _dot(f, dft_ref[...]).astype(BF16)
    d_ref[0] = _conv_silu_norm(zd, prev_row, next_row, cw_ref)
    bg_ref[0] = _gates_to_beta_g(_dot(h, wg_ref[...]), alog_ref, dtb_ref)


def _project(x, sc, sh, norm_g, wts, qg, kg, cos_t, sin_t, conv_w, alog_row, dtb_row, use_rope, tm):
    B, L, D = x.shape
    per_batch = sc.shape[0] > 1
    mod_map = (lambda b, i: (b, 0, 0)) if per_batch else (lambda b, i: (0, 0, 0))
    full = lambda a: pl.BlockSpec(a.shape, lambda b, i: (0,) * a.ndim)
    row = lambda n: pl.BlockSpec((1, tm, n), lambda b, i: (b, i, 0))
    per = tm // HALO_ROWS
    last = L // HALO_ROWS - 1
    halo_prev = pl.BlockSpec((1, HALO_ROWS, D), lambda b, i: (b, jnp.maximum(i * per - 1, 0), 0))
    halo_next = pl.BlockSpec((1, HALO_ROWS, D), lambda b, i: (b, jnp.minimum((i + 1) * per, last), 0))
    wf, dft, wq, wkv, wd, wz, wg = wts
    out_shape = (jax.ShapeDtypeStruct((B, L, 2 * FT_CH), BF16),
                 jax.ShapeDtypeStruct((B, L, ATT_CH), BF16),
                 jax.ShapeDtypeStruct((B, L, KV_CH), BF16),
                 jax.ShapeDtypeStruct((B, L, KV_CH), BF16),
                 jax.ShapeDtypeStruct((B, L, CONV_CH), F32),
                 jax.ShapeDtypeStruct((B, L, DN_CH), F32),
                 jax.ShapeDtypeStruct((B, L, LANES), F32))
    return pl.pallas_call(
        functools.partial(_project_kernel, use_rope),
        out_shape=out_shape,
        grid=(B, L // tm),
        in_specs=[row(D), halo_prev, halo_next,
                  pl.BlockSpec((1, 1, D), mod_map), pl.BlockSpec((1, 1, D), mod_map),
                  full(norm_g), full(wf), full(dft), full(wq), full(wkv), full(wd), full(wz),
                  full(wg), full(qg), full(kg),
                  pl.BlockSpec((tm, LANES), lambda b, i: (i, 0)),
                  pl.BlockSpec((tm, LANES), lambda b, i: (i, 0)),
                  full(conv_w), full(alog_row), full(dtb_row)],
        out_specs=(row(2 * FT_CH), row(ATT_CH), row(KV_CH), row(KV_CH), row(CONV_CH),
                   row(DN_CH), row(LANES)),
        compiler_params=pltpu.CompilerParams(
            dimension_semantics=("arbitrary", "arbitrary"), vmem_limit_bytes=VMEM_LIMIT),
        name="project_rope" if use_rope else "project",
    )(x, x, x, sc, sh, norm_g, wf, dft, wq, wkv, wd, wz, wg, qg, kg, cos_t, sin_t,
      conv_w, alog_row, dtb_row)


def _fourier_kernel(scale, dc_ref, ds_ref, f_ref, o_ref):
    fcs = f_ref[0]
    y = _dot(dc_ref[...], fcs[:, :FT_CH]) - _dot(ds_ref[...], fcs[:, FT_CH:])
    o_ref[0] = (y * scale).astype(BF16)


def _fourier(fcs, dft_c, dft_s, tn):
    B, L, _ = fcs.shape
    scale = float(1.0 / np.sqrt(L * HEAD_DIM))
    return pl.pallas_call(
        functools.partial(_fourier_kernel, scale),
        out_shape=jax.ShapeDtypeStruct((B, L, FT_CH), BF16),
        grid=(L // tn, B),
        in_specs=[pl.BlockSpec((tn, L), lambda n, b: (n, 0)),
                  pl.BlockSpec((tn, L), lambda n, b: (n, 0)),
                  pl.BlockSpec((1, L, 2 * FT_CH), lambda n, b: (b, 0, 0))],
        out_specs=pl.BlockSpec((1, tn, FT_CH), lambda n, b: (b, n, 0)),
        compiler_params=pltpu.CompilerParams(
            dimension_semantics=("arbitrary", "arbitrary"), vmem_limit_bytes=VMEM_LIMIT),
        name="fourier",
    )(dft_c, dft_s, fcs)


def _attention_kernel(n_src, q_ref, *refs):
    kv_refs = refs[:2 * n_src]
    o_ref = refs[2 * n_src]
    tq = q_ref.shape[1]
    kv_heads = range(ATT_HKV)
    scores, row_max = [], []
    for h in kv_heads:
        q = q_ref[0, :, h * ATT_GROUP * HEAD_DIM:(h + 1) * ATT_GROUP * HEAD_DIM]
        q4 = jnp.concatenate([q[:, g * HEAD_DIM:(g + 1) * HEAD_DIM] for g in range(ATT_GROUP)],
                             axis=0)
        sc_h, m = [], None
        for s in range(n_src):
            k = kv_refs[2 * s][0, :, h * HEAD_DIM:(h + 1) * HEAD_DIM]
            sc = _dot_nt(q4, k)
            sc_h.append(sc)
            ms = jnp.max(sc, axis=-1, keepdims=True)
            m = ms if m is None else jnp.maximum(m, ms)
        scores.append(sc_h)
        row_max.append(m)
    accs = []
    for h in kv_heads:
        acc = None
        for s in range(n_src):
            v = kv_refs[2 * s + 1][0]
            lane = lax.broadcasted_iota(jnp.int32, v.shape, 1)
            in_head = (lane >= h * HEAD_DIM) & (lane < (h + 1) * HEAD_DIM)
            v_aug = jnp.where(in_head, v, jnp.ones_like(v))
            p = jnp.exp(scores[h][s] - row_max[h]).astype(BF16)
            pv = _dot(p, v_aug)
            acc = pv if acc is None else acc + pv
        accs.append(acc)
    for h in kv_heads:
        o = accs[h][:, h * HEAD_DIM:(h + 1) * HEAD_DIM]
        den = accs[h][:, (1 - h) * HEAD_DIM:(2 - h) * HEAD_DIM]
        o = (o / den).astype(BF16)
        for g in range(ATT_GROUP):
            c0 = (h * ATT_GROUP + g) * HEAD_DIM
            o_ref[0, :, c0:c0 + HEAD_DIM] = o[g * tq:(g + 1) * tq]


def _attention(q, kv_sources, tq):
    B, L, _ = q.shape
    n_src = len(kv_sources)
    in_specs = [pl.BlockSpec((1, tq, ATT_CH), lambda b, i: (b, i, 0))]
    args = [q]
    for k, v in kv_sources:
        lk = k.shape[1]
        in_specs += [pl.BlockSpec((1, lk, KV_CH), lambda b, i: (b, 0, 0)),
                     pl.BlockSpec((1, lk, KV_CH), lambda b, i: (b, 0, 0))]
        args += [k, v]
    return pl.pallas_call(
        functools.partial(_attention_kernel, n_src),
        out_shape=jax.ShapeDtypeStruct((B, L, ATT_CH), BF16),
        grid=(B, L // tq),
        in_specs=in_specs,
        out_specs=pl.BlockSpec((1, tq, ATT_CH), lambda b, i: (b, i, 0)),
        compiler_params=pltpu.CompilerParams(
            dimension_semantics=("arbitrary", "arbitrary"), vmem_limit_bytes=VMEM_LIMIT),
        name="attention",
    )(*args)


def _lane_col(x, c):
    lane = lax.broadcasted_iota(jnp.int32, x.shape, 1)
    return jnp.sum(jnp.where(lane == c, x, 0.0), axis=1, keepdims=True)


def _head_bcast(cols, lane_head):
    out = cols[DN_HEADS - 1]
    for h in range(DN_HEADS - 2, -1, -1):
        out = jnp.where(lane_head <= h, cols[h], out)
    return out


def _conv_silu_norm(x, prev_row, next_row, w_ref):
    n = x.shape[0]
    row = lax.broadcasted_iota(jnp.int32, x.shape, 0)
    x_m1 = jnp.where(row == 0, prev_row, pltpu.roll(x, 1, axis=0))
    x_p1 = jnp.where(row == n - 1, next_row, pltpu.roll(x, n - 1, axis=0))
    y = _silu(x_m1 * w_ref[0:1, :] + x * w_ref[1:2, :] + x_p1 * w_ref[2:3, :])
    qk = y[:, :2 * DN_CH]
    qk = qk * lax.rsqrt(_seg64_sum(qk * qk, 2) + EPS)
    return jnp.concatenate([qk[:, :DN_CH] * (HEAD_DIM ** -0.5), qk[:, DN_CH:], y[:, 2 * DN_CH:]],
                           axis=1)


def _gates_to_beta_g(z, alog_ref, dtb_ref):
    lane = lax.broadcasted_iota(jnp.int32, z.shape, 1)
    g = -jnp.exp(alog_ref[...]) * _softplus(z + dtb_ref[...])
    return jnp.where(lane < 2 * DN_HEADS, _sigmoid(z), g)


def _gdn_prep(direction, qkv, bg, cum_ref, ones_ref, pu_ref, pwq_ref, pk_ref, pe_ref):
    cum_m = cum_ref[...]
    ones_m = ones_ref[...]
    ri = lax.broadcasted_iota(jnp.int32, (GDN_TILE, GDN_TILE), 0)
    ci = lax.broadcasted_iota(jnp.int32, (GDN_TILE, GDN_TILE), 1)
    blk = (ri // CHUNK) == (ci // CHUNK)
    lane_head = ci // HEAD_DIM
    if direction == 0:
        tri_strict, tri_incl = blk & (ri > ci), blk & (ri >= ci)
    else:
        tri_strict, tri_incl = blk & (ri < ci), blk & (ri <= ci)
    q = qkv[:, 0:DN_CH]
    k = qkv[:, DN_CH:2 * DN_CH]
    v = qkv[:, 2 * DN_CH:3 * DN_CH]

    gc = _dot_exact_lhs(cum_m, bg)
    gt = _dot_exact_lhs(ones_m, bg)
    gc_t = gc.T
    row_t = lax.broadcasted_iota(jnp.int32, gc_t.shape, 0)

    beta_cols, gc_cols, gt_cols = [], [], []
    for h in range(DN_HEADS):
        beta_cols.append(_lane_col(bg, direction * DN_HEADS + h))
        gc_cols.append(_lane_col(gc, 2 * DN_HEADS + direction * DN_HEADS + h))
        gt_cols.append(_lane_col(gt, 2 * DN_HEADS + direction * DN_HEADS + h))
    beta_b = _head_bcast(beta_cols, lane_head)
    gc_b = _head_bcast(gc_cols, lane_head)
    gt_b = _head_bcast(gt_cols, lane_head)

    e_gc = jnp.exp(gc_b)
    kb = k * beta_b
    vb = v * beta_b
    kbe = kb * e_gc
    qe = q * e_gc
    kd = k * jnp.exp(gt_b - gc_b)
    pe_ref[...] = jnp.exp(gt_b)
    pk_ref[0] = kd.T.astype(BF16)
    pwq_ref[1] = qe.astype(BF16)

    k16 = k.astype(BF16)
    rhs_uw = jnp.concatenate([vb, kbe], axis=1).astype(BF16)
    eye = (ri == ci).astype(F32)
    yield

    heads = range(DN_HEADS)
    neg_a = []
    for h in heads:
        in_h = lane_head == h
        r = jnp.sum(jnp.where(row_t == 2 * DN_HEADS + direction * DN_HEADS + h, gc_t, 0.0),
                    axis=0, keepdims=True)
        diff = gc_cols[h] - r
        dec_i = jnp.exp(jnp.where(tri_incl, diff, NEG_BIG))
        dec_s = jnp.where(tri_strict, dec_i, 0.0)
        kk = _dot_nt(jnp.where(in_h, kb, 0.0).astype(BF16), k16)
        qk = _dot_nt(jnp.where(in_h, q, 0.0).astype(BF16), k16)
        neg_a.append(-(kk * dec_s))
        pk_ref[1 + h] = (qk * dec_i).astype(BF16)
        yield

    t = [eye + jnp.where((ri // 2) == (ci // 2), neg_a[h], 0.0) for h in heads]
    b = 2
    while b < CHUNK:
        lvl = ((ri // (2 * b)) == (ci // (2 * b))) & ((ri // b) != (ci // b))
        t16 = [t[h].astype(BF16) for h in heads]
        g = [_dot(jnp.where(lvl, neg_a[h], 0.0).astype(BF16), t16[h]) for h in heads]
        yield
        t = [t[h] + _dot(t16[h], g[h].astype(BF16)) for h in heads]
        yield
        b *= 2
    uw = [_dot(t[h].astype(BF16), rhs_uw) for h in heads]
    u_all, w_all = uw[DN_HEADS - 1][:, :DN_CH], uw[DN_HEADS - 1][:, DN_CH:]
    for h in range(DN_HEADS - 2, -1, -1):
        u_all = jnp.where(lane_head <= h, uw[h][:, :DN_CH], u_all)
        w_all = jnp.where(lane_head <= h, uw[h][:, DN_CH:], w_all)
    pu_ref[...] = u_all
    pwq_ref[0] = w_all.astype(BF16)


def _gdn_scan(direction, s_ref, pu_ref, pwq_ref, pk_ref, pe_ref, o_ref, o_rows):
    ri = lax.broadcasted_iota(jnp.int32, (GDN_TILE, GDN_TILE), 0)
    ci = lax.broadcasted_iota(jnp.int32, (GDN_TILE, GDN_TILE), 1)
    blk = (ri // HEAD_DIM) == (ci // HEAD_DIM)
    order = range(CHUNKS_PER_TILE) if direction == 0 else range(CHUNKS_PER_TILE - 1, -1, -1)
    outs = [None] * CHUNKS_PER_TILE
    zeros_c = jnp.zeros((CHUNK, DN_CH), BF16)
    lane_head_c = lax.broadcasted_iota(jnp.int32, (CHUNK, DN_CH), 1) // HEAD_DIM
    for c in order:
        rows = pl.ds(c * CHUNK, CHUNK)
        s = s_ref[...]
        wq = jnp.concatenate([pwq_ref[0, rows, :], pwq_ref[1, rows, :]], axis=0)
        ws = _dot(wq, s.astype(BF16))
        yield
        v_new = (pu_ref[rows, :] - ws[:CHUNK]).astype(BF16)
        v_tile = jnp.concatenate([v_new if i == c else zeros_c for i in range(CHUNKS_PER_TILE)],
                                 axis=0)
        lhs = jnp.concatenate([pk_ref[0]] + [pk_ref[1 + h, rows, :] for h in range(DN_HEADS)],
                              axis=0)
        r2 = _dot(lhs, v_tile)
        s_ref[...] = s * pe_ref[pl.ds(c * CHUNK, 1), :] + jnp.where(blk, r2[:GDN_TILE], 0.0)
        o = ws[CHUNK:]
        for h in range(DN_HEADS):
            o = o + jnp.where(lane_head_c == h,
                              r2[GDN_TILE + h * CHUNK:GDN_TILE + (h + 1) * CHUNK], 0.0)
        outs[c] = o
        yield
    o_ref[o_rows, :] = jnp.concatenate(outs, axis=0)


def _trace_interleaved(stages):
    live = [[g, p] for g, p in stages]
    r = 0
    while live:
        for item in list(live):
            if r % item[1] == 0 and next(item[0], StopIteration) is StopIteration:
                live.remove(item)
        r += 1


def _gdn_kernel(n_lat, write_ctx, dc_ref, dl_ref, gc_ref, gl_ref, zc_ref, zl_ref, on_ref,
                cum_f_ref, cum_b_ref, ones_ref, ol_ref, oc_ref, of_ref, ob_ref, sf_ref, sb_ref,
                pu_ref, pwq_ref, pk_ref, pe_ref):
    T = GDN_TILE
    sf_ref[...] = jnp.zeros_like(sf_ref)
    sb_ref[...] = jnp.zeros_like(sb_ref)

    s_refs = (sf_ref, sb_ref)
    cum_refs = (cum_f_ref, cum_b_ref)
    o_refs = (of_ref, ob_ref)

    def tile_of(direction, s):
        if isinstance(s, int):
            return s if (direction == 0 or s == 0) else n_lat + 1 - s
        return s if direction == 0 else jnp.where(s == 0, 0, n_lat + 1 - s)

    def tile_rows(direction, s):
        t = tile_of(direction, s)
        return pl.ds(t * T, T) if isinstance(t, int) else pl.ds(pl.multiple_of(t * T, T), T)

    def tile_inputs(direction, s):
        if isinstance(s, int) and s == 0:
            return dc_ref[0], gc_ref[0]
        t = tile_of(direction, s)
        rows = (pl.ds((t - 1) * T, T) if isinstance(t, int)
                else pl.ds(pl.multiple_of((t - 1) * T, T), T))
        return dl_ref[0, rows, :], gl_ref[0, rows, :]

    def prep_stages(s, slot):
        return [(_gdn_prep(d, *tile_inputs(d, s), cum_refs[d], ones_ref, pu_ref.at[slot, d],
                           pwq_ref.at[slot, d], pk_ref.at[slot, d], pe_ref.at[slot, d]), 1)
                for d in range(2)]

    def scan_stages(s, slot):
        return [(_gdn_scan(d, s_refs[d], pu_ref.at[slot, d], pwq_ref.at[slot, d],
                           pk_ref.at[slot, d], pe_ref.at[slot, d], o_refs[d], tile_rows(d, s)),
                 SCAN_STAGE_PERIOD) for d in range(2)]

    _trace_interleaved(prep_stages(0, 0))

    def scan_pair(j, carry):
        s = 2 * j
        _trace_interleaved(scan_stages(s, 0) + prep_stages(s + 1, 1))
        _trace_interleaved(scan_stages(s + 1, 1) + prep_stages(s + 2, 0))
        return carry

    lax.fori_loop(0, n_lat // 2, scan_pair, 0)

    def finish(o, z):
        ms = _seg64_sum(o * o, 2) * (1.0 / HEAD_DIM)
        return (o * lax.rsqrt(ms + EPS) * on_ref[...] * _silu(z)).astype(BF16)

    def finish_stages(tiles):
        for t in tiles:
            if t == 0:
                if write_ctx:
                    oc_ref[0] = finish(of_ref[0:T] + ob_ref[0:T], zc_ref[0])
                else:
                    oc_ref[0] = jnp.zeros(oc_ref.shape[1:], BF16)
            else:
                rows = pl.ds(t * T, T)
                ol_ref[0, pl.ds((t - 1) * T, T), :] = finish(of_ref[rows, :] + ob_ref[rows, :],
                                                             zl_ref[0, pl.ds((t - 1) * T, T), :])
            yield

    last_tiles = sorted({1, n_lat})
    _trace_interleaved(scan_stages(n_lat, 0)
                       + [(finish_stages([t for t in range(n_lat + 1) if t not in last_tiles]), 1)])
    _trace_interleaved([(finish_stages(last_tiles), 1)])


def _gdn(d_ctx, d_lat, g_ctx, g_lat, z_ctx, z_lat, on_row, consts, write_ctx):
    B, L, _ = d_lat.shape
    Lc = d_ctx.shape[1]
    assert Lc == GDN_TILE and L % (2 * GDN_TILE) == 0
    n_lat = L // GDN_TILE
    T = GDN_TILE
    cum_f, cum_b, ones_m = consts
    per_b = lambda n, c: pl.BlockSpec((1, n, c), lambda b: (b, 0, 0))
    full = lambda a: pl.BlockSpec(a.shape, lambda b: (0,) * a.ndim)
    tot = L + Lc
    return pl.pallas_call(
        functools.partial(_gdn_kernel, n_lat, write_ctx),
        out_shape=(jax.ShapeDtypeStruct((B, L, DN_CH), BF16),
                   jax.ShapeDtypeStruct((B, Lc, DN_CH), BF16)),
        grid=(B,),
        in_specs=[per_b(Lc, CONV_CH), per_b(L, CONV_CH), per_b(Lc, LANES), per_b(L, LANES),
                  per_b(Lc, DN_CH), per_b(L, DN_CH), full(on_row), full(cum_f), full(cum_b),
                  full(ones_m)],
        out_specs=(per_b(L, DN_CH), per_b(Lc, DN_CH)),
        scratch_shapes=[pltpu.VMEM((tot, DN_CH), F32), pltpu.VMEM((tot, DN_CH), F32),
                        pltpu.VMEM((T, DN_CH), F32), pltpu.VMEM((T, DN_CH), F32),
                        pltpu.VMEM((2, 2, T, DN_CH), F32), pltpu.VMEM((2, 2, 2, T, DN_CH), BF16),
                        pltpu.VMEM((2, 2, 1 + DN_HEADS, T, T), BF16), pltpu.VMEM((2, 2, T, DN_CH), F32)],
        compiler_params=pltpu.CompilerParams(
            dimension_semantics=("arbitrary",), vmem_limit_bytes=VMEM_LIMIT),
        name="gdn",
    )(d_ctx, d_lat, g_ctx, g_lat, z_ctx, z_lat, on_row, cum_f, cum_b, ones_m)


def _mix_mlp_kernel(ff_tile, x_ref, fy_ref, at_ref, dn_ref, g1_ref, sc_ref, sh_ref, g2_ref, ng_ref,
                    wof_ref, woa_ref, wod_ref, w1_ref, w2_ref, o_ref):
    mix = (_dot(fy_ref[0], wof_ref[...]) + _dot(at_ref[0], woa_ref[...])
           + _dot(dn_ref[0], wod_ref[...]))
    x1 = x_ref[0] + g1_ref[0] * mix
    ms = jnp.mean(x1 * x1, axis=-1, keepdims=True)
    gain = ng_ref[...] * (1.0 + sc_ref[0])
    h = (x1 * lax.rsqrt(ms + EPS) * gain + sh_ref[0]).astype(BF16)
    acc = jnp.zeros_like(x1)
    for c in range(w1_ref.shape[1] // ff_tile):
        u = jnp.maximum(_dot(h, w1_ref[:, c * ff_tile:(c + 1) * ff_tile]), 0.0)
        acc = acc + _dot((u * u).astype(BF16), w2_ref[c * ff_tile:(c + 1) * ff_tile, :])
    o_ref[0] = x1 + g2_ref[0] * acc


def _mix_mlp(x, fy, att, dn, g1, sc, sh, g2, norm_g, wts, tm):
    B, L, D = x.shape
    per_batch = g1.shape[0] > 1
    mod_map = (lambda b, i: (b, 0, 0)) if per_batch else (lambda b, i: (0, 0, 0))
    mod = pl.BlockSpec((1, 1, D), mod_map)
    full = lambda a: pl.BlockSpec(a.shape, lambda b, i: (0,) * a.ndim,
                                  pipeline_mode=pl.Buffered(1))
    row = lambda n: pl.BlockSpec((1, tm, n), lambda b, i: (b, i, 0))
    wof, woa, wod, w1, w2 = wts
    return pl.pallas_call(
        functools.partial(_mix_mlp_kernel, 1024),
        out_shape=jax.ShapeDtypeStruct((B, L, D), F32),
        grid=(B, L // tm),
        in_specs=[row(D), row(FT_CH), row(ATT_CH), row(DN_CH), mod, mod, mod, mod,
                  pl.BlockSpec(norm_g.shape, lambda b, i: (0, 0)),
                  full(wof), full(woa), full(wod), full(w1), full(w2)],
        out_specs=row(D),
        compiler_params=pltpu.CompilerParams(
            dimension_semantics=("arbitrary", "arbitrary"), vmem_limit_bytes=VMEM_LIMIT),
        name="mix_mlp",
    )(x, fy, att, dn, g1, sc, sh, g2, norm_g, wof, woa, wod, w1, w2)


def _rope_tables(S):
    rows = S // GRID_W
    t_row = jnp.repeat(jnp.arange(rows), GRID_W).astype(F32)
    t_col = jnp.tile(jnp.arange(GRID_W), rows).astype(F32)
    inv_freq = ROPE_THETA ** (-jnp.arange(ROT_PAIRS, dtype=F32) * 2.0 / AXIS_DIM)
    ang_r = t_row[:, None] * inv_freq
    ang_c = t_col[:, None] * inv_freq
    ang = jnp.concatenate([ang_r, ang_r, ang_c, ang_c], axis=-1)
    cos, sin = jnp.cos(ang), jnp.sin(ang)
    sign = jnp.where((jnp.arange(HEAD_DIM) & ROT_PAIRS) == 0, -1.0, 1.0).astype(F32)
    tile2 = lambda a: jnp.concatenate([a, a], axis=-1)
    return tile2(cos), tile2(sin * sign)


def _cos_sin(rows, cols, period):
    k = (jnp.arange(rows, dtype=jnp.int32)[:, None] * jnp.arange(cols, dtype=jnp.int32)[None, :]) % period
    ang = k.astype(F32) * np.float32(2.0 * np.pi / period)
    return jnp.cos(ang), jnp.sin(ang)


def _dft_tables(n):
    if n <= HEAD_DIM:
        return _cos_sin(n, n, n)
    assert n % HEAD_DIM == 0
    ca, sa = _cos_sin(n, n // HEAD_DIM, n // HEAD_DIM)
    cb, sb = _cos_sin(n, HEAD_DIM, n)
    cos = ca[:, :, None] * cb[:, None, :] - sa[:, :, None] * sb[:, None, :]
    sin = sa[:, :, None] * cb[:, None, :] + ca[:, :, None] * sb[:, None, :]
    return cos.reshape(n, n), sin.reshape(n, n)


def _channel_dft():
    c, s = _dft_tables(HEAD_DIM)
    eye = jnp.eye(FT_GROUPS, dtype=F32)
    return jnp.concatenate([jnp.kron(eye, c), jnp.kron(eye, s)], axis=1).astype(BF16)


def _gdn_consts():
    i = np.arange(GDN_TILE)
    blk = (i[:, None] // CHUNK) == (i[None, :] // CHUNK)
    cum_f = (blk & (i[:, None] >= i[None, :])).astype(np.float32)
    cum_b = (blk & (i[:, None] <= i[None, :])).astype(np.float32)
    return (jnp.asarray(cum_f, BF16), jnp.asarray(cum_b, BF16), jnp.asarray(blk.astype(np.float32), BF16))


def _pad_lanes(a):
    flat = a.reshape(1, -1).astype(F32)
    return jnp.pad(flat, ((0, 0), (0, LANES - flat.shape[1])))


def kernel(x, c, ctx, c_ctx, norm1_g, norm2_g, w_mod, b_mod, w_in, conv_w, q_norm_g, k_norm_g,
           a_log, dt_bias, o_norm_g, w_out, w_ff1, w_ff2):
    B, S, D = x.shape
    Lc = ctx.shape[1]
    depth = w_mod.shape[0]

    rows = ((B + 1 + 7) // 8) * 8
    c_all = jnp.concatenate([c, c_ctx[None, :], jnp.zeros((rows - B - 1, D), F32)], axis=0)
    mod_all = _modulation(c_all, w_mod, b_mod)

    cos_t, sin_t = _rope_tables(S)
    dft_ch = _channel_dft()
    dft_lat = tuple(t.astype(BF16) for t in _dft_tables(S))
    dft_ctx = tuple(t.astype(BF16) for t in _dft_tables(Lc))
    gdn_consts = _gdn_consts()

    o0 = FT_CH
    o1 = o0 + ATT_CH
    o2 = o1 + 2 * KV_CH
    o3 = o2 + CONV_CH
    o4 = o3 + DN_CH

    x_lat, x_ctx = x, ctx
    for l in range(depth):
        last = l == depth - 1
        w = w_in[l]
        wg = jnp.pad(w[:, o4:], ((0, 0), (0, LANES - N_GATES)))
        in_wts = (w[:, :o0].astype(BF16), dft_ch, w[:, o0:o1].astype(BF16), w[:, o1:o2].astype(BF16),
                  w[:, o2:o3].astype(BF16), w[:, o3:o4].astype(BF16), wg.astype(BF16))
        qg = jnp.tile(q_norm_g[l], ATT_HQ).reshape(1, ATT_CH)
        kg = jnp.tile(k_norm_g[l], ATT_HKV).reshape(1, KV_CH)
        n1 = norm1_g[l].reshape(1, D)
        n2 = norm2_g[l].reshape(1, D)
        wo = w_out[l].astype(BF16)
        out_wts = (wo[:FT_CH], wo[FT_CH:FT_CH + ATT_CH], wo[FT_CH + ATT_CH:],
                   w_ff1[l].astype(BF16), w_ff2[l].astype(BF16))
        alog_row = jnp.pad(a_log[l].reshape(1, -1), ((0, 0), (2 * DN_HEADS, LANES - N_GATES)))
        dtb_row = jnp.pad(dt_bias[l].reshape(1, -1), ((0, 0), (2 * DN_HEADS, LANES - N_GATES)))
        on_row = jnp.tile(o_norm_g[l], DN_HEADS).reshape(1, DN_CH)

        mod = mod_all[l, :B].reshape(B, 1, 6 * D)
        modc = mod_all[l, B:B + 1].reshape(1, 1, 6 * D)
        sh1, sc1, g1, sh2, sc2, g2 = [mod[:, :, i * D:(i + 1) * D] for i in range(6)]
        csh1, csc1, cg1, csh2, csc2, cg2 = [modc[:, :, i * D:(i + 1) * D] for i in range(6)]

        fl, ql, kl, vl, dl, zl, gl = _project(x_lat, sc1, sh1, n1, in_wts, qg, kg, cos_t, sin_t,
                                              conv_w[l], alog_row, dtb_row, True, min(S, 1024))
        fc, qc, kc, vc, dc, zc, gc = _project(x_ctx, csc1, csh1, n1, in_wts, qg, kg,
                                              cos_t[:Lc], sin_t[:Lc], conv_w[l], alog_row, dtb_row,
                                              False, Lc)

        dn_lat, dn_ctx = _gdn(dc, dl, gc, gl, zc, zl, on_row, gdn_consts, not last)
        att_lat = _attention(ql, [(kl, vl), (kc, vc)], 256)
        fy_lat = _fourier(fl, dft_lat[0], dft_lat[1], min(S, 1024))
        x_lat = _mix_mlp(x_lat, fy_lat, att_lat, dn_lat, g1, sc2, sh2, g2, n2, out_wts,
                         min(S, 1024))
        if not last:
            att_ctx = _attention(qc, [(kc, vc)], 128)
            fy_ctx = _fourier(fc, dft_ctx[0], dft_ctx[1], Lc)
            x_ctx = _mix_mlp(x_ctx, fy_ctx, att_ctx, dn_ctx, cg1, csc2, csh2, cg2, n2, out_wts, Lc)
    return x_lat
```

```python
import functools

import jax
import jax.numpy as jnp
import numpy as np
from jax import lax
from jax.experimental import pallas as pl
from jax.experimental.pallas import tpu as pltpu

HEAD_DIM = 64
FT_GROUPS = 4
FT_CH = FT_GROUPS * HEAD_DIM
ATT_HQ = 8
ATT_HKV = 2
ATT_GROUP = ATT_HQ // ATT_HKV
ATT_CH = ATT_HQ * HEAD_DIM
KV_CH = ATT_HKV * HEAD_DIM
DN_HEADS = 4
DN_CH = DN_HEADS * HEAD_DIM
CONV_CH = 3 * DN_CH
N_GATES = 4 * DN_HEADS
GRID_W = 64
ROPE_THETA = 10000.0
AXIS_DIM = HEAD_DIM // 2
ROT_PAIRS = AXIS_DIM // 2
EPS = 1e-6
CHUNK = 64

LANES = 128
SUBLANES = 8
HALO_ROWS = SUBLANES
MXU_DIM = 256
V7X_VMEM_BYTES = 64 * 1024 * 1024
VMEM_LIMIT = V7X_VMEM_BYTES - 8 * 1024 * 1024

GDN_TILE = 256
CHUNKS_PER_TILE = GDN_TILE // CHUNK
NEG_BIG = -1e30
SCAN_STAGE_PERIOD = 1

F32 = jnp.float32
BF16 = jnp.bfloat16


def _dot(a, b):
    return jnp.dot(a, b, preferred_element_type=F32)


def _dot_nt(a, b):
    return lax.dot_general(a, b, (((1,), (1,)), ((), ())), preferred_element_type=F32)


def _split3(x):
    hi = x.astype(BF16)
    r1 = x - hi.astype(F32)
    mid = r1.astype(BF16)
    lo = (r1 - mid.astype(F32)).astype(BF16)
    return hi, mid, lo


def _dot_exact_lhs(m_bf16, x):
    hi, mid, lo = _split3(x)
    return _dot(m_bf16, hi) + _dot(m_bf16, mid) + _dot(m_bf16, lo)


def _seg64_sum(x, passes):
    w = x.shape[1]
    blk = min(w, MXU_DIM)
    r = lax.broadcasted_iota(jnp.int32, (blk, blk), 0) // HEAD_DIM
    c = lax.broadcasted_iota(jnp.int32, (blk, blk), 1) // HEAD_DIM
    ones = jnp.where(r == c, 1.0, 0.0).astype(BF16)
    outs = []
    for j in range(w // blk):
        xs = x[:, j * blk:(j + 1) * blk]
        hi = xs.astype(BF16)
        acc = _dot(hi, ones)
        if passes == 2:
            acc = acc + _dot((xs - hi.astype(F32)).astype(BF16), ones)
        outs.append(acc)
    return outs[0] if len(outs) == 1 else jnp.concatenate(outs, axis=1)


def _sigmoid(x):
    return 1.0 / (1.0 + jnp.exp(-x))


def _silu(x):
    return x * _sigmoid(x)


def _softplus(x):
    return jnp.maximum(x, 0.0) + jnp.log1p(jnp.exp(-jnp.abs(x)))


def _mod_kernel(c_ref, w_ref, b_ref, o_ref):
    a = _silu(c_ref[...])
    o_ref[0] = jnp.dot(a, w_ref[0], preferred_element_type=F32,
                       precision=lax.Precision.HIGHEST) + b_ref[0]


def _modulation(c_all, w_mod, b_mod):
    depth, d, n = w_mod.shape
    rows = c_all.shape[0]
    tn = 1024
    return pl.pallas_call(
        _mod_kernel,
        out_shape=jax.ShapeDtypeStruct((depth, rows, n), F32),
        grid=(depth, n // tn),
        in_specs=[pl.BlockSpec((rows, d), lambda l, j: (0, 0)),
                  pl.BlockSpec((1, d, tn), lambda l, j: (l, 0, j)),
                  pl.BlockSpec((1, 1, tn), lambda l, j: (l, 0, j))],
        out_specs=pl.BlockSpec((1, rows, tn), lambda l, j: (l, 0, j)),
        compiler_params=pltpu.CompilerParams(
            dimension_semantics=("arbitrary", "arbitrary"), vmem_limit_bytes=VMEM_LIMIT),
        name="modulation",
    )(c_all, w_mod, b_mod.reshape(depth, 1, n))


def _head_rms_rope(z, gain, cos, sin_signed, scale):
    ms = _seg64_sum(z * z, 1) * (1.0 / HEAD_DIM)
    y = z * lax.rsqrt(ms + EPS) * gain
    if scale != 1.0:
        y = y * scale
    if cos is None:
        return y
    lane = lax.broadcasted_iota(jnp.int32, cos.shape, 1)
    slabs = []
    for j in range(z.shape[1] // LANES):
        ys = y[:, j * LANES:(j + 1) * LANES]
        partner = jnp.where((lane & ROT_PAIRS) == 0,
                            pltpu.roll(ys, LANES - ROT_PAIRS, axis=1),
                            pltpu.roll(ys, ROT_PAIRS, axis=1))
        slabs.append(ys * cos + partner * sin_signed)
    return slabs[0] if len(slabs) == 1 else jnp.concatenate(slabs, axis=1)


def _conv_silu_norm(x, prev_row, next_row, w_ref):
    n = x.shape[0]
    row = lax.broadcasted_iota(jnp.int32, x.shape, 0)
    x_m1 = jnp.where(row == 0, prev_row, pltpu.roll(x, 1, axis=0))
    x_p1 = jnp.where(row == n - 1, next_row, pltpu.roll(x, n - 1, axis=0))
    y = _silu(x_m1 * w_ref[0:1, :] + x * w_ref[1:2, :] + x_p1 * w_ref[2:3, :])
    qk = y[:, :2 * DN_CH]
    qk = qk * lax.rsqrt(_seg64_sum(qk * qk, 1) + EPS)
    return jnp.concatenate([qk[:, :DN_CH] * (HEAD_DIM ** -0.5), qk[:, DN_CH:], y[:, 2 * DN_CH:]],
                           axis=1)


def _gates_to_beta_g(z, alog_ref, dtb_ref):
    lane = lax.broadcasted_iota(jnp.int32, z.shape, 1)
    g = -jnp.exp(alog_ref[...]) * _softplus(z + dtb_ref[...])
    return jnp.where(lane < 2 * DN_HEADS, _sigmoid(z), g)


def _project_kernel(use_rope, x_ref, xp_ref, xn_ref, sc_ref, sh_ref, g_ref, wf_ref, dft_ref, wq_ref,
                    wkv_ref, wd_ref, wz_ref, wg_ref, qg_ref, kg_ref, cos_ref, sin_ref, cw_ref,
                    alog_ref, dtb_ref, f_ref, q_ref, k_ref, v_ref, d_ref, z_ref, bg_ref):
    gain = g_ref[...] * (1.0 + sc_ref[0])

    def modulated(x):
        ms = jnp.mean(x * x, axis=-1, keepdims=True)
        return (x * lax.rsqrt(ms + EPS) * gain + sh_ref[0]).astype(BF16)

    h = modulated(x_ref[0])
    cos = cos_ref[...] if use_rope else None
    sin = sin_ref[...] if use_rope else None

    i = pl.program_id(1)
    zd_halo = _dot(modulated(jnp.concatenate([xp_ref[0], xn_ref[0]], axis=0)), wd_ref[...])
    halo_row = lax.broadcasted_iota(jnp.int32, zd_halo.shape, 0)
    prev_row = jnp.sum(jnp.where(halo_row == HALO_ROWS - 1, zd_halo, 0.0), axis=0, keepdims=True)
    next_row = jnp.sum(jnp.where(halo_row == HALO_ROWS, zd_halo, 0.0), axis=0, keepdims=True)
    prev_row = jnp.where(i == 0, 0.0, prev_row)
    next_row = jnp.where(i == pl.num_programs(1) - 1, 0.0, next_row)

    zd = _dot(h, wd_ref[...])
    zq = _dot(h, wq_ref[...])
    d_ref[0] = _conv_silu_norm(zd, prev_row, next_row, cw_ref)
    zkv = _dot(h, wkv_ref[...])
    f = _dot(h, wf_ref[...]).astype(BF16)
    q_ref[0] = _head_rms_rope(zq, qg_ref[...], cos, sin, HEAD_DIM ** -0.5).astype(BF16)
    z_ref[0] = _dot(h, wz_ref[...])
    k_ref[0] = _head_rms_rope(zkv[:, :KV_CH], kg_ref[...], cos, sin, 1.0).astype(BF16)
    v_ref[0] = zkv[:, KV_CH:].astype(BF16)
    bg_ref[0] = _gates_to_beta_g(_dot(h, wg_ref[...]), alog_ref, dtb_ref)
    f_ref[0] = _dot(f, dft_ref[...]).astype(BF16)


def _project(x, sc, sh, norm_g, wts, qg, kg, cos_t, sin_t, conv_w, alog_row, dtb_row, use_rope, tm):
    B, L, D = x.shape
    per_batch = sc.shape[0] > 1
    mod_map = (lambda b, i: (b, 0, 0)) if per_batch else (lambda b, i: (0, 0, 0))
    full = lambda a: pl.BlockSpec(a.shape, lambda b, i: (0,) * a.ndim)
    row = lambda n: pl.BlockSpec((1, tm, n), lambda b, i: (b, i, 0))
    per = tm // HALO_ROWS
    last = L // HALO_ROWS - 1
    halo_prev = pl.BlockSpec((1, HALO_ROWS, D), lambda b, i: (b, jnp.maximum(i * per - 1, 0), 0))
    halo_next = pl.BlockSpec((1, HALO_ROWS, D), lambda b, i: (b, jnp.minimum((i + 1) * per, last), 0))
    wf, dft, wq, wkv, wd, wz, wg = wts
    out_shape = (jax.ShapeDtypeStruct((B, L, 2 * FT_CH), BF16),
                 jax.ShapeDtypeStruct((B, L, ATT_CH), BF16),
                 jax.ShapeDtypeStruct((B, L, KV_CH), BF16),
                 jax.ShapeDtypeStruct((B, L, KV_CH), BF16),
                 jax.ShapeDtypeStruct((B, L, CONV_CH), F32),
                 jax.ShapeDtypeStruct((B, L, DN_CH), F32),
                 jax.ShapeDtypeStruct((B, L, LANES), F32))
    return pl.pallas_call(
        functools.partial(_project_kernel, use_rope),
        out_shape=out_shape,
        grid=(B, L // tm),
        in_specs=[row(D), halo_prev, halo_next,
                  pl.BlockSpec((1, 1, D), mod_map), pl.BlockSpec((1, 1, D), mod_map),
                  full(norm_g), full(wf), full(dft), full(wq), full(wkv), full(wd), full(wz),
                  full(wg), full(qg), full(kg),
                  pl.BlockSpec((tm, LANES), lambda b, i: (i, 0)),
                  pl.BlockSpec((tm, LANES), lambda b, i: (i, 0)),
                  full(conv_w), full(alog_row), full(dtb_row)],
        out_specs=(row(2 * FT_CH), row(ATT_CH), row(KV_CH), row(KV_CH), row(CONV_CH),
                   row(DN_CH), row(LANES)),
        compiler_params=pltpu.CompilerParams(
            dimension_semantics=("arbitrary", "arbitrary"), vmem_limit_bytes=VMEM_LIMIT),
        name="project_rope" if use_rope else "project",
    )(x, x, x, sc, sh, norm_g, wf, dft, wq, wkv, wd, wz, wg, qg, kg, cos_t, sin_t,
      conv_w, alog_row, dtb_row)


def _fourier_kernel(scale, dc_ref, ds_ref, f_ref, o_ref):
    fcs = f_ref[0]
    y = _dot(dc_ref[...], fcs[:, :FT_CH]) - _dot(ds_ref[...], fcs[:, FT_CH:])
    o_ref[0] = (y * scale).astype(BF16)


def _fourier(fcs, dft_c, dft_s, tn):
    B, L, _ = fcs.shape
    scale = float(1.0 / np.sqrt(L * HEAD_DIM))
    return pl.pallas_call(
        functools.partial(_fourier_kernel, scale),
        out_shape=jax.ShapeDtypeStruct((B, L, FT_CH), BF16),
        grid=(L // tn, B),
        in_specs=[pl.BlockSpec((tn, L), lambda n, b: (n, 0)),
                  pl.BlockSpec((tn, L), lambda n, b: (n, 0)),
                  pl.BlockSpec((1, L, 2 * FT_CH), lambda n, b: (b, 0, 0))],
        out_specs=pl.BlockSpec((1, tn, FT_CH), lambda n, b: (b, n, 0)),
        compiler_params=pltpu.CompilerParams(
            dimension_semantics=("arbitrary", "arbitrary"), vmem_limit_bytes=VMEM_LIMIT),
        name="fourier",
    )(dft_c, dft_s, fcs)


def _attention_kernel(n_src, q_ref, *refs):
    kv_refs = refs[:2 * n_src]
    o_ref = refs[2 * n_src]
    tq = q_ref.shape[1]
    kv_heads = range(ATT_HKV)
    scores, row_max = [], []
    for h in kv_heads:
        q = q_ref[0, :, h * ATT_GROUP * HEAD_DIM:(h + 1) * ATT_GROUP * HEAD_DIM]
        q4 = jnp.concatenate([q[:, g * HEAD_DIM:(g + 1) * HEAD_DIM] for g in range(ATT_GROUP)],
                             axis=0)
        sc_h, m = [], None
        for s in range(n_src):
            k = kv_refs[2 * s][0, :, h * HEAD_DIM:(h + 1) * HEAD_DIM]
            sc = _dot_nt(q4, k)
            sc_h.append(sc)
            ms = jnp.max(sc, axis=-1, keepdims=True)
            m = ms if m is None else jnp.maximum(m, ms)
        scores.append(sc_h)
        row_max.append(m)
    accs = []
    for h in kv_heads:
        acc = None
        for s in range(n_src):
            v = kv_refs[2 * s + 1][0]
            lane = lax.broadcasted_iota(jnp.int32, v.shape, 1)
            in_head = (lane >= h * HEAD_DIM) & (lane < (h + 1) * HEAD_DIM)
            v_aug = jnp.where(in_head, v, jnp.ones_like(v))
            p = jnp.exp(scores[h][s] - row_max[h]).astype(BF16)
            pv = _dot(p, v_aug)
            acc = pv if acc is None else acc + pv
        accs.append(acc)
    for h in kv_heads:
        o = accs[h][:, h * HEAD_DIM:(h + 1) * HEAD_DIM]
        den = accs[h][:, (1 - h) * HEAD_DIM:(2 - h) * HEAD_DIM]
        o = (o / den).astype(BF16)
        for g in range(ATT_GROUP):
            c0 = (h * ATT_GROUP + g) * HEAD_DIM
            o_ref[0, :, c0:c0 + HEAD_DIM] = o[g * tq:(g + 1) * tq]


def _attention(q, kv_sources, tq):
    B, L, _ = q.shape
    n_src = len(kv_sources)
    in_specs = [pl.BlockSpec((1, tq, ATT_CH), lambda b, i: (b, i, 0))]
    args = [q]
    for k, v in kv_sources:
        lk = k.shape[1]
        in_specs += [pl.BlockSpec((1, lk, KV_CH), lambda b, i: (b, 0, 0)),
                     pl.BlockSpec((1, lk, KV_CH), lambda b, i: (b, 0, 0))]
        args += [k, v]
    return pl.pallas_call(
        functools.partial(_attention_kernel, n_src),
        out_shape=jax.ShapeDtypeStruct((B, L, ATT_CH), BF16),
        grid=(B, L // tq),
        in_specs=in_specs,
        out_specs=pl.BlockSpec((1, tq, ATT_CH), lambda b, i: (b, i, 0)),
        compiler_params=pltpu.CompilerParams(
            dimension_semantics=("arbitrary", "arbitrary"), vmem_limit_bytes=VMEM_LIMIT),
        name="attention",
    )(*args)


def _lane_col(x, c):
    lane = lax.broadcasted_iota(jnp.int32, x.shape, 1)
    return jnp.sum(jnp.where(lane == c, x, 0.0), axis=1, keepdims=True)


def _head_bcast(cols, lane_head):
    out = cols[DN_HEADS - 1]
    for h in range(DN_HEADS - 2, -1, -1):
        out = jnp.where(lane_head <= h, cols[h], out)
    return out


def _gdn_prep(direction, qkv, bg, cum_ref, ones_ref, pu_ref, pwq_ref, pk_ref, pe_ref):
    cum_m = cum_ref[...]
    ones_m = ones_ref[...]
    ri = lax.broadcasted_iota(jnp.int32, (GDN_TILE, GDN_TILE), 0)
    ci = lax.broadcasted_iota(jnp.int32, (GDN_TILE, GDN_TILE), 1)
    blk = (ri // CHUNK) == (ci // CHUNK)
    lane_head = ci // HEAD_DIM
    if direction == 0:
        tri_strict, tri_incl = blk & (ri > ci), blk & (ri >= ci)
    else:
        tri_strict, tri_incl = blk & (ri < ci), blk & (ri <= ci)
    q = qkv[:, 0:DN_CH]
    k = qkv[:, DN_CH:2 * DN_CH]
    v = qkv[:, 2 * DN_CH:3 * DN_CH]

    gc = _dot_exact_lhs(cum_m, bg)
    gt = _dot_exact_lhs(ones_m, bg)
    gc_t = gc.T
    row_t = lax.broadcasted_iota(jnp.int32, gc_t.shape, 0)

    beta_cols, gc_cols, gt_cols = [], [], []
    for h in range(DN_HEADS):
        beta_cols.append(_lane_col(bg, direction * DN_HEADS + h))
        gc_cols.append(_lane_col(gc, 2 * DN_HEADS + direction * DN_HEADS + h))
        gt_cols.append(_lane_col(gt, 2 * DN_HEADS + direction * DN_HEADS + h))
    beta_b = _head_bcast(beta_cols, lane_head)
    gc_b = _head_bcast(gc_cols, lane_head)
    gt_b = _head_bcast(gt_cols, lane_head)

    e_gc = jnp.exp(gc_b)
    kb = k * beta_b
    vb = v * beta_b
    kbe = kb * e_gc
    qe = q * e_gc
    kd = k * jnp.exp(gt_b - gc_b)
    pe_ref[...] = jnp.exp(gt_b)
    pk_ref[0] = kd.T.astype(BF16)
    pwq_ref[1] = qe.astype(BF16)

    k16 = k.astype(BF16)
    rhs_uw = jnp.concatenate([vb, kbe], axis=1).astype(BF16)
    eye = (ri == ci).astype(F32)
    yield

    heads = range(DN_HEADS)
    neg_a = []
    for h in heads:
        in_h = lane_head == h
        r = jnp.sum(jnp.where(row_t == 2 * DN_HEADS + direction * DN_HEADS + h, gc_t, 0.0),
                    axis=0, keepdims=True)
        diff = gc_cols[h] - r
        dec_i = jnp.exp(jnp.where(tri_incl, diff, NEG_BIG))
        dec_s = jnp.where(tri_strict, dec_i, 0.0)
        kk = _dot_nt(jnp.where(in_h, kb, 0.0).astype(BF16), k16)
        qk = _dot_nt(jnp.where(in_h, q, 0.0).astype(BF16), k16)
        neg_a.append(-(kk * dec_s))
        pk_ref[1 + h] = (qk * dec_i).astype(BF16)
        yield

    t = [eye + jnp.where((ri // 2) == (ci // 2), neg_a[h], 0.0) for h in heads]
    b = 2
    while b < CHUNK:
        lvl = ((ri // (2 * b)) == (ci // (2 * b))) & ((ri // b) != (ci // b))
        t16 = [t[h].astype(BF16) for h in heads]
        g = [_dot(jnp.where(lvl, neg_a[h], 0.0).astype(BF16), t16[h]) for h in heads]
        yield
        t = [t[h] + _dot(t16[h], g[h].astype(BF16)) for h in heads]
        yield
        b *= 2
    uw = [_dot(t[h].astype(BF16), rhs_uw) for h in heads]
    u_all, w_all = uw[DN_HEADS - 1][:, :DN_CH], uw[DN_HEADS - 1][:, DN_CH:]
    for h in range(DN_HEADS - 2, -1, -1):
        u_all = jnp.where(lane_head <= h, uw[h][:, :DN_CH], u_all)
        w_all = jnp.where(lane_head <= h, uw[h][:, DN_CH:], w_all)
    pu_ref[...] = u_all
    pwq_ref[0] = w_all.astype(BF16)


def _gdn_scan(direction, s_ref, pu_ref, pwq_ref, pk_ref, pe_ref, o_ref, o_rows):
    ri = lax.broadcasted_iota(jnp.int32, (GDN_TILE, GDN_TILE), 0)
    ci = lax.broadcasted_iota(jnp.int32, (GDN_TILE, GDN_TILE), 1)
    blk = (ri // HEAD_DIM) == (ci // HEAD_DIM)
    order = range(CHUNKS_PER_TILE) if direction == 0 else range(CHUNKS_PER_TILE - 1, -1, -1)
    outs = [None] * CHUNKS_PER_TILE
    zeros_c = jnp.zeros((CHUNK, DN_CH), BF16)
    lane_head_c = lax.broadcasted_iota(jnp.int32, (CHUNK, DN_CH), 1) // HEAD_DIM
    for c in order:
        rows = pl.ds(c * CHUNK, CHUNK)
        s = s_ref[...]
        wq = jnp.concatenate([pwq_ref[0, rows, :], pwq_ref[1, rows, :]], axis=0)
        ws = _dot(wq, s.astype(BF16))
        yield
        v_new = (pu_ref[rows, :] - ws[:CHUNK]).astype(BF16)
        v_tile = jnp.concatenate([v_new if i == c else zeros_c for i in range(CHUNKS_PER_TILE)],
                                 axis=0)
        lhs = jnp.concatenate([pk_ref[0]] + [pk_ref[1 + h, rows, :] for h in range(DN_HEADS)],
                              axis=0)
        r2 = _dot(lhs, v_tile)
        s_ref[...] = s * pe_ref[pl.ds(c * CHUNK, 1), :] + jnp.where(blk, r2[:GDN_TILE], 0.0)
        o = ws[CHUNK:]
        for h in range(DN_HEADS):
            o = o + jnp.where(lane_head_c == h,
                              r2[GDN_TILE + h * CHUNK:GDN_TILE + (h + 1) * CHUNK], 0.0)
        outs[c] = o
        yield
    o_ref[o_rows, :] = jnp.concatenate(outs, axis=0)


def _trace_interleaved(stages):
    live = [[g, p] for g, p in stages]
    r = 0
    while live:
        for item in list(live):
            if r % item[1] == 0 and next(item[0], StopIteration) is StopIteration:
                live.remove(item)
        r += 1


def _gdn_kernel(n_lat, write_ctx, dc_ref, dl_ref, gc_ref, gl_ref, zc_ref, zl_ref, on_ref,
                cum_f_ref, cum_b_ref, ones_ref, ol_ref, oc_ref, of_ref, ob_ref, sf_ref, sb_ref,
                pu_ref, pwq_ref, pk_ref, pe_ref):
    T = GDN_TILE
    sf_ref[...] = jnp.zeros_like(sf_ref)
    sb_ref[...] = jnp.zeros_like(sb_ref)

    s_refs = (sf_ref, sb_ref)
    cum_refs = (cum_f_ref, cum_b_ref)
    o_refs = (of_ref, ob_ref)

    def tile_of(direction, s):
        if isinstance(s, int):
            return s if (direction == 0 or s == 0) else n_lat + 1 - s
        return s if direction == 0 else jnp.where(s == 0, 0, n_lat + 1 - s)

    def tile_rows(direction, s):
        t = tile_of(direction, s)
        return pl.ds(t * T, T) if isinstance(t, int) else pl.ds(pl.multiple_of(t * T, T), T)

    def tile_inputs(direction, s):
        if isinstance(s, int) and s == 0:
            return dc_ref[0], gc_ref[0]
        t = tile_of(direction, s)
        rows = (pl.ds((t - 1) * T, T) if isinstance(t, int)
                else pl.ds(pl.multiple_of((t - 1) * T, T), T))
        return dl_ref[0, rows, :], gl_ref[0, rows, :]

    def prep_stages(s, slot):
        return [(_gdn_prep(d, *tile_inputs(d, s), cum_refs[d], ones_ref, pu_ref.at[slot, d],
                           pwq_ref.at[slot, d], pk_ref.at[slot, d], pe_ref.at[slot, d]), 1)
                for d in range(2)]

    def scan_stages(s, slot):
        return [(_gdn_scan(d, s_refs[d], pu_ref.at[slot, d], pwq_ref.at[slot, d],
                           pk_ref.at[slot, d], pe_ref.at[slot, d], o_refs[d], tile_rows(d, s)),
                 SCAN_STAGE_PERIOD) for d in range(2)]

    _trace_interleaved(prep_stages(0, 0))

    def scan_pair(j, carry):
        s = 2 * j
        _trace_interleaved(scan_stages(s, 0) + prep_stages(s + 1, 1))
        _trace_interleaved(scan_stages(s + 1, 1) + prep_stages(s + 2, 0))
        return carry

    lax.fori_loop(0, n_lat // 2, scan_pair, 0)

    def finish(o, z):
        ms = _seg64_sum(o * o, 2) * (1.0 / HEAD_DIM)
        return (o * lax.rsqrt(ms + EPS) * on_ref[...] * _silu(z)).astype(BF16)

    def finish_stages(tiles):
        for t in tiles:
            if t == 0:
                if write_ctx:
                    oc_ref[0] = finish(of_ref[0:T] + ob_ref[0:T], zc_ref[0])
                else:
                    oc_ref[0] = jnp.zeros(oc_ref.shape[1:], BF16)
            else:
                rows = pl.ds(t * T, T)
                ol_ref[0, pl.ds((t - 1) * T, T), :] = finish(of_ref[rows, :] + ob_ref[rows, :],
                                                             zl_ref[0, pl.ds((t - 1) * T, T), :])
            yield

    last_tiles = sorted({1, n_lat})
    _trace_interleaved(scan_stages(n_lat, 0)
                       + [(finish_stages([t for t in range(n_lat + 1) if t not in last_tiles]), 1)])
    _trace_interleaved([(finish_stages(last_tiles), 1)])


def _gdn(d_ctx, d_lat, g_ctx, g_lat, z_ctx, z_lat, on_row, consts, write_ctx):
    B, L, _ = d_lat.shape
    Lc = d_ctx.shape[1]
    assert Lc == GDN_TILE and L % (2 * GDN_TILE) == 0
    n_lat = L // GDN_TILE
    T = GDN_TILE
    cum_f, cum_b, ones_m = consts
    per_b = lambda n, c: pl.BlockSpec((1, n, c), lambda b: (b, 0, 0))
    full = lambda a: pl.BlockSpec(a.shape, lambda b: (0,) * a.ndim)
    tot = L + Lc
    return pl.pallas_call(
        functools.partial(_gdn_kernel, n_lat, write_ctx),
        out_shape=(jax.ShapeDtypeStruct((B, L, DN_CH), BF16),
                   jax.ShapeDtypeStruct((B, Lc, DN_CH), BF16)),
        grid=(B,),
        in_specs=[per_b(Lc, CONV_CH), per_b(L, CONV_CH), per_b(Lc, LANES), per_b(L, LANES),
                  per_b(Lc, DN_CH), per_b(L, DN_CH), full(on_row), full(cum_f), full(cum_b),
                  full(ones_m)],
        out_specs=(per_b(L, DN_CH), per_b(Lc, DN_CH)),
        scratch_shapes=[pltpu.VMEM((tot, DN_CH), F32), pltpu.VMEM((tot, DN_CH), F32),
                        pltpu.VMEM((T, DN_CH), F32), pltpu.VMEM((T, DN_CH), F32),
                        pltpu.VMEM((2, 2, T, DN_CH), F32), pltpu.VMEM((2, 2, 2, T, DN_CH), BF16),
                        pltpu.VMEM((2, 2, 1 + DN_HEADS, T, T), BF16), pltpu.VMEM((2, 2, T, DN_CH), F32)],
        compiler_params=pltpu.CompilerParams(
            dimension_semantics=("arbitrary",), vmem_limit_bytes=VMEM_LIMIT),
        name="gdn",
    )(d_ctx, d_lat, g_ctx, g_lat, z_ctx, z_lat, on_row, cum_f, cum_b, ones_m)


def _mix_mlp_kernel(ff_tile, x_ref, fy_ref, at_ref, dn_ref, g1_ref, sc_ref, sh_ref, g2_ref, ng_ref,
                    wof_ref, woa_ref, wod_ref, w1_ref, w2_ref, o_ref):
    mix = (_dot(fy_ref[0], wof_ref[...]) + _dot(at_ref[0], woa_ref[...])
           + _dot(dn_ref[0], wod_ref[...]))
    x1 = x_ref[0] + g1_ref[0] * mix
    ms = jnp.mean(x1 * x1, axis=-1, keepdims=True)
    gain = ng_ref[...] * (1.0 + sc_ref[0])
    h = (x1 * lax.rsqrt(ms + EPS) * gain + sh_ref[0]).astype(BF16)
    acc = jnp.zeros_like(x1)
    for c in range(w1_ref.shape[1] // ff_tile):
        u = jnp.maximum(_dot(h, w1_ref[:, c * ff_tile:(c + 1) * ff_tile]), 0.0)
        acc = acc + _dot((u * u).astype(BF16), w2_ref[c * ff_tile:(c + 1) * ff_tile, :])
    o_ref[0] = x1 + g2_ref[0] * acc


def _mix_mlp(x, fy, att, dn, g1, sc, sh, g2, norm_g, wts, tm):
    B, L, D = x.shape
    per_batch = g1.shape[0] > 1
    mod_map = (lambda b, i: (b, 0, 0)) if per_batch else (lambda b, i: (0, 0, 0))
    mod = pl.BlockSpec((1, 1, D), mod_map)
    full = lambda a: pl.BlockSpec(a.shape, lambda b, i: (0,) * a.ndim,
                                  pipeline_mode=pl.Buffered(1))
    row = lambda n: pl.BlockSpec((1, tm, n), lambda b, i: (b, i, 0))
    wof, woa, wod, w1, w2 = wts
    return pl.pallas_call(
        functools.partial(_mix_mlp_kernel, 1024),
        out_shape=jax.ShapeDtypeStruct((B, L, D), F32),
        grid=(B, L // tm),
        in_specs=[row(D), row(FT_CH), row(ATT_CH), row(DN_CH), mod, mod, mod, mod,
                  pl.BlockSpec(norm_g.shape, lambda b, i: (0, 0)),
                  full(wof), full(woa), full(wod), full(w1), full(w2)],
        out_specs=row(D),
        compiler_params=pltpu.CompilerParams(
            dimension_semantics=("arbitrary", "arbitrary"), vmem_limit_bytes=VMEM_LIMIT),
        name="mix_mlp",
    )(x, fy, att, dn, g1, sc, sh, g2, norm_g, wof, woa, wod, w1, w2)


def _rope_tables(S):
    rows = S // GRID_W
    t_row = jnp.repeat(jnp.arange(rows), GRID_W).astype(F32)
    t_col = jnp.tile(jnp.arange(GRID_W), rows).astype(F32)
    inv_freq = ROPE_THETA ** (-jnp.arange(ROT_PAIRS, dtype=F32) * 2.0 / AXIS_DIM)
    ang_r = t_row[:, None] * inv_freq
    ang_c = t_col[:, None] * inv_freq
    ang = jnp.concatenate([ang_r, ang_r, ang_c, ang_c], axis=-1)
    cos, sin = jnp.cos(ang), jnp.sin(ang)
    sign = jnp.where((jnp.arange(HEAD_DIM) & ROT_PAIRS) == 0, -1.0, 1.0).astype(F32)
    tile2 = lambda a: jnp.concatenate([a, a], axis=-1)
    return tile2(cos), tile2(sin * sign)


def _cos_sin(rows, cols, period):
    k = (jnp.arange(rows, dtype=jnp.int32)[:, None] * jnp.arange(cols, dtype=jnp.int32)[None, :]) % period
    ang = k.astype(F32) * np.float32(2.0 * np.pi / period)
    return jnp.cos(ang), jnp.sin(ang)


def _dft_tables(n):
    if n <= HEAD_DIM:
        return _cos_sin(n, n, n)
    assert n % HEAD_DIM == 0
    ca, sa = _cos_sin(n, n // HEAD_DIM, n // HEAD_DIM)
    cb, sb = _cos_sin(n, HEAD_DIM, n)
    cos = ca[:, :, None] * cb[:, None, :] - sa[:, :, None] * sb[:, None, :]
    sin = sa[:, :, None] * cb[:, None, :] + ca[:, :, None] * sb[:, None, :]
    return cos.reshape(n, n), sin.reshape(n, n)


def _channel_dft():
    c, s = _dft_tables(HEAD_DIM)
    eye = jnp.eye(FT_GROUPS, dtype=F32)
    return jnp.concatenate([jnp.kron(eye, c), jnp.kron(eye, s)], axis=1).astype(BF16)


def _gdn_consts():
    i = np.arange(GDN_TILE)
    blk = (i[:, None] // CHUNK) == (i[None, :] // CHUNK)
    cum_f = (blk & (i[:, None] >= i[None, :])).astype(np.float32)
    cum_b = (blk & (i[:, None] <= i[None, :])).astype(np.float32)
    return (jnp.asarray(cum_f, BF16), jnp.asarray(cum_b, BF16), jnp.asarray(blk.astype(np.float32), BF16))


def _pad_lanes(a):
    flat = a.reshape(1, -1).astype(F32)
    return jnp.pad(flat, ((0, 0), (0, LANES - flat.shape[1])))


def kernel(x, c, ctx, c_ctx, norm1_g, norm2_g, w_mod, b_mod, w_in, conv_w, q_norm_g, k_norm_g,
           a_log, dt_bias, o_norm_g, w_out, w_ff1, w_ff2):
    B, S, D = x.shape
    Lc = ctx.shape[1]
    depth = w_mod.shape[0]

    rows = ((B + 1 + 7) // 8) * 8
    c_all = jnp.concatenate([c, c_ctx[None, :], jnp.zeros((rows - B - 1, D), F32)], axis=0)
    mod_all = _modulation(c_all, w_mod, b_mod)

    cos_t, sin_t = _rope_tables(S)
    dft_ch = _channel_dft()
    dft_lat = tuple(t.astype(BF16) for t in _dft_tables(S))
    dft_ctx = tuple(t.astype(BF16) for t in _dft_tables(Lc))
    gdn_consts = _gdn_consts()

    o0 = FT_CH
    o1 = o0 + ATT_CH
    o2 = o1 + 2 * KV_CH
    o3 = o2 + CONV_CH
    o4 = o3 + DN_CH

    x_lat, x_ctx = x, ctx
    for l in range(depth):
        last = l == depth - 1
        w = w_in[l]
        wg = jnp.pad(w[:, o4:], ((0, 0), (0, LANES - N_GATES)))
        in_wts = (w[:, :o0].astype(BF16), dft_ch, w[:, o0:o1].astype(BF16), w[:, o1:o2].astype(BF16),
                  w[:, o2:o3].astype(BF16), w[:, o3:o4].astype(BF16), wg.astype(BF16))
        qg = jnp.tile(q_norm_g[l], ATT_HQ).reshape(1, ATT_CH)
        kg = jnp.tile(k_norm_g[l], ATT_HKV).reshape(1, KV_CH)
        n1 = norm1_g[l].reshape(1, D)
        n2 = norm2_g[l].reshape(1, D)
        wo = w_out[l].astype(BF16)
        out_wts = (wo[:FT_CH], wo[FT_CH:FT_CH + ATT_CH], wo[FT_CH + ATT_CH:],
                   w_ff1[l].astype(BF16), w_ff2[l].astype(BF16))
        alog_row = jnp.pad(a_log[l].reshape(1, -1), ((0, 0), (2 * DN_HEADS, LANES - N_GATES)))
        dtb_row = jnp.pad(dt_bias[l].reshape(1, -1), ((0, 0), (2 * DN_HEADS, LANES - N_GATES)))
        on_row = jnp.tile(o_norm_g[l], DN_HEADS).reshape(1, DN_CH)

        mod = mod_all[l, :B].reshape(B, 1, 6 * D)
        modc = mod_all[l, B:B + 1].reshape(1, 1, 6 * D)
        sh1, sc1, g1, sh2, sc2, g2 = [mod[:, :, i * D:(i + 1) * D] for i in range(6)]
        csh1, csc1, cg1, csh2, csc2, cg2 = [modc[:, :, i * D:(i + 1) * D] for i in range(6)]

        fl, ql, kl, vl, dl, zl, gl = _project(x_lat, sc1, sh1, n1, in_wts, qg, kg, cos_t, sin_t,
                                              conv_w[l], alog_row, dtb_row, True, min(S, 1024))
        fc, qc, kc, vc, dc, zc, gc = _project(x_ctx, csc1, csh1, n1, in_wts, qg, kg,
                                              cos_t[:Lc], sin_t[:Lc], conv_w[l], alog_row, dtb_row,
                                              False, Lc)

        dn_lat, dn_ctx = _gdn(dc, dl, gc, gl, zc, zl, on_row, gdn_consts, not last)
        att_lat = _attention(ql, [(kl, vl), (kc, vc)], min(S, 256))
        fy_lat = _fourier(fl, dft_lat[0], dft_lat[1], min(S, 1024))
        x_lat = _mix_mlp(x_lat, fy_lat, att_lat, dn_lat, g1, sc2, sh2, g2, n2, out_wts,
                         min(S, 1024))
        if not last:
            att_ctx = _attention(qc, [(kc, vc)], Lc)
            fy_ctx = _fourier(fc, dft_ctx[0], dft_ctx[1], Lc)
            x_ctx = _mix_mlp(x_ctx, fy_ctx, att_ctx, dn_ctx, cg1, csc2, csh2, cg2, n2, out_wts, Lc)
    return x_lat
```

```python
import functools

import jax
import jax.numpy as jnp
import numpy as np
from jax import lax
from jax.experimental import pallas as pl
from jax.experimental.pallas import tpu as pltpu

HEAD_DIM = 64
FT_GROUPS = 4
FT_CH = FT_GROUPS * HEAD_DIM
ATT_HQ = 8
ATT_HKV = 2
ATT_GROUP = ATT_HQ // ATT_HKV
ATT_CH = ATT_HQ * HEAD_DIM
KV_CH = ATT_HKV * HEAD_DIM
DN_HEADS = 4
DN_CH = DN_HEADS * HEAD_DIM
CONV_CH = 3 * DN_CH
N_GATES = 4 * DN_HEADS
GRID_W = 64
ROPE_THETA = 10000.0
AXIS_DIM = HEAD_DIM // 2
ROT_PAIRS = AXIS_DIM // 2
EPS = 1e-6
CHUNK = 64

LANES = 128
SUBLANES = 8
HALO_ROWS = SUBLANES
MXU_DIM = 256
V7X_VMEM_BYTES = 64 * 1024 * 1024
VMEM_LIMIT = V7X_VMEM_BYTES - 8 * 1024 * 1024

GDN_TILE = 256
CHUNKS_PER_TILE = GDN_TILE // CHUNK
NEG_BIG = -1e30
SCAN_STAGE_PERIOD = 1

F32 = jnp.float32
BF16 = jnp.bfloat16


def _dot(a, b):
    return jnp.dot(a, b, preferred_element_type=F32)


def _dot_nt(a, b):
    return lax.dot_general(a, b, (((1,), (1,)), ((), ())), preferred_element_type=F32)


def _split3(x):
    hi = x.astype(BF16)
    r1 = x - hi.astype(F32)
    mid = r1.astype(BF16)
    lo = (r1 - mid.astype(F32)).astype(BF16)
    return hi, mid, lo


def _dot_exact_lhs(m_bf16, x):
    hi, mid, lo = _split3(x)
    return _dot(m_bf16, hi) + _dot(m_bf16, mid) + _dot(m_bf16, lo)


def _seg64_sum(x, passes):
    w = x.shape[1]
    blk = min(w, MXU_DIM)
    r = lax.broadcasted_iota(jnp.int32, (blk, blk), 0) // HEAD_DIM
    c = lax.broadcasted_iota(jnp.int32, (blk, blk), 1) // HEAD_DIM
    ones = jnp.where(r == c, 1.0, 0.0).astype(BF16)
    outs = []
    for j in range(w // blk):
        xs = x[:, j * blk:(j + 1) * blk]
        hi = xs.astype(BF16)
        acc = _dot(hi, ones)
        if passes == 2:
            acc = acc + _dot((xs - hi.astype(F32)).astype(BF16), ones)
        outs.append(acc)
    return outs[0] if len(outs) == 1 else jnp.concatenate(outs, axis=1)


def _sigmoid(x):
    return 1.0 / (1.0 + jnp.exp(-x))


def _silu(x):
    return x * _sigmoid(x)


def _softplus(x):
    return jnp.maximum(x, 0.0) + jnp.log1p(jnp.exp(-jnp.abs(x)))


def _mod_kernel(c_ref, w_ref, b_ref, o_ref):
    a = _silu(c_ref[...])
    o_ref[0] = jnp.dot(a, w_ref[0], preferred_element_type=F32,
                       precision=lax.Precision.HIGHEST) + b_ref[0]


def _modulation(c_all, w_mod, b_mod):
    depth, d, n = w_mod.shape
    rows = c_all.shape[0]
    tn = 1024
    return pl.pallas_call(
        _mod_kernel,
        out_shape=jax.ShapeDtypeStruct((depth, rows, n), F32),
        grid=(depth, n // tn),
        in_specs=[pl.BlockSpec((rows, d), lambda l, j: (0, 0)),
                  pl.BlockSpec((1, d, tn), lambda l, j: (l, 0, j)),
                  pl.BlockSpec((1, 1, tn), lambda l, j: (l, 0, j))],
        out_specs=pl.BlockSpec((1, rows, tn), lambda l, j: (l, 0, j)),
        compiler_params=pltpu.CompilerParams(
            dimension_semantics=("arbitrary", "arbitrary"), vmem_limit_bytes=VMEM_LIMIT),
        name="modulation",
    )(c_all, w_mod, b_mod.reshape(depth, 1, n))


def _head_rms_rope(z, gain, cos, sin_signed, scale):
    ms = _seg64_sum(z * z, 1) * (1.0 / HEAD_DIM)
    y = z * lax.rsqrt(ms + EPS) * gain
    if scale != 1.0:
        y = y * scale
    if cos is None:
        return y
    lane = lax.broadcasted_iota(jnp.int32, cos.shape, 1)
    slabs = []
    for j in range(z.shape[1] // LANES):
        ys = y[:, j * LANES:(j + 1) * LANES]
        partner = jnp.where((lane & ROT_PAIRS) == 0,
                            pltpu.roll(ys, LANES - ROT_PAIRS, axis=1),
                            pltpu.roll(ys, ROT_PAIRS, axis=1))
        slabs.append(ys * cos + partner * sin_signed)
    return slabs[0] if len(slabs) == 1 else jnp.concatenate(slabs, axis=1)


def _conv_silu_norm(x, prev_row, next_row, w_ref):
    n = x.shape[0]
    row = lax.broadcasted_iota(jnp.int32, x.shape, 0)
    x_m1 = jnp.where(row == 0, prev_row, pltpu.roll(x, 1, axis=0))
    x_p1 = jnp.where(row == n - 1, next_row, pltpu.roll(x, n - 1, axis=0))
    y = _silu(x_m1 * w_ref[0:1, :] + x * w_ref[1:2, :] + x_p1 * w_ref[2:3, :])
    qk = y[:, :2 * DN_CH]
    qk = qk * lax.rsqrt(_seg64_sum(qk * qk, 1) + EPS)
    return jnp.concatenate([qk[:, :DN_CH] * (HEAD_DIM ** -0.5), qk[:, DN_CH:], y[:, 2 * DN_CH:]],
                           axis=1)


def _gates_to_beta_g(z, alog_ref, dtb_ref):
    lane = lax.broadcasted_iota(jnp.int32, z.shape, 1)
    g = -jnp.exp(alog_ref[...]) * _softplus(z + dtb_ref[...])
    return jnp.where(lane < 2 * DN_HEADS, _sigmoid(z), g)


def _project_kernel(use_rope, x_ref, xp_ref, xn_ref, sc_ref, sh_ref, g_ref, wf_ref, dft_ref, wq_ref,
                    wkv_ref, wd_ref, wz_ref, wg_ref, qg_ref, kg_ref, cos_ref, sin_ref, cw_ref,
                    alog_ref, dtb_ref, f_ref, q_ref, k_ref, v_ref, d_ref, z_ref, bg_ref):
    gain = g_ref[...] * (1.0 + sc_ref[0])

    def modulated(x):
        ms = jnp.mean(x * x, axis=-1, keepdims=True)
        return (x * lax.rsqrt(ms + EPS) * gain + sh_ref[0]).astype(BF16)

    h = modulated(x_ref[0])
    cos = cos_ref[...] if use_rope else None
    sin = sin_ref[...] if use_rope else None

    i = pl.program_id(1)
    zd_halo = _dot(modulated(jnp.concatenate([xp_ref[0], xn_ref[0]], axis=0)), wd_ref[...])
    halo_row = lax.broadcasted_iota(jnp.int32, zd_halo.shape, 0)
    prev_row = jnp.sum(jnp.where(halo_row == HALO_ROWS - 1, zd_halo, 0.0), axis=0, keepdims=True)
    next_row = jnp.sum(jnp.where(halo_row == HALO_ROWS, zd_halo, 0.0), axis=0, keepdims=True)
    prev_row = jnp.where(i == 0, 0.0, prev_row)
    next_row = jnp.where(i == pl.num_programs(1) - 1, 0.0, next_row)

    zd = _dot(h, wd_ref[...])
    zq = _dot(h, wq_ref[...])
    d_ref[0] = _conv_silu_norm(zd, prev_row, next_row, cw_ref)
    zkv = _dot(h, wkv_ref[...])
    f = _dot(h, wf_ref[...]).astype(BF16)
    q_ref[0] = _head_rms_rope(zq, qg_ref[...], cos, sin, HEAD_DIM ** -0.5).astype(BF16)
    z_ref[0] = _dot(h, wz_ref[...])
    k_ref[0] = _head_rms_rope(zkv[:, :KV_CH], kg_ref[...], cos, sin, 1.0).astype(BF16)
    v_ref[0] = zkv[:, KV_CH:].astype(BF16)
    bg_ref[0] = _gates_to_beta_g(_dot(h, wg_ref[...]), alog_ref, dtb_ref)
    f_ref[0] = _dot(f, dft_ref[...]).astype(BF16)


def _project(x, sc, sh, norm_g, wts, qg, kg, cos_t, sin_t, conv_w, alog_row, dtb_row, use_rope, tm):
    B, L, D = x.shape
    per_batch = sc.shape[0] > 1
    mod_map = (lambda b, i: (b, 0, 0)) if per_batch else (lambda b, i: (0, 0, 0))
    full = lambda a: pl.BlockSpec(a.shape, lambda b, i: (0,) * a.ndim)
    row = lambda n: pl.BlockSpec((1, tm, n), lambda b, i: (b, i, 0))
    per = tm // HALO_ROWS
    last = L // HALO_ROWS - 1
    halo_prev = pl.BlockSpec((1, HALO_ROWS, D), lambda b, i: (b, jnp.maximum(i * per - 1, 0), 0))
    halo_next = pl.BlockSpec((1, HALO_ROWS, D), lambda b, i: (b, jnp.minimum((i + 1) * per, last), 0))
    wf, dft, wq, wkv, wd, wz, wg = wts
    out_shape = (jax.ShapeDtypeStruct((B, L, 2 * FT_CH), BF16),
                 jax.ShapeDtypeStruct((B, L, ATT_CH), BF16),
                 jax.ShapeDtypeStruct((B, L, KV_CH), BF16),
                 jax.ShapeDtypeStruct((B, L, KV_CH), BF16),
                 jax.ShapeDtypeStruct((B, L, CONV_CH), F32),
                 jax.ShapeDtypeStruct((B, L, DN_CH), F32),
                 jax.ShapeDtypeStruct((B, L, LANES), F32))
    return pl.pallas_call(
        functools.partial(_project_kernel, use_rope),
        out_shape=out_shape,
        grid=(B, L // tm),
        in_specs=[row(D), halo_prev, halo_next,
                  pl.BlockSpec((1, 1, D), mod_map), pl.BlockSpec((1, 1, D), mod_map),
                  full(norm_g), full(wf), full(dft), full(wq), full(wkv), full(wd), full(wz),
                  full(wg), full(qg), full(kg),
                  pl.BlockSpec((tm, LANES), lambda b, i: (i, 0)),
                  pl.BlockSpec((tm, LANES), lambda b, i: (i, 0)),
                  full(conv_w), full(alog_row), full(dtb_row)],
        out_specs=(row(2 * FT_CH), row(ATT_CH), row(KV_CH), row(KV_CH), row(CONV_CH),
                   row(DN_CH), row(LANES)),
        compiler_params=pltpu.CompilerParams(
            dimension_semantics=("arbitrary", "arbitrary"), vmem_limit_bytes=VMEM_LIMIT),
        name="project_rope" if use_rope else "project",
    )(x, x, x, sc, sh, norm_g, wf, dft, wq, wkv, wd, wz, wg, qg, kg, cos_t, sin_t,
      conv_w, alog_row, dtb_row)


def _fourier_kernel(scale, dc_ref, ds_ref, f_ref, o_ref):
    fcs = f_ref[0]
    y = _dot(dc_ref[...], fcs[:, :FT_CH]) - _dot(ds_ref[...], fcs[:, FT_CH:])
    o_ref[0] = (y * scale).astype(BF16)


def _fourier(fcs, dft_c, dft_s, tn):
    B, L, _ = fcs.shape
    scale = float(1.0 / np.sqrt(L * HEAD_DIM))
    return pl.pallas_call(
        functools.partial(_fourier_kernel, scale),
        out_shape=jax.ShapeDtypeStruct((B, L, FT_CH), BF16),
        grid=(L // tn, B),
        in_specs=[pl.BlockSpec((tn, L), lambda n, b: (n, 0)),
                  pl.BlockSpec((tn, L), lambda n, b: (n, 0)),
                  pl.BlockSpec((1, L, 2 * FT_CH), lambda n, b: (b, 0, 0))],
        out_specs=pl.BlockSpec((1, tn, FT_CH), lambda n, b: (b, n, 0)),
        compiler_params=pltpu.CompilerParams(
            dimension_semantics=("arbitrary", "arbitrary"), vmem_limit_bytes=VMEM_LIMIT),
        name="fourier",
    )(dft_c, dft_s, fcs)


def _attention_kernel(n_src, q_ref, *refs):
    kv_refs = refs[:2 * n_src]
    o_ref = refs[2 * n_src]
    tq = q_ref.shape[1]
    kv_heads = range(ATT_HKV)
    scores, row_max = [], []
    for h in kv_heads:
        q = q_ref[0, :, h * ATT_GROUP * HEAD_DIM:(h + 1) * ATT_GROUP * HEAD_DIM]
        q4 = jnp.concatenate([q[:, g * HEAD_DIM:(g + 1) * HEAD_DIM] for g in range(ATT_GROUP)],
                             axis=0)
        sc_h, m = [], None
        for s in range(n_src):
            k = kv_refs[2 * s][0, :, h * HEAD_DIM:(h + 1) * HEAD_DIM]
            sc = _dot_nt(q4, k)
            sc_h.append(sc)
            ms = jnp.max(sc, axis=-1, keepdims=True)
            m = ms if m is None else jnp.maximum(m, ms)
        scores.append(sc_h)
        row_max.append(m)
    accs = []
    for h in kv_heads:
        acc = None
        for s in range(n_src):
            v = kv_refs[2 * s + 1][0]
            lane = lax.broadcasted_iota(jnp.int32, v.shape, 1)
            in_head = (lane >= h * HEAD_DIM) & (lane < (h + 1) * HEAD_DIM)
            v_aug = jnp.where(in_head, v, jnp.ones_like(v))
            p = jnp.exp(scores[h][s] - row_max[h]).astype(BF16)
            pv = _dot(p, v_aug)
            acc = pv if acc is None else acc + pv
        accs.append(acc)
    for h in kv_heads:
        o = accs[h][:, h * HEAD_DIM:(h + 1) * HEAD_DIM]
        den = accs[h][:, (1 - h) * HEAD_DIM:(2 - h) * HEAD_DIM]
        o = (o / den).astype(BF16)
        for g in range(ATT_GROUP):
            c0 = (h * ATT_GROUP + g) * HEAD_DIM
            o_ref[0, :, c0:c0 + HEAD_DIM] = o[g * tq:(g + 1) * tq]


def _attention(q, kv_sources, tq):
    B, L, _ = q.shape
    n_src = len(kv_sources)
    in_specs = [pl.BlockSpec((1, tq, ATT_CH), lambda b, i: (b, i, 0))]
    args = [q]
    for k, v in kv_sources:
        lk = k.shape[1]
        in_specs += [pl.BlockSpec((1, lk, KV_CH), lambda b, i: (b, 0, 0)),
                     pl.BlockSpec((1, lk, KV_CH), lambda b, i: (b, 0, 0))]
        args += [k, v]
    return pl.pallas_call(
        functools.partial(_attention_kernel, n_src),
        out_shape=jax.ShapeDtypeStruct((B, L, ATT_CH), BF16),
        grid=(B, L // tq),
        in_specs=in_specs,
        out_specs=pl.BlockSpec((1, tq, ATT_CH), lambda b, i: (b, i, 0)),
        compiler_params=pltpu.CompilerParams(
            dimension_semantics=("arbitrary", "arbitrary"), vmem_limit_bytes=VMEM_LIMIT),
        name="attention",
    )(*args)


def _lane_col(x, c):
    lane = lax.broadcasted_iota(jnp.int32, x.shape, 1)
    return jnp.sum(jnp.where(lane == c, x, 0.0), axis=1, keepdims=True)


def _head_bcast(cols, lane_head):
    out = cols[DN_HEADS - 1]
    for h in range(DN_HEADS - 2, -1, -1):
        out = jnp.where(lane_head <= h, cols[h], out)
    return out


def _level_mask(ri, ci, b):
    return ((ri // (2 * b)) == (ci // (2 * b))) & ((ri // b) != (ci // b))


def _gdn_prep(direction, qkv, bg, cum_ref, ones_ref, pu_ref, pwq_ref, pk_ref, pe_ref):
    cum_m = cum_ref[...]
    ones_m = ones_ref[...]
    ri = lax.broadcasted_iota(jnp.int32, (GDN_TILE, GDN_TILE), 0)
    ci = lax.broadcasted_iota(jnp.int32, (GDN_TILE, GDN_TILE), 1)
    blk = (ri // CHUNK) == (ci // CHUNK)
    lane_head = ci // HEAD_DIM
    if direction == 0:
        tri_strict, tri_incl = blk & (ri > ci), blk & (ri >= ci)
    else:
        tri_strict, tri_incl = blk & (ri < ci), blk & (ri <= ci)
    q = qkv[:, 0:DN_CH]
    k = qkv[:, DN_CH:2 * DN_CH]
    v = qkv[:, 2 * DN_CH:3 * DN_CH]

    gc = _dot_exact_lhs(cum_m, bg)
    gt = _dot_exact_lhs(ones_m, bg)
    gc_t = gc.T
    row_t = lax.broadcasted_iota(jnp.int32, gc_t.shape, 0)

    beta_cols, gc_cols, gt_cols = [], [], []
    for h in range(DN_HEADS):
        beta_cols.append(_lane_col(bg, direction * DN_HEADS + h))
        gc_cols.append(_lane_col(gc, 2 * DN_HEADS + direction * DN_HEADS + h))
        gt_cols.append(_lane_col(gt, 2 * DN_HEADS + direction * DN_HEADS + h))
    beta_b = _head_bcast(beta_cols, lane_head)
    gc_b = _head_bcast(gc_cols, lane_head)
    gt_b = _head_bcast(gt_cols, lane_head)

    e_gc = jnp.exp(gc_b)
    kb = k * beta_b
    vb = v * beta_b
    kbe = kb * e_gc
    qe = q * e_gc
    kd = k * jnp.exp(gt_b - gc_b)
    pe_ref[...] = jnp.exp(gt_b)
    pk_ref[0] = kd.T.astype(BF16)
    pwq_ref[1] = qe.astype(BF16)

    k16 = k.astype(BF16)
    rhs_uw = jnp.concatenate([vb, kbe], axis=1).astype(BF16)
    eye = (ri == ci).astype(F32)
    yield

    heads = range(DN_HEADS)
    levels = [2 ** e for e in range(1, CHUNK.bit_length() - 1)]
    neg_a, t = [], []
    for h in heads:
        in_h = lane_head == h
        r = jnp.sum(jnp.where(row_t == 2 * DN_HEADS + direction * DN_HEADS + h, gc_t, 0.0),
                    axis=0, keepdims=True)
        diff = gc_cols[h] - r
        dec_i = jnp.exp(jnp.where(tri_incl, diff, NEG_BIG))
        dec_s = jnp.where(tri_strict, dec_i, 0.0)
        kk = _dot_nt(jnp.where(in_h, kb, 0.0).astype(BF16), k16)
        qk = _dot_nt(jnp.where(in_h, q, 0.0).astype(BF16), k16)
        p = -(kk * dec_s)
        neg_a.append([jnp.where(_level_mask(ri, ci, b), p, 0.0).astype(BF16) for b in levels])
        t.append((eye + jnp.where((ri // 2) == (ci // 2), p, 0.0)).astype(BF16))
        pk_ref[1 + h] = (qk * dec_i).astype(BF16)
        yield

    for lvl in range(len(levels)):
        g = [_dot(neg_a[h][lvl], t[h]) for h in heads]
        yield
        t = [t[h] + _dot(t[h], g[h].astype(BF16)).astype(BF16) for h in heads]
        yield
    uw = [_dot(t[h], rhs_uw) for h in heads]
    u_all, w_all = uw[DN_HEADS - 1][:, :DN_CH], uw[DN_HEADS - 1][:, DN_CH:]
    for h in range(DN_HEADS - 2, -1, -1):
        u_all = jnp.where(lane_head <= h, uw[h][:, :DN_CH], u_all)
        w_all = jnp.where(lane_head <= h, uw[h][:, DN_CH:], w_all)
    pu_ref[...] = u_all
    pwq_ref[0] = w_all.astype(BF16)


def _gdn_scan(direction, s_ref, pu_ref, pwq_ref, pk_ref, pe_ref, o_ref, o_rows):
    ri = lax.broadcasted_iota(jnp.int32, (GDN_TILE, GDN_TILE), 0)
    ci = lax.broadcasted_iota(jnp.int32, (GDN_TILE, GDN_TILE), 1)
    blk = (ri // HEAD_DIM) == (ci // HEAD_DIM)
    order = range(CHUNKS_PER_TILE) if direction == 0 else range(CHUNKS_PER_TILE - 1, -1, -1)
    outs = [None] * CHUNKS_PER_TILE
    zeros_c = jnp.zeros((CHUNK, DN_CH), BF16)
    lane_head_c = lax.broadcasted_iota(jnp.int32, (CHUNK, DN_CH), 1) // HEAD_DIM
    for c in order:
        rows = pl.ds(c * CHUNK, CHUNK)
        s = s_ref[...]
        wq = jnp.concatenate([pwq_ref[0, rows, :], pwq_ref[1, rows, :]], axis=0)
        ws = _dot(wq, s.astype(BF16))
        yield
        v_new = (pu_ref[rows, :] - ws[:CHUNK]).astype(BF16)
        v_tile = jnp.concatenate([v_new if i == c else zeros_c for i in range(CHUNKS_PER_TILE)],
                                 axis=0)
        lhs = jnp.concatenate([pk_ref[0]] + [pk_ref[1 + h, rows, :] for h in range(DN_HEADS)],
                              axis=0)
        r2 = _dot(lhs, v_tile)
        s_ref[...] = s * pe_ref[pl.ds(c * CHUNK, 1), :] + jnp.where(blk, r2[:GDN_TILE], 0.0)
        o = ws[CHUNK:]
        for h in range(DN_HEADS):
            o = o + jnp.where(lane_head_c == h,
                              r2[GDN_TILE + h * CHUNK:GDN_TILE + (h + 1) * CHUNK], 0.0)
        outs[c] = o
        yield
    o_ref[o_rows, :] = jnp.concatenate(outs, axis=0)


def _trace_interleaved(stages):
    live = [[g, p] for g, p in stages]
    r = 0
    while live:
        for item in list(live):
            if r % item[1] == 0 and next(item[0], StopIteration) is StopIteration:
                live.remove(item)
        r += 1


def _gdn_kernel(n_lat, write_ctx, dc_ref, dl_ref, gc_ref, gl_ref, zc_ref, zl_ref, on_ref,
                cum_f_ref, cum_b_ref, ones_ref, ol_ref, oc_ref, of_ref, ob_ref, sf_ref, sb_ref,
                pu_ref, pwq_ref, pk_ref, pe_ref):
    T = GDN_TILE
    sf_ref[...] = jnp.zeros_like(sf_ref)
    sb_ref[...] = jnp.zeros_like(sb_ref)

    s_refs = (sf_ref, sb_ref)
    cum_refs = (cum_f_ref, cum_b_ref)
    o_refs = (of_ref, ob_ref)

    def tile_of(direction, s):
        if isinstance(s, int):
            return s if (direction == 0 or s == 0) else n_lat + 1 - s
        return s if direction == 0 else jnp.where(s == 0, 0, n_lat + 1 - s)

    def tile_rows(direction, s):
        t = tile_of(direction, s)
        return pl.ds(t * T, T) if isinstance(t, int) else pl.ds(pl.multiple_of(t * T, T), T)

    def tile_inputs(direction, s):
        if isinstance(s, int) and s == 0:
            return dc_ref[0], gc_ref[0]
        t = tile_of(direction, s)
        rows = (pl.ds((t - 1) * T, T) if isinstance(t, int)
                else pl.ds(pl.multiple_of((t - 1) * T, T), T))
        return dl_ref[0, rows, :], gl_ref[0, rows, :]

    def prep_stages(s, slot):
        return [(_gdn_prep(d, *tile_inputs(d, s), cum_refs[d], ones_ref, pu_ref.at[slot, d],
                           pwq_ref.at[slot, d], pk_ref.at[slot, d], pe_ref.at[slot, d]), 1)
                for d in range(2)]

    def scan_stages(s, slot):
        return [(_gdn_scan(d, s_refs[d], pu_ref.at[slot, d], pwq_ref.at[slot, d],
                           pk_ref.at[slot, d], pe_ref.at[slot, d], o_refs[d], tile_rows(d, s)),
                 SCAN_STAGE_PERIOD) for d in range(2)]

    _trace_interleaved(prep_stages(0, 0))

    def scan_pair(j, carry):
        s = 2 * j
        _trace_interleaved(scan_stages(s, 0) + prep_stages(s + 1, 1))
        _trace_interleaved(scan_stages(s + 1, 1) + prep_stages(s + 2, 0))
        return carry

    lax.fori_loop(0, n_lat // 2, scan_pair, 0)

    def finish(o, z):
        ms = _seg64_sum(o * o, 2) * (1.0 / HEAD_DIM)
        return (o * lax.rsqrt(ms + EPS) * on_ref[...] * _silu(z)).astype(BF16)

    def finish_stages(tiles):
        for t in tiles:
            if t == 0:
                if write_ctx:
                    oc_ref[0] = finish(of_ref[0:T] + ob_ref[0:T], zc_ref[0])
                else:
                    oc_ref[0] = jnp.zeros(oc_ref.shape[1:], BF16)
            else:
                rows = pl.ds(t * T, T)
                ol_ref[0, pl.ds((t - 1) * T, T), :] = finish(of_ref[rows, :] + ob_ref[rows, :],
                                                             zl_ref[0, pl.ds((t - 1) * T, T), :])
            yield

    last_tiles = sorted({1, n_lat})
    _trace_interleaved(scan_stages(n_lat, 0)
                       + [(finish_stages([t for t in range(n_lat + 1) if t not in last_tiles]), 1)])
    _trace_interleaved([(finish_stages(last_tiles), 1)])


def _gdn(d_ctx, d_lat, g_ctx, g_lat, z_ctx, z_lat, on_row, consts, write_ctx):
    B, L, _ = d_lat.shape
    Lc = d_ctx.shape[1]
    assert Lc == GDN_TILE and L % (2 * GDN_TILE) == 0
    n_lat = L // GDN_TILE
    T = GDN_TILE
    cum_f, cum_b, ones_m = consts
    per_b = lambda n, c: pl.BlockSpec((1, n, c), lambda b: (b, 0, 0))
    full = lambda a: pl.BlockSpec(a.shape, lambda b: (0,) * a.ndim)
    tot = L + Lc
    return pl.pallas_call(
        functools.partial(_gdn_kernel, n_lat, write_ctx),
        out_shape=(jax.ShapeDtypeStruct((B, L, DN_CH), BF16),
                   jax.ShapeDtypeStruct((B, Lc, DN_CH), BF16)),
        grid=(B,),
        in_specs=[per_b(Lc, CONV_CH), per_b(L, CONV_CH), per_b(Lc, LANES), per_b(L, LANES),
                  per_b(Lc, DN_CH), per_b(L, DN_CH), full(on_row), full(cum_f), full(cum_b),
                  full(ones_m)],
        out_specs=(per_b(L, DN_CH), per_b(Lc, DN_CH)),
        scratch_shapes=[pltpu.VMEM((tot, DN_CH), F32), pltpu.VMEM((tot, DN_CH), F32),
                        pltpu.VMEM((T, DN_CH), F32), pltpu.VMEM((T, DN_CH), F32),
                        pltpu.VMEM((2, 2, T, DN_CH), F32), pltpu.VMEM((2, 2, 2, T, DN_CH), BF16),
                        pltpu.VMEM((2, 2, 1 + DN_HEADS, T, T), BF16), pltpu.VMEM((2, 2, T, DN_CH), F32)],
        compiler_params=pltpu.CompilerParams(
            dimension_semantics=("arbitrary",), vmem_limit_bytes=VMEM_LIMIT),
        name="gdn",
    )(d_ctx, d_lat, g_ctx, g_lat, z_ctx, z_lat, on_row, cum_f, cum_b, ones_m)


def _mix_mlp_kernel(ff_tile, x_ref, fy_ref, at_ref, dn_ref, g1_ref, sc_ref, sh_ref, g2_ref, ng_ref,
                    wof_ref, woa_ref, wod_ref, w1_ref, w2_ref, o_ref):
    mix = (_dot(fy_ref[0], wof_ref[...]) + _dot(at_ref[0], woa_ref[...])
           + _dot(dn_ref[0], wod_ref[...]))
    x1 = x_ref[0] + g1_ref[0] * mix
    ms = jnp.mean(x1 * x1, axis=-1, keepdims=True)
    gain = ng_ref[...] * (1.0 + sc_ref[0])
    h = (x1 * lax.rsqrt(ms + EPS) * gain + sh_ref[0]).astype(BF16)
    acc = jnp.zeros_like(x1)
    for c in range(w1_ref.shape[1] // ff_tile):
        u = jnp.maximum(_dot(h, w1_ref[:, c * ff_tile:(c + 1) * ff_tile]), 0.0)
        acc = acc + _dot((u * u).astype(BF16), w2_ref[c * ff_tile:(c + 1) * ff_tile, :])
    o_ref[0] = x1 + g2_ref[0] * acc


def _mix_mlp(x, fy, att, dn, g1, sc, sh, g2, norm_g, wts, tm):
    B, L, D = x.shape
    per_batch = g1.shape[0] > 1
    mod_map = (lambda b, i: (b, 0, 0)) if per_batch else (lambda b, i: (0, 0, 0))
    mod = pl.BlockSpec((1, 1, D), mod_map)
    full = lambda a: pl.BlockSpec(a.shape, lambda b, i: (0,) * a.ndim,
                                  pipeline_mode=pl.Buffered(1))
    row = lambda n: pl.BlockSpec((1, tm, n), lambda b, i: (b, i, 0))
    wof, woa, wod, w1, w2 = wts
    return pl.pallas_call(
        functools.partial(_mix_mlp_kernel, 1024),
        out_shape=jax.ShapeDtypeStruct((B, L, D), F32),
        grid=(B, L // tm),
        in_specs=[row(D), row(FT_CH), row(ATT_CH), row(DN_CH), mod, mod, mod, mod,
                  pl.BlockSpec(norm_g.shape, lambda b, i: (0, 0)),
                  full(wof), full(woa), full(wod), full(w1), full(w2)],
        out_specs=row(D),
        compiler_params=pltpu.CompilerParams(
            dimension_semantics=("arbitrary", "arbitrary"), vmem_limit_bytes=VMEM_LIMIT),
        name="mix_mlp",
    )(x, fy, att, dn, g1, sc, sh, g2, norm_g, wof, woa, wod, w1, w2)


def _rope_tables(S):
    rows = S // GRID_W
    t_row = jnp.repeat(jnp.arange(rows), GRID_W).astype(F32)
    t_col = jnp.tile(jnp.arange(GRID_W), rows).astype(F32)
    inv_freq = ROPE_THETA ** (-jnp.arange(ROT_PAIRS, dtype=F32) * 2.0 / AXIS_DIM)
    ang_r = t_row[:, None] * inv_freq
    ang_c = t_col[:, None] * inv_freq
    ang = jnp.concatenate([ang_r, ang_r, ang_c, ang_c], axis=-1)
    cos, sin = jnp.cos(ang), jnp.sin(ang)
    sign = jnp.where((jnp.arange(HEAD_DIM) & ROT_PAIRS) == 0, -1.0, 1.0).astype(F32)
    tile2 = lambda a: jnp.concatenate([a, a], axis=-1)
    return tile2(cos), tile2(sin * sign)


def _cos_sin(rows, cols, period):
    k = (jnp.arange(rows, dtype=jnp.int32)[:, None] * jnp.arange(cols, dtype=jnp.int32)[None, :]) % period
    ang = k.astype(F32) * np.float32(2.0 * np.pi / period)
    return jnp.cos(ang), jnp.sin(ang)


def _dft_tables(n):
    if n <= HEAD_DIM:
        return _cos_sin(n, n, n)
    assert n % HEAD_DIM == 0
    ca, sa = _cos_sin(n, n // HEAD_DIM, n // HEAD_DIM)
    cb, sb = _cos_sin(n, HEAD_DIM, n)
    cos = ca[:, :, None] * cb[:, None, :] - sa[:, :, None] * sb[:, None, :]
    sin = sa[:, :, None] * cb[:, None, :] + ca[:, :, None] * sb[:, None, :]
    return cos.reshape(n, n), sin.reshape(n, n)


def _channel_dft():
    c, s = _dft_tables(HEAD_DIM)
    eye = jnp.eye(FT_GROUPS, dtype=F32)
    return jnp.concatenate([jnp.kron(eye, c), jnp.kron(eye, s)], axis=1).astype(BF16)


def _gdn_consts():
    i = np.arange(GDN_TILE)
    blk = (i[:, None] // CHUNK) == (i[None, :] // CHUNK)
    cum_f = (blk & (i[:, None] >= i[None, :])).astype(np.float32)
    cum_b = (blk & (i[:, None] <= i[None, :])).astype(np.float32)
    return (jnp.asarray(cum_f, BF16), jnp.asarray(cum_b, BF16), jnp.asarray(blk.astype(np.float32), BF16))


def _pad_lanes(a):
    flat = a.reshape(1, -1).astype(F32)
    return jnp.pad(flat, ((0, 0), (0, LANES - flat.shape[1])))


def kernel(x, c, ctx, c_ctx, norm1_g, norm2_g, w_mod, b_mod, w_in, conv_w, q_norm_g, k_norm_g,
           a_log, dt_bias, o_norm_g, w_out, w_ff1, w_ff2):
    B, S, D = x.shape
    Lc = ctx.shape[1]
    depth = w_mod.shape[0]

    rows = ((B + 1 + 7) // 8) * 8
    c_all = jnp.concatenate([c, c_ctx[None, :], jnp.zeros((rows - B - 1, D), F32)], axis=0)
    mod_all = _modulation(c_all, w_mod, b_mod)

    cos_t, sin_t = _rope_tables(S)
    dft_ch = _channel_dft()
    dft_lat = tuple(t.astype(BF16) for t in _dft_tables(S))
    dft_ctx = tuple(t.astype(BF16) for t in _dft_tables(Lc))
    gdn_consts = _gdn_consts()

    o0 = FT_CH
    o1 = o0 + ATT_CH
    o2 = o1 + 2 * KV_CH
    o3 = o2 + CONV_CH
    o4 = o3 + DN_CH

    x_lat, x_ctx = x, ctx
    for l in range(depth):
        last = l == depth - 1
        w = w_in[l]
        wg = jnp.pad(w[:, o4:], ((0, 0), (0, LANES - N_GATES)))
        in_wts = (w[:, :o0].astype(BF16), dft_ch, w[:, o0:o1].astype(BF16), w[:, o1:o2].astype(BF16),
                  w[:, o2:o3].astype(BF16), w[:, o3:o4].astype(BF16), wg.astype(BF16))
        qg = jnp.tile(q_norm_g[l], ATT_HQ).reshape(1, ATT_CH)
        kg = jnp.tile(k_norm_g[l], ATT_HKV).reshape(1, KV_CH)
        n1 = norm1_g[l].reshape(1, D)
        n2 = norm2_g[l].reshape(1, D)
        wo = w_out[l].astype(BF16)
        out_wts = (wo[:FT_CH], wo[FT_CH:FT_CH + ATT_CH], wo[FT_CH + ATT_CH:],
                   w_ff1[l].astype(BF16), w_ff2[l].astype(BF16))
        alog_row = jnp.pad(a_log[l].reshape(1, -1), ((0, 0), (2 * DN_HEADS, LANES - N_GATES)))
        dtb_row = jnp.pad(dt_bias[l].reshape(1, -1), ((0, 0), (2 * DN_HEADS, LANES - N_GATES)))
        on_row = jnp.tile(o_norm_g[l], DN_HEADS).reshape(1, DN_CH)

        mod = mod_all[l, :B].reshape(B, 1, 6 * D)
        modc = mod_all[l, B:B + 1].reshape(1, 1, 6 * D)
        sh1, sc1, g1, sh2, sc2, g2 = [mod[:, :, i * D:(i + 1) * D] for i in range(6)]
        csh1, csc1, cg1, csh2, csc2, cg2 = [modc[:, :, i * D:(i + 1) * D] for i in range(6)]

        fl, ql, kl, vl, dl, zl, gl = _project(x_lat, sc1, sh1, n1, in_wts, qg, kg, cos_t, sin_t,
                                              conv_w[l], alog_row, dtb_row, True, min(S, 1024))
        fc, qc, kc, vc, dc, zc, gc = _project(x_ctx, csc1, csh1, n1, in_wts, qg, kg,
                                              cos_t[:Lc], sin_t[:Lc], conv_w[l], alog_row, dtb_row,
                                              False, Lc)

        dn_lat, dn_ctx = _gdn(dc, dl, gc, gl, zc, zl, on_row, gdn_consts, not last)
        att_lat = _attention(ql, [(kl, vl), (kc, vc)], min(S, 256))
        fy_lat = _fourier(fl, dft_lat[0], dft_lat[1], min(S, 1024))
        x_lat = _mix_mlp(x_lat, fy_lat, att_lat, dn_lat, g1, sc2, sh2, g2, n2, out_wts,
                         min(S, 1024))
        if not last:
            att_ctx = _attention(qc, [(kc, vc)], Lc)
            fy_ctx = _fourier(fc, dft_ctx[0], dft_ctx[1], Lc)
            x_ctx = _mix_mlp(x_ctx, fy_ctx, att_ctx, dn_ctx, cg1, csc2, csh2, cg2, n2, out_wts, Lc)
    return x_lat
```

```python
import functools

import jax
import jax.numpy as jnp
import numpy as np
from jax import lax
from jax.experimental import pallas as pl
from jax.experimental.pallas import tpu as pltpu

HEAD_DIM = 64
FT_GROUPS = 4
FT_CH = FT_GROUPS * HEAD_DIM
ATT_HQ = 8
ATT_HKV = 2
ATT_GROUP = ATT_HQ // ATT_HKV
ATT_CH = ATT_HQ * HEAD_DIM
KV_CH = ATT_HKV * HEAD_DIM
DN_HEADS = 4
DN_CH = DN_HEADS * HEAD_DIM
CONV_CH = 3 * DN_CH
N_GATES = 4 * DN_HEADS
GRID_W = 64
ROPE_THETA = 10000.0
AXIS_DIM = HEAD_DIM // 2
ROT_PAIRS = AXIS_DIM // 2
EPS = 1e-6
CHUNK = 64

LANES = 128
SUBLANES = 8
HALO_ROWS = SUBLANES
MXU_DIM = 256
V7X_VMEM_BYTES = 64 * 1024 * 1024
VMEM_LIMIT = V7X_VMEM_BYTES - 8 * 1024 * 1024

GDN_TILE = 256
CHUNKS_PER_TILE = GDN_TILE // CHUNK
NEG_BIG = -1e30
SCAN_STAGE_PERIOD = 1

F32 = jnp.float32
BF16 = jnp.bfloat16


def _dot(a, b):
    return jnp.dot(a, b, preferred_element_type=F32)


def _dot_nt(a, b):
    return lax.dot_general(a, b, (((1,), (1,)), ((), ())), preferred_element_type=F32)


def _split3(x):
    hi = x.astype(BF16)
    r1 = x - hi.astype(F32)
    mid = r1.astype(BF16)
    lo = (r1 - mid.astype(F32)).astype(BF16)
    return hi, mid, lo


def _dot_exact_lhs(m_bf16, x):
    hi, mid, lo = _split3(x)
    return _dot(m_bf16, hi) + _dot(m_bf16, mid) + _dot(m_bf16, lo)


def _seg64_sum(x, passes):
    w = x.shape[1]
    blk = min(w, MXU_DIM)
    r = lax.broadcasted_iota(jnp.int32, (blk, blk), 0) // HEAD_DIM
    c = lax.broadcasted_iota(jnp.int32, (blk, blk), 1) // HEAD_DIM
    ones = jnp.where(r == c, 1.0, 0.0).astype(BF16)
    outs = []
    for j in range(w // blk):
        xs = x[:, j * blk:(j + 1) * blk]
        hi = xs.astype(BF16)
        acc = _dot(hi, ones)
        if passes == 2:
            acc = acc + _dot((xs - hi.astype(F32)).astype(BF16), ones)
        outs.append(acc)
    return outs[0] if len(outs) == 1 else jnp.concatenate(outs, axis=1)


def _sigmoid(x):
    return 1.0 / (1.0 + jnp.exp(-x))


def _silu(x):
    return x * _sigmoid(x)


def _softplus(x):
    return jnp.maximum(x, 0.0) + jnp.log1p(jnp.exp(-jnp.abs(x)))


def _mod_kernel(c_ref, w_ref, b_ref, o_ref):
    a = _silu(c_ref[...])
    o_ref[0] = jnp.dot(a, w_ref[0], preferred_element_type=F32,
                       precision=lax.Precision.HIGHEST) + b_ref[0]


def _modulation(c_all, w_mod, b_mod):
    depth, d, n = w_mod.shape
    rows = c_all.shape[0]
    tn = 1024
    return pl.pallas_call(
        _mod_kernel,
        out_shape=jax.ShapeDtypeStruct((depth, rows, n), F32),
        grid=(depth, n // tn),
        in_specs=[pl.BlockSpec((rows, d), lambda l, j: (0, 0)),
                  pl.BlockSpec((1, d, tn), lambda l, j: (l, 0, j)),
                  pl.BlockSpec((1, 1, tn), lambda l, j: (l, 0, j))],
        out_specs=pl.BlockSpec((1, rows, tn), lambda l, j: (l, 0, j)),
        compiler_params=pltpu.CompilerParams(
            dimension_semantics=("arbitrary", "arbitrary"), vmem_limit_bytes=VMEM_LIMIT),
        name="modulation",
    )(c_all, w_mod, b_mod.reshape(depth, 1, n))


def _head_rms_rope(z, gain, cos, sin_signed, scale):
    ms = _seg64_sum(z * z, 1) * (1.0 / HEAD_DIM)
    y = z * lax.rsqrt(ms + EPS) * gain
    if scale != 1.0:
        y = y * scale
    if cos is None:
        return y
    lane = lax.broadcasted_iota(jnp.int32, cos.shape, 1)
    slabs = []
    for j in range(z.shape[1] // LANES):
        ys = y[:, j * LANES:(j + 1) * LANES]
        partner = jnp.where((lane & ROT_PAIRS) == 0,
                            pltpu.roll(ys, LANES - ROT_PAIRS, axis=1),
                            pltpu.roll(ys, ROT_PAIRS, axis=1))
        slabs.append(ys * cos + partner * sin_signed)
    return slabs[0] if len(slabs) == 1 else jnp.concatenate(slabs, axis=1)


def _conv_silu_norm(x, prev_row, next_row, w_ref):
    n = x.shape[0]
    row = lax.broadcasted_iota(jnp.int32, x.shape, 0)
    x_m1 = jnp.where(row == 0, prev_row, pltpu.roll(x, 1, axis=0))
    x_p1 = jnp.where(row == n - 1, next_row, pltpu.roll(x, n - 1, axis=0))
    y = _silu(x_m1 * w_ref[0:1, :] + x * w_ref[1:2, :] + x_p1 * w_ref[2:3, :])
    qk = y[:, :2 * DN_CH]
    qk = qk * lax.rsqrt(_seg64_sum(qk * qk, 1) + EPS)
    return jnp.concatenate([qk[:, :DN_CH] * (HEAD_DIM ** -0.5), qk[:, DN_CH:], y[:, 2 * DN_CH:]],
                           axis=1)


def _gates_to_beta_g(z, alog_ref, dtb_ref):
    lane = lax.broadcasted_iota(jnp.int32, z.shape, 1)
    g = -jnp.exp(alog_ref[...]) * _softplus(z + dtb_ref[...])
    return jnp.where(lane < 2 * DN_HEADS, _sigmoid(z), g)


def _project_kernel(use_rope, x_ref, xp_ref, xn_ref, sc_ref, sh_ref, g_ref, wf_ref, dft_ref, wq_ref,
                    wkv_ref, wd_ref, wz_ref, wg_ref, qg_ref, kg_ref, cos_ref, sin_ref, cw_ref,
                    alog_ref, dtb_ref, f_ref, q_ref, k_ref, v_ref, d_ref, z_ref, bg_ref):
    gain = g_ref[...] * (1.0 + sc_ref[0])

    def modulated(x):
        ms = jnp.mean(x * x, axis=-1, keepdims=True)
        return (x * lax.rsqrt(ms + EPS) * gain + sh_ref[0]).astype(BF16)

    h = modulated(x_ref[0])
    cos = cos_ref[...] if use_rope else None
    sin = sin_ref[...] if use_rope else None

    i = pl.program_id(1)
    tm = h.shape[0]
    h_halo = modulated(jnp.concatenate([xp_ref[0], xn_ref[0]], axis=0))
    zd_ext = _dot(jnp.concatenate([h, h_halo], axis=0), wd_ref[...])
    zd, zd_halo = zd_ext[:tm], zd_ext[tm:]
    halo_row = lax.broadcasted_iota(jnp.int32, zd_halo.shape, 0)
    prev_row = jnp.sum(jnp.where(halo_row == HALO_ROWS - 1, zd_halo, 0.0), axis=0, keepdims=True)
    next_row = jnp.sum(jnp.where(halo_row == HALO_ROWS, zd_halo, 0.0), axis=0, keepdims=True)
    prev_row = jnp.where(i == 0, 0.0, prev_row)
    next_row = jnp.where(i == pl.num_programs(1) - 1, 0.0, next_row)

    zq = _dot(h, wq_ref[...])
    d_ref[0] = _conv_silu_norm(zd, prev_row, next_row, cw_ref)
    zkv = _dot(h, wkv_ref[...])
    f = _dot(h, wf_ref[...]).astype(BF16)
    q_ref[0] = _head_rms_rope(zq, qg_ref[...], cos, sin, HEAD_DIM ** -0.5).astype(BF16)
    z_ref[0] = _dot(h, wz_ref[...])
    k_ref[0] = _head_rms_rope(zkv[:, :KV_CH], kg_ref[...], cos, sin, 1.0).astype(BF16)
    v_ref[0] = zkv[:, KV_CH:].astype(BF16)
    bg_ref[0] = _gates_to_beta_g(_dot(h, wg_ref[...]), alog_ref, dtb_ref)
    f_ref[0] = _dot(f, dft_ref[...]).astype(BF16)


def _project(x, sc, sh, norm_g, wts, qg, kg, cos_t, sin_t, conv_w, alog_row, dtb_row, use_rope, tm):
    B, L, D = x.shape
    per_batch = sc.shape[0] > 1
    mod_map = (lambda b, i: (b, 0, 0)) if per_batch else (lambda b, i: (0, 0, 0))
    full = lambda a: pl.BlockSpec(a.shape, lambda b, i: (0,) * a.ndim)
    row = lambda n: pl.BlockSpec((1, tm, n), lambda b, i: (b, i, 0))
    per = tm // HALO_ROWS
    last = L // HALO_ROWS - 1
    halo_prev = pl.BlockSpec((1, HALO_ROWS, D), lambda b, i: (b, jnp.maximum(i * per - 1, 0), 0))
    halo_next = pl.BlockSpec((1, HALO_ROWS, D), lambda b, i: (b, jnp.minimum((i + 1) * per, last), 0))
    wf, dft, wq, wkv, wd, wz, wg = wts
    out_shape = (jax.ShapeDtypeStruct((B, L, 2 * FT_CH), BF16),
                 jax.ShapeDtypeStruct((B, L, ATT_CH), BF16),
                 jax.ShapeDtypeStruct((B, L, KV_CH), BF16),
                 jax.ShapeDtypeStruct((B, L, KV_CH), BF16),
                 jax.ShapeDtypeStruct((B, L, CONV_CH), F32),
                 jax.ShapeDtypeStruct((B, L, DN_CH), F32),
                 jax.ShapeDtypeStruct((B, L, LANES), F32))
    return pl.pallas_call(
        functools.partial(_project_kernel, use_rope),
        out_shape=out_shape,
        grid=(B, L // tm),
        in_specs=[row(D), halo_prev, halo_next,
                  pl.BlockSpec((1, 1, D), mod_map), pl.BlockSpec((1, 1, D), mod_map),
                  full(norm_g), full(wf), full(dft), full(wq), full(wkv), full(wd), full(wz),
                  full(wg), full(qg), full(kg),
                  pl.BlockSpec((tm, LANES), lambda b, i: (i, 0)),
                  pl.BlockSpec((tm, LANES), lambda b, i: (i, 0)),
                  full(conv_w), full(alog_row), full(dtb_row)],
        out_specs=(row(2 * FT_CH), row(ATT_CH), row(KV_CH), row(KV_CH), row(CONV_CH),
                   row(DN_CH), row(LANES)),
        compiler_params=pltpu.CompilerParams(
            dimension_semantics=("arbitrary", "arbitrary"), vmem_limit_bytes=VMEM_LIMIT),
        name="project_rope" if use_rope else "project",
    )(x, x, x, sc, sh, norm_g, wf, dft, wq, wkv, wd, wz, wg, qg, kg, cos_t, sin_t,
      conv_w, alog_row, dtb_row)


def _fourier_kernel(scale, dc_ref, ds_ref, f_ref, o_ref):
    fcs = f_ref[0]
    y = _dot(dc_ref[...], fcs[:, :FT_CH]) - _dot(ds_ref[...], fcs[:, FT_CH:])
    o_ref[0] = (y * scale).astype(BF16)


def _fourier(fcs, dft_c, dft_s, tn):
    B, L, _ = fcs.shape
    scale = float(1.0 / np.sqrt(L * HEAD_DIM))
    return pl.pallas_call(
        functools.partial(_fourier_kernel, scale),
        out_shape=jax.ShapeDtypeStruct((B, L, FT_CH), BF16),
        grid=(L // tn, B),
        in_specs=[pl.BlockSpec((tn, L), lambda n, b: (n, 0)),
                  pl.BlockSpec((tn, L), lambda n, b: (n, 0)),
                  pl.BlockSpec((1, L, 2 * FT_CH), lambda n, b: (b, 0, 0))],
        out_specs=pl.BlockSpec((1, tn, FT_CH), lambda n, b: (b, n, 0)),
        compiler_params=pltpu.CompilerParams(
            dimension_semantics=("arbitrary", "arbitrary"), vmem_limit_bytes=VMEM_LIMIT),
        name="fourier",
    )(dft_c, dft_s, fcs)


def _attention_kernel(n_src, q_ref, *refs):
    kv_refs = refs[:2 * n_src]
    o_ref = refs[2 * n_src]
    tq = q_ref.shape[1]
    kv_heads = range(ATT_HKV)
    scores, row_max = [], []
    for h in kv_heads:
        q = q_ref[0, :, h * ATT_GROUP * HEAD_DIM:(h + 1) * ATT_GROUP * HEAD_DIM]
        q4 = jnp.concatenate([q[:, g * HEAD_DIM:(g + 1) * HEAD_DIM] for g in range(ATT_GROUP)],
                             axis=0)
        sc_h, m = [], None
        for s in range(n_src):
            k = kv_refs[2 * s][0, :, h * HEAD_DIM:(h + 1) * HEAD_DIM]
            sc = _dot_nt(q4, k)
            sc_h.append(sc)
            ms = jnp.max(sc, axis=-1, keepdims=True)
            m = ms if m is None else jnp.maximum(m, ms)
        scores.append(sc_h)
        row_max.append(m)
    accs = []
    for h in kv_heads:
        acc = None
        for s in range(n_src):
            v = kv_refs[2 * s + 1][0]
            lane = lax.broadcasted_iota(jnp.int32, v.shape, 1)
            in_head = (lane >= h * HEAD_DIM) & (lane < (h + 1) * HEAD_DIM)
            v_aug = jnp.where(in_head, v, jnp.ones_like(v))
            p = jnp.exp(scores[h][s] - row_max[h]).astype(BF16)
            pv = _dot(p, v_aug)
            acc = pv if acc is None else acc + pv
        accs.append(acc)
    for h in kv_heads:
        o = accs[h][:, h * HEAD_DIM:(h + 1) * HEAD_DIM]
        den = accs[h][:, (1 - h) * HEAD_DIM:(2 - h) * HEAD_DIM]
        o = (o / den).astype(BF16)
        for g in range(ATT_GROUP):
            c0 = (h * ATT_GROUP + g) * HEAD_DIM
            o_ref[0, :, c0:c0 + HEAD_DIM] = o[g * tq:(g + 1) * tq]


def _attention(q, kv_sources, tq):
    B, L, _ = q.shape
    n_src = len(kv_sources)
    in_specs = [pl.BlockSpec((1, tq, ATT_CH), lambda b, i: (b, i, 0))]
    args = [q]
    for k, v in kv_sources:
        lk = k.shape[1]
        in_specs += [pl.BlockSpec((1, lk, KV_CH), lambda b, i: (b, 0, 0)),
                     pl.BlockSpec((1, lk, KV_CH), lambda b, i: (b, 0, 0))]
        args += [k, v]
    return pl.pallas_call(
        functools.partial(_attention_kernel, n_src),
        out_shape=jax.ShapeDtypeStruct((B, L, ATT_CH), BF16),
        grid=(B, L // tq),
        in_specs=in_specs,
        out_specs=pl.BlockSpec((1, tq, ATT_CH), lambda b, i: (b, i, 0)),
        compiler_params=pltpu.CompilerParams(
            dimension_semantics=("arbitrary", "arbitrary"), vmem_limit_bytes=VMEM_LIMIT),
        name="attention",
    )(*args)


def _lane_col(x, c):
    lane = lax.broadcasted_iota(jnp.int32, x.shape, 1)
    return jnp.sum(jnp.where(lane == c, x, 0.0), axis=1, keepdims=True)


def _head_bcast(cols, lane_head):
    out = cols[DN_HEADS - 1]
    for h in range(DN_HEADS - 2, -1, -1):
        out = jnp.where(lane_head <= h, cols[h], out)
    return out


def _level_mask(ri, ci, b):
    return ((ri // (2 * b)) == (ci // (2 * b))) & ((ri // b) != (ci // b))


def _gdn_prep(direction, qkv, bg, cum_ref, ones_ref, pu_ref, pwq_ref, pk_ref, pe_ref):
    cum_m = cum_ref[...]
    ones_m = ones_ref[...]
    ri = lax.broadcasted_iota(jnp.int32, (GDN_TILE, GDN_TILE), 0)
    ci = lax.broadcasted_iota(jnp.int32, (GDN_TILE, GDN_TILE), 1)
    blk = (ri // CHUNK) == (ci // CHUNK)
    lane_head = ci // HEAD_DIM
    if direction == 0:
        tri_strict, tri_incl = blk & (ri > ci), blk & (ri >= ci)
    else:
        tri_strict, tri_incl = blk & (ri < ci), blk & (ri <= ci)
    q = qkv[:, 0:DN_CH]
    k = qkv[:, DN_CH:2 * DN_CH]
    v = qkv[:, 2 * DN_CH:3 * DN_CH]

    gc = _dot_exact_lhs(cum_m, bg)
    gt = _dot_exact_lhs(ones_m, bg)
    gc_t = gc.T
    row_t = lax.broadcasted_iota(jnp.int32, gc_t.shape, 0)

    beta_cols, gc_cols, gt_cols = [], [], []
    for h in range(DN_HEADS):
        beta_cols.append(_lane_col(bg, direction * DN_HEADS + h))
        gc_cols.append(_lane_col(gc, 2 * DN_HEADS + direction * DN_HEADS + h))
        gt_cols.append(_lane_col(gt, 2 * DN_HEADS + direction * DN_HEADS + h))
    beta_b = _head_bcast(beta_cols, lane_head)
    gc_b = _head_bcast(gc_cols, lane_head)
    gt_b = _head_bcast(gt_cols, lane_head)

    e_gc = jnp.exp(gc_b)
    kb = k * beta_b
    vb = v * beta_b
    kbe = kb * e_gc
    qe = q * e_gc
    kd = k * jnp.exp(gt_b - gc_b)
    pe_ref[...] = jnp.exp(gt_b)
    pk_ref[0] = kd.T.astype(BF16)
    pwq_ref[1] = qe.astype(BF16)

    k16 = k.astype(BF16)
    rhs_uw = jnp.concatenate([vb, kbe], axis=1).astype(BF16)
    eye = (ri == ci).astype(F32)
    yield

    heads = range(DN_HEADS)
    levels = [2 ** e for e in range(1, CHUNK.bit_length() - 1)]
    neg_a, t = [], []
    for h in heads:
        in_h = lane_head == h
        r = jnp.sum(jnp.where(row_t == 2 * DN_HEADS + direction * DN_HEADS + h, gc_t, 0.0),
                    axis=0, keepdims=True)
        diff = gc_cols[h] - r
        dec_i = jnp.exp(jnp.where(tri_incl, diff, NEG_BIG))
        dec_s = jnp.where(tri_strict, dec_i, 0.0)
        kk = _dot_nt(jnp.where(in_h, kb, 0.0).astype(BF16), k16)
        qk = _dot_nt(jnp.where(in_h, q, 0.0).astype(BF16), k16)
        p = -(kk * dec_s)
        neg_a.append([jnp.where(_level_mask(ri, ci, b), p, 0.0).astype(BF16) for b in levels])
        t.append((eye + jnp.where((ri // 2) == (ci // 2), p, 0.0)).astype(BF16))
        pk_ref[1 + h] = (qk * dec_i).astype(BF16)
        yield

    for lvl in range(len(levels)):
        g = [_dot(neg_a[h][lvl], t[h]) for h in heads]
        yield
        t = [t[h] + _dot(t[h], g[h].astype(BF16)).astype(BF16) for h in heads]
        yield
    uw = [_dot(t[h], rhs_uw) for h in heads]
    u_all, w_all = uw[DN_HEADS - 1][:, :DN_CH], uw[DN_HEADS - 1][:, DN_CH:]
    for h in range(DN_HEADS - 2, -1, -1):
        u_all = jnp.where(lane_head <= h, uw[h][:, :DN_CH], u_all)
        w_all = jnp.where(lane_head <= h, uw[h][:, DN_CH:], w_all)
    pu_ref[...] = u_all
    pwq_ref[0] = w_all.astype(BF16)


def _gdn_scan(direction, s_ref, pu_ref, pwq_ref, pk_ref, pe_ref, o_ref, o_rows):
    ri = lax.broadcasted_iota(jnp.int32, (GDN_TILE, GDN_TILE), 0)
    ci = lax.broadcasted_iota(jnp.int32, (GDN_TILE, GDN_TILE), 1)
    blk = (ri // HEAD_DIM) == (ci // HEAD_DIM)
    order = range(CHUNKS_PER_TILE) if direction == 0 else range(CHUNKS_PER_TILE - 1, -1, -1)
    outs = [None] * CHUNKS_PER_TILE
    zeros_c = jnp.zeros((CHUNK, DN_CH), BF16)
    lane_head_c = lax.broadcasted_iota(jnp.int32, (CHUNK, DN_CH), 1) // HEAD_DIM
    for c in order:
        rows = pl.ds(c * CHUNK, CHUNK)
        s = s_ref[...]
        wq = jnp.concatenate([pwq_ref[0, rows, :], pwq_ref[1, rows, :]], axis=0)
        ws = _dot(wq, s.astype(BF16))
        yield
        v_new = (pu_ref[rows, :] - ws[:CHUNK]).astype(BF16)
        v_tile = jnp.concatenate([v_new if i == c else zeros_c for i in range(CHUNKS_PER_TILE)],
                                 axis=0)
        lhs = jnp.concatenate([pk_ref[0]] + [pk_ref[1 + h, rows, :] for h in range(DN_HEADS)],
                              axis=0)
        r2 = _dot(lhs, v_tile)
        s_ref[...] = s * pe_ref[pl.ds(c * CHUNK, 1), :] + jnp.where(blk, r2[:GDN_TILE], 0.0)
        o = ws[CHUNK:]
        for h in range(DN_HEADS):
            o = o + jnp.where(lane_head_c == h,
                              r2[GDN_TILE + h * CHUNK:GDN_TILE + (h + 1) * CHUNK], 0.0)
        outs[c] = o
        yield
    o_ref[o_rows, :] = jnp.concatenate(outs, axis=0)


def _trace_interleaved(stages):
    live = [[g, p] for g, p in stages]
    r = 0
    while live:
        for item in list(live):
            if r % item[1] == 0 and next(item[0], StopIteration) is StopIteration:
                live.remove(item)
        r += 1


def _gdn_kernel(n_lat, write_ctx, dc_ref, dl_ref, gc_ref, gl_ref, zc_ref, zl_ref, on_ref,
                cum_f_ref, cum_b_ref, ones_ref, ol_ref, oc_ref, of_ref, ob_ref, sf_ref, sb_ref,
                pu_ref, pwq_ref, pk_ref, pe_ref):
    T = GDN_TILE
    sf_ref[...] = jnp.zeros_like(sf_ref)
    sb_ref[...] = jnp.zeros_like(sb_ref)

    s_refs = (sf_ref, sb_ref)
    cum_refs = (cum_f_ref, cum_b_ref)
    o_refs = (of_ref, ob_ref)

    def tile_of(direction, s):
        if isinstance(s, int):
            return s if (direction == 0 or s == 0) else n_lat + 1 - s
        return s if direction == 0 else jnp.where(s == 0, 0, n_lat + 1 - s)

    def tile_rows(direction, s):
        t = tile_of(direction, s)
        return pl.ds(t * T, T) if isinstance(t, int) else pl.ds(pl.multiple_of(t * T, T), T)

    def tile_inputs(direction, s):
        if isinstance(s, int) and s == 0:
            return dc_ref[0], gc_ref[0]
        t = tile_of(direction, s)
        rows = (pl.ds((t - 1) * T, T) if isinstance(t, int)
                else pl.ds(pl.multiple_of((t - 1) * T, T), T))
        return dl_ref[0, rows, :], gl_ref[0, rows, :]

    def prep_stages(s, slot):
        return [(_gdn_prep(d, *tile_inputs(d, s), cum_refs[d], ones_ref, pu_ref.at[slot, d],
                           pwq_ref.at[slot, d], pk_ref.at[slot, d], pe_ref.at[slot, d]), 1)
                for d in range(2)]

    def scan_stages(s, slot):
        return [(_gdn_scan(d, s_refs[d], pu_ref.at[slot, d], pwq_ref.at[slot, d],
                           pk_ref.at[slot, d], pe_ref.at[slot, d], o_refs[d], tile_rows(d, s)),
                 SCAN_STAGE_PERIOD) for d in range(2)]

    _trace_interleaved(prep_stages(0, 0))

    def scan_pair(j, carry):
        s = 2 * j
        _trace_interleaved(scan_stages(s, 0) + prep_stages(s + 1, 1))
        _trace_interleaved(scan_stages(s + 1, 1) + prep_stages(s + 2, 0))
        return carry

    lax.fori_loop(0, n_lat // 2, scan_pair, 0)

    def finish(o, z):
        ms = _seg64_sum(o * o, 2) * (1.0 / HEAD_DIM)
        return (o * lax.rsqrt(ms + EPS) * on_ref[...] * _silu(z)).astype(BF16)

    def finish_stages(tiles):
        for t in tiles:
            if t == 0:
                if write_ctx:
                    oc_ref[0] = finish(of_ref[0:T] + ob_ref[0:T], zc_ref[0])
                else:
                    oc_ref[0] = jnp.zeros(oc_ref.shape[1:], BF16)
            else:
                rows = pl.ds(t * T, T)
                ol_ref[0, pl.ds((t - 1) * T, T), :] = finish(of_ref[rows, :] + ob_ref[rows, :],
                                                             zl_ref[0, pl.ds((t - 1) * T, T), :])
            yield

    last_tiles = sorted({1, n_lat})
    _trace_interleaved(scan_stages(n_lat, 0)
                       + [(finish_stages([t for t in range(n_lat + 1) if t not in last_tiles]), 1)])
    _trace_interleaved([(finish_stages(last_tiles), 1)])


def _gdn(d_ctx, d_lat, g_ctx, g_lat, z_ctx, z_lat, on_row, consts, write_ctx):
    B, L, _ = d_lat.shape
    Lc = d_ctx.shape[1]
    assert Lc == GDN_TILE and L % (2 * GDN_TILE) == 0
    n_lat = L // GDN_TILE
    T = GDN_TILE
    cum_f, cum_b, ones_m = consts
    per_b = lambda n, c: pl.BlockSpec((1, n, c), lambda b: (b, 0, 0))
    full = lambda a: pl.BlockSpec(a.shape, lambda b: (0,) * a.ndim)
    tot = L + Lc
    return pl.pallas_call(
        functools.partial(_gdn_kernel, n_lat, write_ctx),
        out_shape=(jax.ShapeDtypeStruct((B, L, DN_CH), BF16),
                   jax.ShapeDtypeStruct((B, Lc, DN_CH), BF16)),
        grid=(B,),
        in_specs=[per_b(Lc, CONV_CH), per_b(L, CONV_CH), per_b(Lc, LANES), per_b(L, LANES),
                  per_b(Lc, DN_CH), per_b(L, DN_CH), full(on_row), full(cum_f), full(cum_b),
                  full(ones_m)],
        out_specs=(per_b(L, DN_CH), per_b(Lc, DN_CH)),
        scratch_shapes=[pltpu.VMEM((tot, DN_CH), F32), pltpu.VMEM((tot, DN_CH), F32),
                        pltpu.VMEM((T, DN_CH), F32), pltpu.VMEM((T, DN_CH), F32),
                        pltpu.VMEM((2, 2, T, DN_CH), F32), pltpu.VMEM((2, 2, 2, T, DN_CH), BF16),
                        pltpu.VMEM((2, 2, 1 + DN_HEADS, T, T), BF16), pltpu.VMEM((2, 2, T, DN_CH), F32)],
        compiler_params=pltpu.CompilerParams(
            dimension_semantics=("arbitrary",), vmem_limit_bytes=VMEM_LIMIT),
        name="gdn",
    )(d_ctx, d_lat, g_ctx, g_lat, z_ctx, z_lat, on_row, cum_f, cum_b, ones_m)


def _mix_mlp_kernel(ff_tile, x_ref, fy_ref, at_ref, dn_ref, g1_ref, sc_ref, sh_ref, g2_ref, ng_ref,
                    wof_ref, woa_ref, wod_ref, w1_ref, w2_ref, o_ref):
    mix = (_dot(fy_ref[0], wof_ref[...]) + _dot(at_ref[0], woa_ref[...])
           + _dot(dn_ref[0], wod_ref[...]))
    x1 = x_ref[0] + g1_ref[0] * mix
    ms = jnp.mean(x1 * x1, axis=-1, keepdims=True)
    gain = ng_ref[...] * (1.0 + sc_ref[0])
    h = (x1 * lax.rsqrt(ms + EPS) * gain + sh_ref[0]).astype(BF16)
    acc = jnp.zeros_like(x1)
    for c in range(w1_ref.shape[1] // ff_tile):
        u = jnp.maximum(_dot(h, w1_ref[:, c * ff_tile:(c + 1) * ff_tile]), 0.0)
        acc = acc + _dot((u * u).astype(BF16), w2_ref[c * ff_tile:(c + 1) * ff_tile, :])
    o_ref[0] = x1 + g2_ref[0] * acc


def _mix_mlp(x, fy, att, dn, g1, sc, sh, g2, norm_g, wts, tm):
    B, L, D = x.shape
    per_batch = g1.shape[0] > 1
    mod_map = (lambda b, i: (b, 0, 0)) if per_batch else (lambda b, i: (0, 0, 0))
    mod = pl.BlockSpec((1, 1, D), mod_map)
    full = lambda a: pl.BlockSpec(a.shape, lambda b, i: (0,) * a.ndim,
                                  pipeline_mode=pl.Buffered(1))
    row = lambda n: pl.BlockSpec((1, tm, n), lambda b, i: (b, i, 0))
    wof, woa, wod, w1, w2 = wts
    return pl.pallas_call(
        functools.partial(_mix_mlp_kernel, 1024),
        out_shape=jax.ShapeDtypeStruct((B, L, D), F32),
        grid=(B, L // tm),
        in_specs=[row(D), row(FT_CH), row(ATT_CH), row(DN_CH), mod, mod, mod, mod,
                  pl.BlockSpec(norm_g.shape, lambda b, i: (0, 0)),
                  full(wof), full(woa), full(wod), full(w1), full(w2)],
        out_specs=row(D),
        compiler_params=pltpu.CompilerParams(
            dimension_semantics=("arbitrary", "arbitrary"), vmem_limit_bytes=VMEM_LIMIT),
        name="mix_mlp",
    )(x, fy, att, dn, g1, sc, sh, g2, norm_g, wof, woa, wod, w1, w2)


def _rope_tables(S):
    rows = S // GRID_W
    t_row = jnp.repeat(jnp.arange(rows), GRID_W).astype(F32)
    t_col = jnp.tile(jnp.arange(GRID_W), rows).astype(F32)
    inv_freq = ROPE_THETA ** (-jnp.arange(ROT_PAIRS, dtype=F32) * 2.0 / AXIS_DIM)
    ang_r = t_row[:, None] * inv_freq
    ang_c = t_col[:, None] * inv_freq
    ang = jnp.concatenate([ang_r, ang_r, ang_c, ang_c], axis=-1)
    cos, sin = jnp.cos(ang), jnp.sin(ang)
    sign = jnp.where((jnp.arange(HEAD_DIM) & ROT_PAIRS) == 0, -1.0, 1.0).astype(F32)
    tile2 = lambda a: jnp.concatenate([a, a], axis=-1)
    return tile2(cos), tile2(sin * sign)


def _cos_sin(rows, cols, period):
    k = (jnp.arange(rows, dtype=jnp.int32)[:, None] * jnp.arange(cols, dtype=jnp.int32)[None, :]) % period
    ang = k.astype(F32) * np.float32(2.0 * np.pi / period)
    return jnp.cos(ang), jnp.sin(ang)


def _dft_tables(n):
    if n <= HEAD_DIM:
        return _cos_sin(n, n, n)
    assert n % HEAD_DIM == 0
    ca, sa = _cos_sin(n, n // HEAD_DIM, n // HEAD_DIM)
    cb, sb = _cos_sin(n, HEAD_DIM, n)
    cos = ca[:, :, None] * cb[:, None, :] - sa[:, :, None] * sb[:, None, :]
    sin = sa[:, :, None] * cb[:, None, :] + ca[:, :, None] * sb[:, None, :]
    return cos.reshape(n, n), sin.reshape(n, n)


def _channel_dft():
    c, s = _dft_tables(HEAD_DIM)
    eye = jnp.eye(FT_GROUPS, dtype=F32)
    return jnp.concatenate([jnp.kron(eye, c), jnp.kron(eye, s)], axis=1).astype(BF16)


def _gdn_consts():
    i = np.arange(GDN_TILE)
    blk = (i[:, None] // CHUNK) == (i[None, :] // CHUNK)
    cum_f = (blk & (i[:, None] >= i[None, :])).astype(np.float32)
    cum_b = (blk & (i[:, None] <= i[None, :])).astype(np.float32)
    return (jnp.asarray(cum_f, BF16), jnp.asarray(cum_b, BF16), jnp.asarray(blk.astype(np.float32), BF16))


def kernel(x, c, ctx, c_ctx, norm1_g, norm2_g, w_mod, b_mod, w_in, conv_w, q_norm_g, k_norm_g,
           a_log, dt_bias, o_norm_g, w_out, w_ff1, w_ff2):
    B, S, D = x.shape
    Lc = ctx.shape[1]
    depth = w_mod.shape[0]

    rows = ((B + 1 + 7) // 8) * 8
    c_all = jnp.concatenate([c, c_ctx[None, :], jnp.zeros((rows - B - 1, D), F32)], axis=0)
    mod_all = _modulation(c_all, w_mod, b_mod)

    cos_t, sin_t = _rope_tables(S)
    dft_ch = _channel_dft()
    dft_lat = tuple(t.astype(BF16) for t in _dft_tables(S))
    dft_ctx = tuple(t.astype(BF16) for t in _dft_tables(Lc))
    gdn_consts = _gdn_consts()

    o0 = FT_CH
    o1 = o0 + ATT_CH
    o2 = o1 + 2 * KV_CH
    o3 = o2 + CONV_CH
    o4 = o3 + DN_CH

    x_lat, x_ctx = x, ctx
    for l in range(depth):
        last = l == depth - 1
        w = w_in[l]
        wg = jnp.pad(w[:, o4:], ((0, 0), (0, LANES - N_GATES)))
        in_wts = (w[:, :o0].astype(BF16), dft_ch, w[:, o0:o1].astype(BF16), w[:, o1:o2].astype(BF16),
                  w[:, o2:o3].astype(BF16), w[:, o3:o4].astype(BF16), wg.astype(BF16))
        qg = jnp.tile(q_norm_g[l], ATT_HQ).reshape(1, ATT_CH)
        kg = jnp.tile(k_norm_g[l], ATT_HKV).reshape(1, KV_CH)
        n1 = norm1_g[l].reshape(1, D)
        n2 = norm2_g[l].reshape(1, D)
        wo = w_out[l].astype(BF16)
        out_wts = (wo[:FT_CH], wo[FT_CH:FT_CH + ATT_CH], wo[FT_CH + ATT_CH:],
                   w_ff1[l].astype(BF16), w_ff2[l].astype(BF16))
        alog_row = jnp.pad(a_log[l].reshape(1, -1), ((0, 0), (2 * DN_HEADS, LANES - N_GATES)))
        dtb_row = jnp.pad(dt_bias[l].reshape(1, -1), ((0, 0), (2 * DN_HEADS, LANES - N_GATES)))
        on_row = jnp.tile(o_norm_g[l], DN_HEADS).reshape(1, DN_CH)

        mod = mod_all[l, :B].reshape(B, 1, 6 * D)
        modc = mod_all[l, B:B + 1].reshape(1, 1, 6 * D)
        sh1, sc1, g1, sh2, sc2, g2 = [mod[:, :, i * D:(i + 1) * D] for i in range(6)]
        csh1, csc1, cg1, csh2, csc2, cg2 = [modc[:, :, i * D:(i + 1) * D] for i in range(6)]

        fl, ql, kl, vl, dl, zl, gl = _project(x_lat, sc1, sh1, n1, in_wts, qg, kg, cos_t, sin_t,
                                              conv_w[l], alog_row, dtb_row, True, min(S, 1024))
        fc, qc, kc, vc, dc, zc, gc = _project(x_ctx, csc1, csh1, n1, in_wts, qg, kg,
                                              cos_t[:Lc], sin_t[:Lc], conv_w[l], alog_row, dtb_row,
                                              False, Lc)

        dn_lat, dn_ctx = _gdn(dc, dl, gc, gl, zc, zl, on_row, gdn_consts, not last)
        att_lat = _attention(ql, [(kl, vl), (kc, vc)], min(S, 256))
        fy_lat = _fourier(fl, dft_lat[0], dft_lat[1], min(S, 1024))
        x_lat = _mix_mlp(x_lat, fy_lat, att_lat, dn_lat, g1, sc2, sh2, g2, n2, out_wts,
                         min(S, 1024))
        if not last:
            att_ctx = _attention(qc, [(kc, vc)], Lc)
            fy_ctx = _fourier(fc, dft_ctx[0], dft_ctx[1], Lc)
            x_ctx = _mix_mlp(x_ctx, fy_ctx, att_ctx, dn_ctx, cg1, csc2, csh2, cg2, n2, out_wts, Lc)
    return x_lat
```

```python
import functools

import jax
import jax.numpy as jnp
import numpy as np
from jax import lax
from jax.experimental import pallas as pl
from jax.experimental.pallas import tpu as pltpu

HEAD_DIM = 64
FT_GROUPS = 4
FT_CH = FT_GROUPS * HEAD_DIM
ATT_HQ = 8
ATT_HKV = 2
ATT_GROUP = ATT_HQ // ATT_HKV
ATT_CH = ATT_HQ * HEAD_DIM
KV_CH = ATT_HKV * HEAD_DIM
DN_HEADS = 4
DN_CH = DN_HEADS * HEAD_DIM
CONV_CH = 3 * DN_CH
N_GATES = 4 * DN_HEADS
GRID_W = 64
ROPE_THETA = 10000.0
AXIS_DIM = HEAD_DIM // 2
ROT_PAIRS = AXIS_DIM // 2
EPS = 1e-6
CHUNK = 64

LANES = 128
SUBLANES = 8
HALO_ROWS = SUBLANES
MXU_DIM = 256
V7X_VMEM_BYTES = 64 * 1024 * 1024
VMEM_LIMIT = V7X_VMEM_BYTES - 8 * 1024 * 1024

GDN_TILE = 256
CHUNKS_PER_TILE = GDN_TILE // CHUNK
NEG_BIG = -1e30
SCAN_STAGE_PERIOD = 1

F32 = jnp.float32
BF16 = jnp.bfloat16


def _dot(a, b):
    return jnp.dot(a, b, preferred_element_type=F32)


def _dot_nt(a, b):
    return lax.dot_general(a, b, (((1,), (1,)), ((), ())), preferred_element_type=F32)


def _split3(x):
    hi = x.astype(BF16)
    r1 = x - hi.astype(F32)
    mid = r1.astype(BF16)
    lo = (r1 - mid.astype(F32)).astype(BF16)
    return hi, mid, lo


def _dot_exact_lhs(m_bf16, x):
    hi, mid, lo = _split3(x)
    return _dot(m_bf16, hi) + _dot(m_bf16, mid) + _dot(m_bf16, lo)


def _seg64_sum(x, passes):
    w = x.shape[1]
    blk = min(w, MXU_DIM)
    r = lax.broadcasted_iota(jnp.int32, (blk, blk), 0) // HEAD_DIM
    c = lax.broadcasted_iota(jnp.int32, (blk, blk), 1) // HEAD_DIM
    ones = jnp.where(r == c, 1.0, 0.0).astype(BF16)
    outs = []
    for j in range(w // blk):
        xs = x[:, j * blk:(j + 1) * blk]
        hi = xs.astype(BF16)
        acc = _dot(hi, ones)
        if passes == 2:
            acc = acc + _dot((xs - hi.astype(F32)).astype(BF16), ones)
        outs.append(acc)
    return outs[0] if len(outs) == 1 else jnp.concatenate(outs, axis=1)


def _sigmoid(x):
    return 1.0 / (1.0 + jnp.exp(-x))


def _silu(x):
    return x * _sigmoid(x)


def _softplus(x):
    return jnp.maximum(x, 0.0) + jnp.log1p(jnp.exp(-jnp.abs(x)))


def _mod_kernel(c_ref, w_ref, b_ref, o_ref):
    a = _silu(c_ref[...])
    o_ref[0] = jnp.dot(a, w_ref[0], preferred_element_type=F32,
                       precision=lax.Precision.HIGHEST) + b_ref[0]


def _modulation(c_all, w_mod, b_mod):
    depth, d, n = w_mod.shape
    rows = c_all.shape[0]
    tn = 1024
    return pl.pallas_call(
        _mod_kernel,
        out_shape=jax.ShapeDtypeStruct((depth, rows, n), F32),
        grid=(depth, n // tn),
        in_specs=[pl.BlockSpec((rows, d), lambda l, j: (0, 0)),
                  pl.BlockSpec((1, d, tn), lambda l, j: (l, 0, j)),
                  pl.BlockSpec((1, 1, tn), lambda l, j: (l, 0, j))],
        out_specs=pl.BlockSpec((1, rows, tn), lambda l, j: (l, 0, j)),
        compiler_params=pltpu.CompilerParams(
            dimension_semantics=("arbitrary", "arbitrary"), vmem_limit_bytes=VMEM_LIMIT),
        name="modulation",
    )(c_all, w_mod, b_mod.reshape(depth, 1, n))


def _head_rms_rope(z, gain, cos, sin_signed, scale):
    ms = _seg64_sum(z * z, 1) * (1.0 / HEAD_DIM)
    y = z * lax.rsqrt(ms + EPS) * gain
    if scale != 1.0:
        y = y * scale
    if cos is None:
        return y
    lane = lax.broadcasted_iota(jnp.int32, cos.shape, 1)
    slabs = []
    for j in range(z.shape[1] // LANES):
        ys = y[:, j * LANES:(j + 1) * LANES]
        partner = jnp.where((lane & ROT_PAIRS) == 0,
                            pltpu.roll(ys, LANES - ROT_PAIRS, axis=1),
                            pltpu.roll(ys, ROT_PAIRS, axis=1))
        slabs.append(ys * cos + partner * sin_signed)
    return slabs[0] if len(slabs) == 1 else jnp.concatenate(slabs, axis=1)


def _conv_silu_norm(x, prev_row, next_row, w_ref):
    n = x.shape[0]
    row = lax.broadcasted_iota(jnp.int32, x.shape, 0)
    x_m1 = jnp.where(row == 0, prev_row, pltpu.roll(x, 1, axis=0))
    x_p1 = jnp.where(row == n - 1, next_row, pltpu.roll(x, n - 1, axis=0))
    y = _silu(x_m1 * w_ref[0:1, :] + x * w_ref[1:2, :] + x_p1 * w_ref[2:3, :])
    qk = y[:, :2 * DN_CH]
    qk = qk * lax.rsqrt(_seg64_sum(qk * qk, 1) + EPS)
    return jnp.concatenate([qk[:, :DN_CH] * (HEAD_DIM ** -0.5), qk[:, DN_CH:], y[:, 2 * DN_CH:]],
                           axis=1)


def _gates_to_beta_g(z, alog_ref, dtb_ref):
    lane = lax.broadcasted_iota(jnp.int32, z.shape, 1)
    g = -jnp.exp(alog_ref[...]) * _softplus(z + dtb_ref[...])
    return jnp.where(lane < 2 * DN_HEADS, _sigmoid(z), g)


def _project_kernel(use_rope, x_ref, xp_ref, xn_ref, sc_ref, sh_ref, g_ref, wf_ref, dft_ref, wq_ref,
                    wkv_ref, wd_ref, wz_ref, wg_ref, qg_ref, kg_ref, cos_ref, sin_ref, cw_ref,
                    alog_ref, dtb_ref, f_ref, q_ref, k_ref, v_ref, d_ref, z_ref, bg_ref):
    gain = g_ref[...] * (1.0 + sc_ref[0])

    def modulated(x):
        ms = jnp.mean(x * x, axis=-1, keepdims=True)
        return (x * lax.rsqrt(ms + EPS) * gain + sh_ref[0]).astype(BF16)

    h = modulated(x_ref[0])
    cos = cos_ref[...] if use_rope else None
    sin = sin_ref[...] if use_rope else None

    i = pl.program_id(1)
    tm = h.shape[0]
    h_halo = modulated(jnp.concatenate([xp_ref[0], xn_ref[0]], axis=0))
    zd_ext = _dot(jnp.concatenate([h, h_halo], axis=0), wd_ref[...])
    zd, zd_halo = zd_ext[:tm], zd_ext[tm:]
    halo_row = lax.broadcasted_iota(jnp.int32, zd_halo.shape, 0)
    prev_row = jnp.sum(jnp.where(halo_row == HALO_ROWS - 1, zd_halo, 0.0), axis=0, keepdims=True)
    next_row = jnp.sum(jnp.where(halo_row == HALO_ROWS, zd_halo, 0.0), axis=0, keepdims=True)
    prev_row = jnp.where(i == 0, 0.0, prev_row)
    next_row = jnp.where(i == pl.num_programs(1) - 1, 0.0, next_row)

    zq = _dot(h, wq_ref[...])
    d_ref[0] = _conv_silu_norm(zd, prev_row, next_row, cw_ref)
    zkv = _dot(h, wkv_ref[...])
    f = _dot(h, wf_ref[...]).astype(BF16)
    q_ref[0] = _head_rms_rope(zq, qg_ref[...], cos, sin, HEAD_DIM ** -0.5).astype(BF16)
    z_ref[0] = _dot(h, wz_ref[...])
    k_ref[0] = _head_rms_rope(zkv[:, :KV_CH], kg_ref[...], cos, sin, 1.0).astype(BF16)
    v_ref[0] = zkv[:, KV_CH:].astype(BF16)
    bg_ref[0] = _gates_to_beta_g(_dot(h, wg_ref[...]), alog_ref, dtb_ref)
    f_ref[0] = _dot(f, dft_ref[...]).astype(BF16)


def _project(x, sc, sh, norm_g, wts, qg, kg, cos_t, sin_t, conv_w, alog_row, dtb_row, use_rope, tm):
    B, L, D = x.shape
    per_batch = sc.shape[0] > 1
    mod_map = (lambda b, i: (b, 0, 0)) if per_batch else (lambda b, i: (0, 0, 0))
    full = lambda a: pl.BlockSpec(a.shape, lambda b, i: (0,) * a.ndim)
    row = lambda n: pl.BlockSpec((1, tm, n), lambda b, i: (b, i, 0))
    per = tm // HALO_ROWS
    last = L // HALO_ROWS - 1
    halo_prev = pl.BlockSpec((1, HALO_ROWS, D), lambda b, i: (b, jnp.maximum(i * per - 1, 0), 0))
    halo_next = pl.BlockSpec((1, HALO_ROWS, D), lambda b, i: (b, jnp.minimum((i + 1) * per, last), 0))
    wf, dft, wq, wkv, wd, wz, wg = wts
    out_shape = (jax.ShapeDtypeStruct((B, L, 2 * FT_CH), BF16),
                 jax.ShapeDtypeStruct((B, L, ATT_CH), BF16),
                 jax.ShapeDtypeStruct((B, L, KV_CH), BF16),
                 jax.ShapeDtypeStruct((B, L, KV_CH), BF16),
                 jax.ShapeDtypeStruct((B, L, CONV_CH), F32),
                 jax.ShapeDtypeStruct((B, L, DN_CH), F32),
                 jax.ShapeDtypeStruct((B, L, LANES), F32))
    return pl.pallas_call(
        functools.partial(_project_kernel, use_rope),
        out_shape=out_shape,
        grid=(B, L // tm),
        in_specs=[row(D), halo_prev, halo_next,
                  pl.BlockSpec((1, 1, D), mod_map), pl.BlockSpec((1, 1, D), mod_map),
                  full(norm_g), full(wf), full(dft), full(wq), full(wkv), full(wd), full(wz),
                  full(wg), full(qg), full(kg),
                  pl.BlockSpec((tm, LANES), lambda b, i: (i, 0)),
                  pl.BlockSpec((tm, LANES), lambda b, i: (i, 0)),
                  full(conv_w), full(alog_row), full(dtb_row)],
        out_specs=(row(2 * FT_CH), row(ATT_CH), row(KV_CH), row(KV_CH), row(CONV_CH),
                   row(DN_CH), row(LANES)),
        compiler_params=pltpu.CompilerParams(
            dimension_semantics=("arbitrary", "arbitrary"), vmem_limit_bytes=VMEM_LIMIT),
        name="project_rope" if use_rope else "project",
    )(x, x, x, sc, sh, norm_g, wf, dft, wq, wkv, wd, wz, wg, qg, kg, cos_t, sin_t,
      conv_w, alog_row, dtb_row)


def _fourier_kernel(scale, dc_ref, ds_ref, f_ref, o_ref):
    fcs = f_ref[0]
    y = _dot(dc_ref[...], fcs[:, :FT_CH]) - _dot(ds_ref[...], fcs[:, FT_CH:])
    o_ref[0] = (y * scale).astype(BF16)


def _fourier(fcs, dft_c, dft_s, tn):
    B, L, _ = fcs.shape
    scale = float(1.0 / np.sqrt(L * HEAD_DIM))
    return pl.pallas_call(
        functools.partial(_fourier_kernel, scale),
        out_shape=jax.ShapeDtypeStruct((B, L, FT_CH), BF16),
        grid=(L // tn, B),
        in_specs=[pl.BlockSpec((tn, L), lambda n, b: (n, 0)),
                  pl.BlockSpec((tn, L), lambda n, b: (n, 0)),
                  pl.BlockSpec((1, L, 2 * FT_CH), lambda n, b: (b, 0, 0))],
        out_specs=pl.BlockSpec((1, tn, FT_CH), lambda n, b: (b, n, 0)),
        compiler_params=pltpu.CompilerParams(
            dimension_semantics=("arbitrary", "arbitrary"), vmem_limit_bytes=VMEM_LIMIT),
        name="fourier",
    )(dft_c, dft_s, fcs)


def _attention_kernel(n_src, q_ref, *refs):
    kv_refs = refs[:2 * n_src]
    o_ref = refs[2 * n_src]
    tq = q_ref.shape[1]
    kv_heads = range(ATT_HKV)
    scores, row_max = [], []
    for h in kv_heads:
        q = q_ref[0, :, h * ATT_GROUP * HEAD_DIM:(h + 1) * ATT_GROUP * HEAD_DIM]
        q4 = jnp.concatenate([q[:, g * HEAD_DIM:(g + 1) * HEAD_DIM] for g in range(ATT_GROUP)],
                             axis=0)
        sc_h, m = [], None
        for s in range(n_src):
            k = kv_refs[2 * s][0, :, h * HEAD_DIM:(h + 1) * HEAD_DIM]
            sc = _dot_nt(q4, k)
            sc_h.append(sc)
            ms = jnp.max(sc, axis=-1, keepdims=True)
            m = ms if m is None else jnp.maximum(m, ms)
        scores.append(sc_h)
        row_max.append(m)
    accs = []
    for h in kv_heads:
        acc = None
        for s in range(n_src):
            v = kv_refs[2 * s + 1][0]
            lane = lax.broadcasted_iota(jnp.int32, v.shape, 1)
            in_head = (lane >= h * HEAD_DIM) & (lane < (h + 1) * HEAD_DIM)
            v_aug = jnp.where(in_head, v, jnp.ones_like(v))
            p = jnp.exp(scores[h][s] - row_max[h]).astype(BF16)
            pv = _dot(p, v_aug)
            acc = pv if acc is None else acc + pv
        accs.append(acc)
    for h in kv_heads:
        o = accs[h][:, h * HEAD_DIM:(h + 1) * HEAD_DIM]
        den = accs[h][:, (1 - h) * HEAD_DIM:(2 - h) * HEAD_DIM]
        o = (o / den).astype(BF16)
        for g in range(ATT_GROUP):
            c0 = (h * ATT_GROUP + g) * HEAD_DIM
            o_ref[0, :, c0:c0 + HEAD_DIM] = o[g * tq:(g + 1) * tq]


def _attention(q, kv_sources, tq):
    B, L, _ = q.shape
    n_src = len(kv_sources)
    in_specs = [pl.BlockSpec((1, tq, ATT_CH), lambda b, i: (b, i, 0))]
    args = [q]
    for k, v in kv_sources:
        lk = k.shape[1]
        in_specs += [pl.BlockSpec((1, lk, KV_CH), lambda b, i: (b, 0, 0)),
                     pl.BlockSpec((1, lk, KV_CH), lambda b, i: (b, 0, 0))]
        args += [k, v]
    return pl.pallas_call(
        functools.partial(_attention_kernel, n_src),
        out_shape=jax.ShapeDtypeStruct((B, L, ATT_CH), BF16),
        grid=(B, L // tq),
        in_specs=in_specs,
        out_specs=pl.BlockSpec((1, tq, ATT_CH), lambda b, i: (b, i, 0)),
        compiler_params=pltpu.CompilerParams(
            dimension_semantics=("arbitrary", "arbitrary"), vmem_limit_bytes=VMEM_LIMIT),
        name="attention",
    )(*args)


def _lane_col(x, c):
    lane = lax.broadcasted_iota(jnp.int32, x.shape, 1)
    return jnp.sum(jnp.where(lane == c, x, 0.0), axis=1, keepdims=True)


def _head_bcast(cols, lane_head):
    out = cols[DN_HEADS - 1]
    for h in range(DN_HEADS - 2, -1, -1):
        out = jnp.where(lane_head <= h, cols[h], out)
    return out


def _level_mask(ri, ci, b):
    return ((ri // (2 * b)) == (ci // (2 * b))) & ((ri // b) != (ci // b))


def _gdn_prep(direction, qkv, bg, cum_ref, ones_ref, pu_ref, pwq_ref, pk_ref, pe_ref):
    cum_m = cum_ref[...]
    ones_m = ones_ref[...]
    ri = lax.broadcasted_iota(jnp.int32, (GDN_TILE, GDN_TILE), 0)
    ci = lax.broadcasted_iota(jnp.int32, (GDN_TILE, GDN_TILE), 1)
    blk = (ri // CHUNK) == (ci // CHUNK)
    lane_head = ci // HEAD_DIM
    if direction == 0:
        tri_strict, tri_incl = blk & (ri > ci), blk & (ri >= ci)
    else:
        tri_strict, tri_incl = blk & (ri < ci), blk & (ri <= ci)
    q = qkv[:, 0:DN_CH]
    k = qkv[:, DN_CH:2 * DN_CH]
    v = qkv[:, 2 * DN_CH:3 * DN_CH]

    gc = _dot_exact_lhs(cum_m, bg)
    gt = _dot_exact_lhs(ones_m, bg)
    gc_t = gc.T
    row_t = lax.broadcasted_iota(jnp.int32, gc_t.shape, 0)

    beta_cols, gc_cols, gt_cols = [], [], []
    for h in range(DN_HEADS):
        beta_cols.append(_lane_col(bg, direction * DN_HEADS + h))
        gc_cols.append(_lane_col(gc, 2 * DN_HEADS + direction * DN_HEADS + h))
        gt_cols.append(_lane_col(gt, 2 * DN_HEADS + direction * DN_HEADS + h))
    beta_b = _head_bcast(beta_cols, lane_head)
    gc_b = _head_bcast(gc_cols, lane_head)
    gt_b = _head_bcast(gt_cols, lane_head)

    e_gc = jnp.exp(gc_b)
    kb = k * beta_b
    vb = v * beta_b
    kbe = kb * e_gc
    qe = q * e_gc
    kd = k * jnp.exp(gt_b - gc_b)
    pe_ref[...] = jnp.exp(gt_b)
    pk_ref[0] = kd.T.astype(BF16)
    pwq_ref[1] = qe.astype(BF16)

    k16 = k.astype(BF16)
    rhs_uw = jnp.concatenate([vb, kbe], axis=1).astype(BF16)
    eye = (ri == ci).astype(F32)
    yield

    heads = range(DN_HEADS)
    levels = [2 ** e for e in range(1, CHUNK.bit_length() - 1)]
    neg_a, t = [], []
    for h in heads:
        in_h = lane_head == h
        r = jnp.sum(jnp.where(row_t == 2 * DN_HEADS + direction * DN_HEADS + h, gc_t, 0.0),
                    axis=0, keepdims=True)
        diff = gc_cols[h] - r
        dec_i = jnp.exp(jnp.where(tri_incl, diff, NEG_BIG))
        dec_s = jnp.where(tri_strict, dec_i, 0.0)
        kk = _dot_nt(jnp.where(in_h, kb, 0.0).astype(BF16), k16)
        qk = _dot_nt(jnp.where(in_h, q, 0.0).astype(BF16), k16)
        p = -(kk * dec_s)
        neg_a.append([jnp.where(_level_mask(ri, ci, b), p, 0.0).astype(BF16) for b in levels])
        t.append((eye + jnp.where((ri // 2) == (ci // 2), p, 0.0)).astype(BF16))
        pk_ref[1 + h] = (qk * dec_i).astype(BF16)
        yield

    for lvl in range(len(levels)):
        g = [_dot(neg_a[h][lvl], t[h]) for h in heads]
        yield
        t = [t[h] + _dot(t[h], g[h].astype(BF16)).astype(BF16) for h in heads]
        yield
    uw = [_dot(t[h], rhs_uw) for h in heads]
    u_all, w_all = uw[DN_HEADS - 1][:, :DN_CH], uw[DN_HEADS - 1][:, DN_CH:]
    for h in range(DN_HEADS - 2, -1, -1):
        u_all = jnp.where(lane_head <= h, uw[h][:, :DN_CH], u_all)
        w_all = jnp.where(lane_head <= h, uw[h][:, DN_CH:], w_all)
    pu_ref[...] = u_all
    pwq_ref[0] = w_all.astype(BF16)


def _gdn_scan(direction, s_ref, pu_ref, pwq_ref, pk_ref, pe_ref, o_ref, o_rows):
    ri = lax.broadcasted_iota(jnp.int32, (GDN_TILE, GDN_TILE), 0)
    ci = lax.broadcasted_iota(jnp.int32, (GDN_TILE, GDN_TILE), 1)
    blk = (ri // HEAD_DIM) == (ci // HEAD_DIM)
    order = range(CHUNKS_PER_TILE) if direction == 0 else range(CHUNKS_PER_TILE - 1, -1, -1)
    outs = [None] * CHUNKS_PER_TILE
    zeros_c = jnp.zeros((CHUNK, DN_CH), BF16)
    lane_head_c = lax.broadcasted_iota(jnp.int32, (CHUNK, DN_CH), 1) // HEAD_DIM
    for c in order:
        rows = pl.ds(c * CHUNK, CHUNK)
        s = s_ref[...]
        wq = jnp.concatenate([pwq_ref[0, rows, :], pwq_ref[1, rows, :]], axis=0)
        ws = _dot(wq, s.astype(BF16))
        yield
        v_new = (pu_ref[rows, :] - ws[:CHUNK]).astype(BF16)
        v_tile = jnp.concatenate([v_new if i == c else zeros_c for i in range(CHUNKS_PER_TILE)],
                                 axis=0)
        lhs = jnp.concatenate([pk_ref[0]] + [pk_ref[1 + h, rows, :] for h in range(DN_HEADS)],
                              axis=0)
        r2 = _dot(lhs, v_tile)
        s_ref[...] = s * pe_ref[pl.ds(c * CHUNK, 1), :] + jnp.where(blk, r2[:GDN_TILE], 0.0)
        o = ws[CHUNK:]
        for h in range(DN_HEADS):
            o = o + jnp.where(lane_head_c == h,
                              r2[GDN_TILE + h * CHUNK:GDN_TILE + (h + 1) * CHUNK], 0.0)
        outs[c] = o
        yield
    o_ref[o_rows, :] = jnp.concatenate(outs, axis=0)


def _trace_interleaved(stages):
    live = [[g, p] for g, p in stages]
    r = 0
    while live:
        for item in list(live):
            if r % item[1] == 0 and next(item[0], StopIteration) is StopIteration:
                live.remove(item)
        r += 1


def _gdn_kernel(n_lat, write_ctx, dc_ref, dl_ref, gc_ref, gl_ref, zc_ref, zl_ref, on_ref,
                cum_f_ref, cum_b_ref, ones_ref, ol_ref, oc_ref, of_ref, ob_ref, sf_ref, sb_ref,
                pu_ref, pwq_ref, pk_ref, pe_ref):
    T = GDN_TILE
    sf_ref[...] = jnp.zeros_like(sf_ref)
    sb_ref[...] = jnp.zeros_like(sb_ref)

    s_refs = (sf_ref, sb_ref)
    cum_refs = (cum_f_ref, cum_b_ref)
    o_refs = (of_ref, ob_ref)

    def tile_of(direction, s):
        if isinstance(s, int):
            return s if (direction == 0 or s == 0) else n_lat + 1 - s
        return s if direction == 0 else jnp.where(s == 0, 0, n_lat + 1 - s)

    def tile_rows(direction, s):
        t = tile_of(direction, s)
        return pl.ds(t * T, T) if isinstance(t, int) else pl.ds(pl.multiple_of(t * T, T), T)

    def tile_inputs(direction, s):
        if isinstance(s, int) and s == 0:
            return dc_ref[0], gc_ref[0]
        t = tile_of(direction, s)
        rows = (pl.ds((t - 1) * T, T) if isinstance(t, int)
                else pl.ds(pl.multiple_of((t - 1) * T, T), T))
        return dl_ref[0, rows, :], gl_ref[0, rows, :]

    def prep_stages(s, slot):
        return [(_gdn_prep(d, *tile_inputs(d, s), cum_refs[d], ones_ref, pu_ref.at[slot, d],
                           pwq_ref.at[slot, d], pk_ref.at[slot, d], pe_ref.at[slot, d]), 1)
                for d in range(2)]

    def scan_stages(s, slot):
        return [(_gdn_scan(d, s_refs[d], pu_ref.at[slot, d], pwq_ref.at[slot, d],
                           pk_ref.at[slot, d], pe_ref.at[slot, d], o_refs[d], tile_rows(d, s)),
                 SCAN_STAGE_PERIOD) for d in range(2)]

    _trace_interleaved(prep_stages(0, 0))

    def scan_pair(j, carry):
        s = 2 * j
        _trace_interleaved(scan_stages(s, 0) + prep_stages(s + 1, 1))
        _trace_interleaved(scan_stages(s + 1, 1) + prep_stages(s + 2, 0))
        return carry

    lax.fori_loop(0, n_lat // 2, scan_pair, 0)

    def finish(o, z):
        ms = _seg64_sum(o * o, 2) * (1.0 / HEAD_DIM)
        return (o * lax.rsqrt(ms + EPS) * on_ref[...] * _silu(z)).astype(BF16)

    def finish_stages(tiles):
        for t in tiles:
            if t == 0:
                if write_ctx:
                    oc_ref[0] = finish(of_ref[0:T] + ob_ref[0:T], zc_ref[0])
                else:
                    oc_ref[0] = jnp.zeros(oc_ref.shape[1:], BF16)
            else:
                rows = pl.ds(t * T, T)
                ol_ref[0, pl.ds((t - 1) * T, T), :] = finish(of_ref[rows, :] + ob_ref[rows, :],
                                                             zl_ref[0, pl.ds((t - 1) * T, T), :])
            yield

    last_tiles = sorted({1, n_lat})
    _trace_interleaved(scan_stages(n_lat, 0)
                       + [(finish_stages([t for t in range(n_lat + 1) if t not in last_tiles]), 1)])
    _trace_interleaved([(finish_stages(last_tiles), 1)])


def _gdn(d_ctx, d_lat, g_ctx, g_lat, z_ctx, z_lat, on_row, consts, write_ctx):
    B, L, _ = d_lat.shape
    Lc = d_ctx.shape[1]
    assert Lc == GDN_TILE and L % (2 * GDN_TILE) == 0
    n_lat = L // GDN_TILE
    T = GDN_TILE
    cum_f, cum_b, ones_m = consts
    per_b = lambda n, c: pl.BlockSpec((1, n, c), lambda b: (b, 0, 0))
    full = lambda a: pl.BlockSpec(a.shape, lambda b: (0,) * a.ndim)
    tot = L + Lc
    return pl.pallas_call(
        functools.partial(_gdn_kernel, n_lat, write_ctx),
        out_shape=(jax.ShapeDtypeStruct((B, L, DN_CH), BF16),
                   jax.ShapeDtypeStruct((B, Lc, DN_CH), BF16)),
        grid=(B,),
        in_specs=[per_b(Lc, CONV_CH), per_b(L, CONV_CH), per_b(Lc, LANES), per_b(L, LANES),
                  per_b(Lc, DN_CH), per_b(L, DN_CH), full(on_row), full(cum_f), full(cum_b),
                  full(ones_m)],
        out_specs=(per_b(L, DN_CH), per_b(Lc, DN_CH)),
        scratch_shapes=[pltpu.VMEM((tot, DN_CH), F32), pltpu.VMEM((tot, DN_CH), F32),
                        pltpu.VMEM((T, DN_CH), F32), pltpu.VMEM((T, DN_CH), F32),
                        pltpu.VMEM((2, 2, T, DN_CH), F32), pltpu.VMEM((2, 2, 2, T, DN_CH), BF16),
                        pltpu.VMEM((2, 2, 1 + DN_HEADS, T, T), BF16), pltpu.VMEM((2, 2, T, DN_CH), F32)],
        compiler_params=pltpu.CompilerParams(
            dimension_semantics=("arbitrary",), vmem_limit_bytes=VMEM_LIMIT),
        name="gdn",
    )(d_ctx, d_lat, g_ctx, g_lat, z_ctx, z_lat, on_row, cum_f, cum_b, ones_m)


def _mix_mlp_kernel(ff_tile, x_ref, fy_ref, at_ref, dn_ref, g1_ref, sc_ref, sh_ref, g2_ref, ng_ref,
                    wof_ref, woa_ref, wod_ref, w1_ref, w2_ref, o_ref):
    mix = (_dot(fy_ref[0], wof_ref[...]) + _dot(at_ref[0], woa_ref[...])
           + _dot(dn_ref[0], wod_ref[...]))
    x1 = x_ref[0] + g1_ref[0] * mix
    ms = jnp.mean(x1 * x1, axis=-1, keepdims=True)
    gain = ng_ref[...] * (1.0 + sc_ref[0])
    h = (x1 * lax.rsqrt(ms + EPS) * gain + sh_ref[0]).astype(BF16)
    acc = jnp.zeros_like(x1)
    for c in range(w1_ref.shape[1] // ff_tile):
        u = jnp.maximum(_dot(h, w1_ref[:, c * ff_tile:(c + 1) * ff_tile]), 0.0)
        acc = acc + _dot((u * u).astype(BF16), w2_ref[c * ff_tile:(c + 1) * ff_tile, :])
    o_ref[0] = x1 + g2_ref[0] * acc


def _mix_mlp(x, fy, att, dn, g1, sc, sh, g2, norm_g, wts, layer, tm):
    B, L, D = x.shape
    per_batch = g1.shape[0] > 1
    mod_map = (lambda b, i: (b, 0, 0)) if per_batch else (lambda b, i: (0, 0, 0))
    mod = pl.BlockSpec((1, 1, D), mod_map)
    full = lambda a: pl.BlockSpec(a.shape, lambda b, i: (0,) * a.ndim,
                                  pipeline_mode=pl.Buffered(1))
    of_layer = lambda a: pl.BlockSpec((None,) + a.shape[1:], lambda b, i: (layer, 0, 0),
                                      pipeline_mode=pl.Buffered(1))
    row = lambda n: pl.BlockSpec((1, tm, n), lambda b, i: (b, i, 0))
    wof, woa, wod, w1, w2 = wts
    return pl.pallas_call(
        functools.partial(_mix_mlp_kernel, 1024),
        out_shape=jax.ShapeDtypeStruct((B, L, D), F32),
        grid=(B, L // tm),
        in_specs=[row(D), row(FT_CH), row(ATT_CH), row(DN_CH), mod, mod, mod, mod,
                  pl.BlockSpec(norm_g.shape, lambda b, i: (0, 0)),
                  full(wof), full(woa), full(wod), of_layer(w1), of_layer(w2)],
        out_specs=row(D),
        compiler_params=pltpu.CompilerParams(
            dimension_semantics=("arbitrary", "arbitrary"), vmem_limit_bytes=VMEM_LIMIT),
        name="mix_mlp",
    )(x, fy, att, dn, g1, sc, sh, g2, norm_g, wof, woa, wod, w1, w2)


def _rope_tables(S):
    rows = S // GRID_W
    t_row = jnp.repeat(jnp.arange(rows), GRID_W).astype(F32)
    t_col = jnp.tile(jnp.arange(GRID_W), rows).astype(F32)
    inv_freq = ROPE_THETA ** (-jnp.arange(ROT_PAIRS, dtype=F32) * 2.0 / AXIS_DIM)
    ang_r = t_row[:, None] * inv_freq
    ang_c = t_col[:, None] * inv_freq
    ang = jnp.concatenate([ang_r, ang_r, ang_c, ang_c], axis=-1)
    cos, sin = jnp.cos(ang), jnp.sin(ang)
    sign = jnp.where((jnp.arange(HEAD_DIM) & ROT_PAIRS) == 0, -1.0, 1.0).astype(F32)
    tile2 = lambda a: jnp.concatenate([a, a], axis=-1)
    return tile2(cos), tile2(sin * sign)


def _cos_sin(rows, cols, period):
    k = (jnp.arange(rows, dtype=jnp.int32)[:, None] * jnp.arange(cols, dtype=jnp.int32)[None, :]) % period
    ang = k.astype(F32) * np.float32(2.0 * np.pi / period)
    return jnp.cos(ang), jnp.sin(ang)


def _dft_tables(n):
    if n <= HEAD_DIM:
        return _cos_sin(n, n, n)
    assert n % HEAD_DIM == 0
    ca, sa = _cos_sin(n, n // HEAD_DIM, n // HEAD_DIM)
    cb, sb = _cos_sin(n, HEAD_DIM, n)
    cos = ca[:, :, None] * cb[:, None, :] - sa[:, :, None] * sb[:, None, :]
    sin = sa[:, :, None] * cb[:, None, :] + ca[:, :, None] * sb[:, None, :]
    return cos.reshape(n, n), sin.reshape(n, n)


def _channel_dft():
    c, s = _dft_tables(HEAD_DIM)
    eye = jnp.eye(FT_GROUPS, dtype=F32)
    return jnp.concatenate([jnp.kron(eye, c), jnp.kron(eye, s)], axis=1).astype(BF16)


def _gdn_consts():
    i = np.arange(GDN_TILE)
    blk = (i[:, None] // CHUNK) == (i[None, :] // CHUNK)
    cum_f = (blk & (i[:, None] >= i[None, :])).astype(np.float32)
    cum_b = (blk & (i[:, None] <= i[None, :])).astype(np.float32)
    return (jnp.asarray(cum_f, BF16), jnp.asarray(cum_b, BF16), jnp.asarray(blk.astype(np.float32), BF16))


def kernel(x, c, ctx, c_ctx, norm1_g, norm2_g, w_mod, b_mod, w_in, conv_w, q_norm_g, k_norm_g,
           a_log, dt_bias, o_norm_g, w_out, w_ff1, w_ff2):
    B, S, D = x.shape
    Lc = ctx.shape[1]
    depth = w_mod.shape[0]

    rows = ((B + 1 + 7) // 8) * 8
    c_all = jnp.concatenate([c, c_ctx[None, :], jnp.zeros((rows - B - 1, D), F32)], axis=0)
    mod_all = _modulation(c_all, w_mod, b_mod)

    cos_t, sin_t = _rope_tables(S)
    dft_ch = _channel_dft()
    dft_lat = tuple(t.astype(BF16) for t in _dft_tables(S))
    dft_ctx = tuple(t.astype(BF16) for t in _dft_tables(Lc))
    gdn_consts = _gdn_consts()
    w_ff1_16 = w_ff1.astype(BF16)
    w_ff2_16 = w_ff2.astype(BF16)

    o0 = FT_CH
    o1 = o0 + ATT_CH
    o2 = o1 + 2 * KV_CH
    o3 = o2 + CONV_CH
    o4 = o3 + DN_CH

    x_lat, x_ctx = x, ctx
    for l in range(depth):
        last = l == depth - 1
        w = w_in[l]
        wg = jnp.pad(w[:, o4:], ((0, 0), (0, LANES - N_GATES)))
        in_wts = (w[:, :o0].astype(BF16), dft_ch, w[:, o0:o1].astype(BF16), w[:, o1:o2].astype(BF16),
                  w[:, o2:o3].astype(BF16), w[:, o3:o4].astype(BF16), wg.astype(BF16))
        qg = jnp.tile(q_norm_g[l], ATT_HQ).reshape(1, ATT_CH)
        kg = jnp.tile(k_norm_g[l], ATT_HKV).reshape(1, KV_CH)
        n1 = norm1_g[l].reshape(1, D)
        n2 = norm2_g[l].reshape(1, D)
        wo = w_out[l].astype(BF16)
        out_wts = (wo[:FT_CH], wo[FT_CH:FT_CH + ATT_CH], wo[FT_CH + ATT_CH:], w_ff1_16, w_ff2_16)
        alog_row = jnp.pad(a_log[l].reshape(1, -1), ((0, 0), (2 * DN_HEADS, LANES - N_GATES)))
        dtb_row = jnp.pad(dt_bias[l].reshape(1, -1), ((0, 0), (2 * DN_HEADS, LANES - N_GATES)))
        on_row = jnp.tile(o_norm_g[l], DN_HEADS).reshape(1, DN_CH)

        mod = mod_all[l, :B].reshape(B, 1, 6 * D)
        modc = mod_all[l, B:B + 1].reshape(1, 1, 6 * D)
        sh1, sc1, g1, sh2, sc2, g2 = [mod[:, :, i * D:(i + 1) * D] for i in range(6)]
        csh1, csc1, cg1, csh2, csc2, cg2 = [modc[:, :, i * D:(i + 1) * D] for i in range(6)]

        fl, ql, kl, vl, dl, zl, gl = _project(x_lat, sc1, sh1, n1, in_wts, qg, kg, cos_t, sin_t,
                                              conv_w[l], alog_row, dtb_row, True, min(S, 1024))
        fc, qc, kc, vc, dc, zc, gc = _project(x_ctx, csc1, csh1, n1, in_wts, qg, kg,
                                              cos_t[:Lc], sin_t[:Lc], conv_w[l], alog_row, dtb_row,
                                              False, Lc)

        dn_lat, dn_ctx = _gdn(dc, dl, gc, gl, zc, zl, on_row, gdn_consts, not last)
        att_lat = _attention(ql, [(kl, vl), (kc, vc)], min(S, 256))
        fy_lat = _fourier(fl, dft_lat[0], dft_lat[1], min(S, 1024))
        x_lat = _mix_mlp(x_lat, fy_lat, att_lat, dn_lat, g1, sc2, sh2, g2, n2, out_wts, l,
                         min(S, 1024))
        if not last:
            att_ctx = _attention(qc, [(kc, vc)], Lc)
            fy_ctx = _fourier(fc, dft_ctx[0], dft_ctx[1], Lc)
            x_ctx = _mix_mlp(x_ctx, fy_ctx, att_ctx, dn_ctx, cg1, csc2, csh2, cg2, n2, out_wts, l,
                             Lc)
    return x_lat
```
